```python
import math
import jax
import jax.numpy as jnp
from jax import lax
import numpy as np

D_MODEL = 1024
BATCH = 4
SEQ = 4096
DEPTH = 2
DEC_BATCH = 128
DEC_SEQ = 1
PAST_LEN = 2048
PAGE_SIZE = 128

D_MIX = D_MODEL
D_GROUP = D_MIX // 4
HEAD_DIM = D_GROUP // 4
H_FOX = D_GROUP // HEAD_DIM
H_NSA = D_GROUP // HEAD_DIM
G_NSA = 2
H_MEM = D_GROUP // HEAD_DIM
D_FOX = H_FOX * HEAD_DIM
D_NSA = H_NSA * HEAD_DIM
D_POOL = D_GROUP
D_MEMX = H_MEM * HEAD_DIM
KV_NSA = G_NSA * HEAD_DIM
POOL_WINDOWS = (2, 4, 8, 16)
N_POOL_GROUPS = len(POOL_WINDOWS)
POOL_GW = D_POOL // N_POOL_GROUPS
POOL_BUF = max(POOL_WINDOWS) - 1
N_MEM = 256
NSA_BLOCK = 64
NSA_TOPN = 16
NSA_WINDOW = 512
NSA_CMP_HID = 256
Q_BLOCK = 128
FGT_BIAS_MEAN = 3.0
RMS_EPS = 1e-6
NEG_INF = -1e30
FORCE_SCORE = 1e4
D_IN = 3 * D_FOX + H_FOX + D_NSA + 6 * KV_NSA + 3 * H_NSA + D_POOL + D_MEMX + D_MIX

kernel_name = "hymba_fox_nsa_pool_decoder_step"

F32 = jnp.float32


def rms_norm(x, g):
    xf = x.astype(F32)
    y = xf * lax.rsqrt(jnp.mean(xf * xf, axis=-1, keepdims=True) + RMS_EPS)
    return (y * g.astype(F32)).astype(x.dtype)


def masked_softmax(s, mask):
    s = jnp.where(mask, s, NEG_INF)
    e = jnp.where(mask, jnp.exp(s - jnp.max(s, axis=-1, keepdims=True)), 0.0)
    return e / jnp.maximum(jnp.sum(e, axis=-1, keepdims=True), 1e-30)


def alibi_slopes(n):
    return jnp.asarray([2.0 ** (-8.0 * (h + 1) / n) for h in range(n)], F32)


def split_projection(p):
    B, T = p.shape[:2]
    sizes = (3 * D_FOX, H_FOX, D_NSA, 6 * KV_NSA, 3 * H_NSA, D_POOL, D_MEMX, D_MIX)
    pts, acc = [], 0
    for s in sizes[:-1]:
        acc += s
        pts.append(acc)
    fox_qkv, fox_f, nsa_q, nsa_kv, nsa_g, pool_u, mem_q, z = jnp.split(p, pts, axis=-1)
    return (fox_qkv.reshape(B, T, 3, H_FOX, HEAD_DIM), fox_f,
            nsa_q.reshape(B, T, H_NSA, HEAD_DIM), nsa_kv.reshape(B, T, 6, G_NSA, HEAD_DIM),
            nsa_g.reshape(B, T, H_NSA, 3), pool_u, mem_q.reshape(B, T, H_MEM, HEAD_DIM), z)


def fox_attention(q, qpos, cq, k, v, kpos, ck):
    s = jnp.einsum('bqhd,bkhd->bhqk', q, k).astype(F32) * (HEAD_DIM ** -0.5)
    s = s + jnp.swapaxes(cq, 1, 2)[:, :, :, None] - jnp.swapaxes(ck, 1, 2)[:, :, None, :]
    mask = (kpos[None, :] <= qpos[:, None])[None, None]
    p = masked_softmax(s, mask)
    return jnp.einsum('bhqk,bkhd->bqhd', p.astype(v.dtype), v)


def fox_prompt(q, k, v, lf):
    B, T = q.shape[:2]
    c = jnp.cumsum(lf, axis=1)
    kpos = jnp.arange(T)

    def block(i):
        q0 = i * Q_BLOCK
        qb = lax.dynamic_slice_in_dim(q, q0, Q_BLOCK, axis=1)
        cb = lax.dynamic_slice_in_dim(c, q0, Q_BLOCK, axis=1)
        return fox_attention(qb, q0 + jnp.arange(Q_BLOCK), cb, k, v, kpos, c)

    o = lax.map(block, jnp.arange(T // Q_BLOCK))
    return jnp.swapaxes(o, 0, 1).reshape(B, T, H_FOX * HEAD_DIM)


def nsa_compress(rows, pe, w1, w2):
    B, L, G, dh = rows.shape
    nb = L // NSA_BLOCK
    blk = rows.reshape(B, nb, NSA_BLOCK, G, dh) + pe[:, None, :].astype(rows.dtype)
    flat = jnp.transpose(blk, (0, 1, 3, 2, 4)).reshape(B, nb, G, NSA_BLOCK * dh)
    hid = jax.nn.silu(jnp.einsum('bngf,fe->bnge', flat, w1))
    return jnp.einsum('bnge,ed->bngd', hid, w2)


def nsa_attention(q, qpos, gates, k_cmp, v_cmp, k_sel, v_sel, k_win, v_win, wpos):
    B, Q = q.shape[:2]
    R = H_NSA // G_NSA
    nb = k_cmp.shape[1]
    scale = HEAD_DIM ** -0.5
    slopes = alibi_slopes(H_NSA).reshape(G_NSA, R)[None, None, :, :, None]
    qg = q.reshape(B, Q, G_NSA, R, HEAD_DIM)
    blk = jnp.arange(nb)
    cmp_end = blk * NSA_BLOCK + (NSA_BLOCK - 1)
    s_c = jnp.einsum('bqgrd,bngd->bqgrn', qg, k_cmp).astype(F32) * scale
    s_c = s_c - slopes * (qpos[:, None] - cmp_end[None, :]).astype(F32)[None, :, None, None, :]
    m_c = (cmp_end[None, :] <= qpos[:, None])[None, :, None, None, :]
    p_c = masked_softmax(s_c, m_c)
    o_c = jnp.einsum('bqgrn,bngd->bqgrd', p_c.astype(v_cmp.dtype), v_cmp)
    cur = qpos // NSA_BLOCK
    imp = jnp.sum(p_c, axis=3)
    forced = (blk[None, :] == 0) | (blk[None, :] == cur[:, None]) | (blk[None, :] == cur[:, None] - 1)
    future = blk[None, :] > cur[:, None]
    score = jnp.where(future[None, :, None, :], -1.0,
                      jnp.where(forced[None, :, None, :], FORCE_SCORE, imp))
    top_val, top_idx = lax.top_k(score, min(NSA_TOPN, nb))
    n = top_idx.shape[-1]
    k_blk = jnp.transpose(k_sel.reshape(B, nb, NSA_BLOCK, G_NSA, HEAD_DIM), (0, 3, 1, 2, 4))
    v_blk = jnp.transpose(v_sel.reshape(B, nb, NSA_BLOCK, G_NSA, HEAD_DIM), (0, 3, 1, 2, 4))
    bi = jnp.arange(B)[:, None, None, None]
    gi = jnp.arange(G_NSA)[None, None, :, None]
    k_g = k_blk[bi, gi, top_idx]
    v_g = v_blk[bi, gi, top_idx]
    spos = top_idx[..., None] * NSA_BLOCK + jnp.arange(NSA_BLOCK)
    m_s = (top_val >= 0)[..., None] & (spos <= qpos[None, :, None, None, None])
    s_s = jnp.einsum('bqgrd,bqgnsd->bqgrns', qg, k_g).astype(F32) * scale
    s_s = s_s - slopes[..., None] * (qpos[None, :, None, None, None] - spos).astype(F32)[:, :, :, None]
    p_s = masked_softmax(s_s.reshape(B, Q, G_NSA, R, n * NSA_BLOCK),
                         m_s[:, :, :, None].reshape(B, Q, G_NSA, 1, n * NSA_BLOCK))
    o_s = jnp.einsum('bqgrns,bqgnsd->bqgrd',
                     p_s.reshape(B, Q, G_NSA, R, n, NSA_BLOCK).astype(v_g.dtype), v_g)
    dist = qpos[:, None] - wpos[None, :]
    s_w = jnp.einsum('bqgrd,bkgd->bqgrk', qg, k_win).astype(F32) * scale
    s_w = s_w - slopes * dist.astype(F32)[None, :, None, None, :]
    m_w = ((dist >= 0) & (dist <= NSA_WINDOW) & (wpos[None, :] >= 0))[None, :, None, None, :]
    p_w = masked_softmax(s_w, m_w)
    o_w = jnp.einsum('bqgrk,bkgd->bqgrd', p_w.astype(v_win.dtype), v_win)
    g = jax.nn.sigmoid(gates.astype(F32)).reshape(B, Q, G_NSA, R, 3)
    o = g[..., 0:1] * o_c + g[..., 1:2] * o_s + g[..., 2:3] * o_w
    return o.reshape(B, Q, H_NSA * HEAD_DIM).astype(q.dtype)


def nsa_prompt(q, kv, gates, pe, w1, w2):
    B, T = q.shape[:2]
    k_cmp = nsa_compress(kv[:, :, 0], pe[0], w1[0], w2[0])
    v_cmp = nsa_compress(kv[:, :, 1], pe[1], w1[1], w2[1])
    k_sel, v_sel = kv[:, :, 2], kv[:, :, 3]
    pad = ((0, 0), (NSA_WINDOW, 0), (0, 0), (0, 0))
    k_w = jnp.pad(kv[:, :, 4], pad)
    v_w = jnp.pad(kv[:, :, 5], pad)
    lw = NSA_WINDOW + Q_BLOCK

    def block(i):
        q0 = i * Q_BLOCK
        qpos = q0 + jnp.arange(Q_BLOCK)
        wpos = q0 - NSA_WINDOW + jnp.arange(lw)
        return nsa_attention(lax.dynamic_slice_in_dim(q, q0, Q_BLOCK, axis=1), qpos,
                             lax.dynamic_slice_in_dim(gates, q0, Q_BLOCK, axis=1),
                             k_cmp, v_cmp, k_sel, v_sel,
                             lax.dynamic_slice_in_dim(k_w, q0, lw, axis=1),
                             lax.dynamic_slice_in_dim(v_w, q0, lw, axis=1), wpos)

    o = lax.map(block, jnp.arange(T // Q_BLOCK))
    return jnp.swapaxes(o, 0, 1).reshape(B, T, H_NSA * HEAD_DIM)


def nsa_sample(q, kv_new, gates, kv_past, win_buf, pe, w1, w2):
    DB, DS = q.shape[:2]
    past = kv_past.shape[1]
    L = past + DS
    Lp = -(-L // NSA_BLOCK) * NSA_BLOCK
    rows = jnp.concatenate([kv_past, kv_new[:, :, :4]], axis=1)
    rows = jnp.pad(rows, ((0, 0), (0, Lp - L), (0, 0), (0, 0), (0, 0)))
    k_cmp = nsa_compress(rows[:, :, 0], pe[0], w1[0], w2[0])
    v_cmp = nsa_compress(rows[:, :, 1], pe[1], w1[1], w2[1])
    w_buf = win_buf.shape[1]
    win = jnp.concatenate([win_buf, kv_new[:, :, 4:6]], axis=1)
    wpos = past - w_buf + jnp.arange(w_buf + DS)
    qpos = past + jnp.arange(DS)
    o = nsa_attention(q, qpos, gates, k_cmp, v_cmp, rows[:, :, 2], rows[:, :, 3],
                      win[:, :, 0], win[:, :, 1], wpos)
    return o, win[:, win.shape[1] - w_buf:]


def pool_mix(u_ext, pos_ext, n_out, w_pool, scale):
    N = u_ext.shape[1]
    cz = jnp.pad(jnp.cumsum(u_ext.astype(F32), axis=1), ((0, 0), (1, 0), (0, 0)))
    idx = jnp.arange(N - n_out, N)
    cnt_pos = pos_ext[N - n_out:]
    u_out = u_ext[:, N - n_out:].astype(F32)
    outs = []
    for g, w in enumerate(POOL_WINDOWS):
        sl = slice(g * POOL_GW, (g + 1) * POOL_GW)
        lo = jnp.maximum(idx + 1 - w, 0)
        cnt = jnp.minimum(cnt_pos + 1, w).astype(F32)[None, :, None]
        mean = (cz[:, idx + 1, sl] - cz[:, lo, sl]) / cnt
        outs.append(jnp.einsum('btc,ce->bte', mean - u_out[:, :, sl], w_pool[g].astype(F32)))
    return (jnp.concatenate(outs, axis=-1) * scale.astype(F32)).astype(u_ext.dtype)


def mem_attention(q, mem_kv):
    B, T = q.shape[:2]
    s = jnp.einsum('bthd,bmhd->bhtm', q, mem_kv[:, :, 0]).astype(F32) * (HEAD_DIM ** -0.5)
    p = jax.nn.softmax(s, axis=-1)
    return jnp.einsum('bhtm,bmhd->bthd', p.astype(q.dtype), mem_kv[:, :, 1]).reshape(B, T, H_MEM * HEAD_DIM)


def mix_out(x, outs, z, w_out):
    o = jnp.concatenate(outs, axis=-1) * jax.nn.silu(z)
    return x + jnp.einsum('bte,ed->btd', o, w_out)


def layer_prompt(x, mem, lw):
    g_n, w_in, b_f, pe, w1, w2, p_w, p_s, w_mkv, w_o = lw
    B, T, _ = x.shape
    hn = rms_norm(x, g_n)
    fox_qkv, fox_f, nsa_q, nsa_kv, nsa_g, pool_u, mem_q, z = split_projection(
        jnp.einsum('btd,de->bte', hn, w_in))
    lf = jax.nn.log_sigmoid(fox_f.astype(F32) + b_f.astype(F32))
    o_a = fox_prompt(fox_qkv[:, :, 0], fox_qkv[:, :, 1], fox_qkv[:, :, 2], lf)
    o_b = nsa_prompt(nsa_q, nsa_kv, nsa_g, pe, w1, w2)
    o_c = pool_mix(pool_u, jnp.arange(T), T, p_w, p_s)
    mem_kv = jnp.einsum('bmd,de->bme', mem, w_mkv).reshape(B, N_MEM, 2, H_MEM, HEAD_DIM)
    o_m = mem_attention(mem_q, mem_kv)
    y = mix_out(x, [o_a, o_b, o_c, o_m], z, w_o)
    w_keep = min(NSA_WINDOW, T)
    return y, (fox_qkv[:, :, 1:3], lf, nsa_kv[:, :, :4], nsa_kv[:, T - w_keep:, 4:6],
               pool_u[:, T - POOL_BUF:], mem_kv)


def layer_sample(x, c_fox_kv, c_fox_lf, c_nsa_kv, s_win, s_pool, c_mem_kv, page_table, lw):
    g_n, w_in, b_f, pe, w1, w2, p_w, p_s, w_mkv, w_o = lw
    DB, DS, _ = x.shape
    past = page_table.shape[1] * PAGE_SIZE
    hn = rms_norm(x, g_n)
    fox_qkv, fox_f, nsa_q, nsa_kv, nsa_g, pool_u, mem_q, z = split_projection(
        jnp.einsum('btd,de->bte', hn, w_in))
    lf = jax.nn.log_sigmoid(fox_f.astype(F32) + b_f.astype(F32))
    fox_past = c_fox_kv[page_table].reshape(DB, past, 2, H_FOX, HEAD_DIM)
    lf_past = c_fox_lf[page_table].reshape(DB, past, H_FOX).astype(F32)
    k_all = jnp.concatenate([fox_past[:, :, 0], fox_qkv[:, :, 1]], axis=1)
    v_all = jnp.concatenate([fox_past[:, :, 1], fox_qkv[:, :, 2]], axis=1)
    c_all = jnp.cumsum(jnp.concatenate([lf_past, lf], axis=1), axis=1)
    qpos = past + jnp.arange(DS)
    o_a = fox_attention(fox_qkv[:, :, 0], qpos, c_all[:, past:], k_all, v_all,
                        jnp.arange(past + DS), c_all).reshape(DB, DS, H_FOX * HEAD_DIM)
    nsa_past = c_nsa_kv[page_table].reshape(DB, past, 4, G_NSA, HEAD_DIM)
    o_b, new_win = nsa_sample(nsa_q, nsa_kv, nsa_g, nsa_past, s_win, pe, w1, w2)
    u_ext = jnp.concatenate([s_pool, pool_u], axis=1)
    o_c = pool_mix(u_ext, past - POOL_BUF + jnp.arange(POOL_BUF + DS), DS, p_w, p_s)
    o_m = mem_attention(mem_q, c_mem_kv)
    y = mix_out(x, [o_a, o_b, o_c, o_m], z, w_o)
    return y, (fox_qkv[:, :, 1:3], lf, nsa_kv[:, :, :4], new_win, u_ext[:, u_ext.shape[1] - POOL_BUF:])


def setup_inputs(seed: int = 0) -> dict:
    key = jax.random.key(seed)
    k = jax.random.split(key, 24)
    n_pages = PAST_LEN // PAGE_SIZE
    n_used = DEC_BATCH * n_pages
    n_phys = n_used + max(1, n_used // 4)
    w_buf = min(NSA_WINDOW, PAST_LEN)

    def nrm(kk, shape, s=1.0):
        return s * jax.random.normal(kk, shape, F32)

    return {
        "x_prompt": nrm(k[0], (BATCH, SEQ, D_MODEL)),
        "x_sample": nrm(k[1], (DEC_BATCH, DEC_SEQ, D_MODEL)),
        "cache_fox_kv": nrm(k[2], (DEPTH, n_phys, PAGE_SIZE, 2, H_FOX, HEAD_DIM)),
        "cache_fox_lf": jax.nn.log_sigmoid(FGT_BIAS_MEAN + nrm(k[3], (DEPTH, n_phys, PAGE_SIZE, H_FOX))),
        "cache_nsa_kv": nrm(k[4], (DEPTH, n_phys, PAGE_SIZE, 4, G_NSA, HEAD_DIM)),
        "state_nsa_win": nrm(k[5], (DEPTH, DEC_BATCH, w_buf, 2, G_NSA, HEAD_DIM)),
        "state_pool": nrm(k[6], (DEPTH, DEC_BATCH, POOL_BUF, D_POOL)),
        "cache_mem_kv": nrm(k[7], (DEPTH, DEC_BATCH, N_MEM, 2, H_MEM, HEAD_DIM)),
        "page_table": jax.random.permutation(k[8], n_phys)[:n_used].reshape(DEC_BATCH, n_pages).astype(jnp.int32),
        "mem_prompt": nrm(k[9], (BATCH, N_MEM, D_MODEL)),
        "norm_g": 1.0 + nrm(k[10], (DEPTH, D_MODEL), 0.05),
        "w_in": nrm(k[11], (DEPTH, D_MODEL, D_IN), D_MODEL ** -0.5),
        "b_fgt": FGT_BIAS_MEAN + nrm(k[12], (DEPTH, H_FOX), 0.5),
        "nsa_pe": nrm(k[13], (DEPTH, 2, NSA_BLOCK, HEAD_DIM), 0.1),
        "nsa_w1": nrm(k[14], (DEPTH, 2, NSA_BLOCK * HEAD_DIM, NSA_CMP_HID), (NSA_BLOCK * HEAD_DIM) ** -0.5),
        "nsa_w2": nrm(k[15], (DEPTH, 2, NSA_CMP_HID, HEAD_DIM), (NSA_CMP_HID / 2.0) ** -0.5),
        "pool_w": nrm(k[16], (DEPTH, N_POOL_GROUPS, POOL_GW, POOL_GW), POOL_GW ** -0.5),
        "pool_scale": 1.0 + nrm(k[17], (DEPTH, D_POOL), 0.1),
        "w_mem_kv": nrm(k[18], (DEPTH, D_MODEL, 2 * D_MEMX), D_MODEL ** -0.5),
        "w_out": nrm(k[19], (DEPTH, D_MIX, D_MODEL), D_MIX ** -0.5),
        "final_g": 1.0 + nrm(k[20], (D_MODEL,), 0.05),
    }


def reference(x_prompt, x_sample, cache_fox_kv, cache_fox_lf, cache_nsa_kv, state_nsa_win,
              state_pool, cache_mem_kv, page_table, mem_prompt, norm_g, w_in, b_fgt, nsa_pe,
              nsa_w1, nsa_w2, pool_w, pool_scale, w_mem_kv, w_out, final_g):
    hp, hs = x_prompt, x_sample
    sp = [[] for _ in range(6)]
    ss = [[] for _ in range(5)]
    for l in range(DEPTH):
        lw = (norm_g[l], w_in[l], b_fgt[l], nsa_pe[l], nsa_w1[l], nsa_w2[l],
              pool_w[l], pool_scale[l], w_mem_kv[l], w_out[l])
        hp, st_p = layer_prompt(hp, mem_prompt, lw)
        hs, st_s = layer_sample(hs, cache_fox_kv[l], cache_fox_lf[l], cache_nsa_kv[l],
                                state_nsa_win[l], state_pool[l], cache_mem_kv[l], page_table, lw)
        for acc, s in zip(sp, st_p):
            acc.append(s)
        for acc, s in zip(ss, st_s):
            acc.append(s)
    y_prompt = rms_norm(hp, final_g)
    y_sample = rms_norm(hs, final_g)
    fox_kv_p = jnp.stack(sp[0])
    fox_lf_p = jnp.stack(sp[1])
    nsa_kv_p = jnp.stack(sp[2])
    nsa_win_p = jnp.stack(sp[3])
    pool_p = jnp.stack(sp[4])
    mem_kv_p = jnp.stack(sp[5])
    fox_kv_s = jnp.stack(ss[0])
    fox_lf_s = jnp.stack(ss[1])
    nsa_kv_s = jnp.stack(ss[2])
    nsa_win_s = jnp.stack(ss[3])
    pool_s = jnp.stack(ss[4])
    return (y_prompt, y_sample, fox_kv_p, fox_lf_p, nsa_kv_p, nsa_win_p, pool_p, mem_kv_p,
            fox_kv_s, fox_lf_s, nsa_kv_s, nsa_win_s, pool_s)
```

```python
import functools

import numpy as np
import jax
import jax.numpy as jnp
from jax import lax
from jax.experimental import pallas as pl
from jax.experimental.pallas import tpu as pltpu

F32 = jnp.float32
BF16 = jnp.bfloat16
I32 = jnp.int32

HEAD_DIM = 64
H_FOX = 4
H_NSA = 4
G_NSA = 2
H_MEM = 4
D_GROUP = 256
POOL_WINDOWS = (2, 4, 8, 16)
POOL_BUF = 15
NSA_BLOCK = 64
NSA_TOPN = 16
NSA_WINDOW = 512
PAGE_SIZE = 128
RMS_EPS = 1e-6
NEG_INF = -1e30
FORCE_SCORE = 1e4
SCALE = HEAD_DIM ** -0.5
ALIBI_SLOPES = tuple(2.0 ** (-8.0 * (h + 1) / H_NSA) for h in range(H_NSA))

LANES = 128
NBLK_PAD = 128
VMEM_LIMIT = 56 * 1024 * 1024

C_FQ, C_FK, C_FV, C_NQ, C_NKV, C_MQ, C_PU, C_Z, C_SM, C_END = (
    0, 256, 512, 768, 1024, 1792, 2048, 2304, 3328, 3456)
N_PB = C_PU
LANE_GATE0 = 4


def _nt(a, b):
    return lax.dot_general(a, b, (((1,), (1,)), ((), ())), preferred_element_type=F32)


def _mm(a, b):
    return jnp.dot(a, b, preferred_element_type=F32)


def _split3(x):
    p1 = x.astype(BF16)
    r1 = x - p1.astype(F32)
    p2 = r1.astype(BF16)
    p3 = (r1 - p2.astype(F32)).astype(BF16)
    return p1, p2, p3


def _exact_mm(m01, x):
    p1, p2, p3 = _split3(x)
    return _mm(m01, p1) + _mm(m01, p2) + _mm(m01, p3)


def _sigmoid(x):
    return 1.0 / (1.0 + jnp.exp(-x))


def _log_sigmoid(x):
    return jnp.minimum(x, 0.0) - jnp.log1p(jnp.exp(-jnp.abs(x)))


def _iota(shape, dim):
    return lax.broadcasted_iota(I32, shape, dim)


def _proj_body(x_ref, g_ref, w_ref, b_ref, pb_ref, sf_ref, pz_ref, aux_ref, *rest,
               tm, tiles_per_seq, with_cum):
    x = x_ref[...]
    ms = jnp.mean(x * x, axis=-1, keepdims=True)
    xn = (x * lax.rsqrt(ms + RMS_EPS) * g_ref[...]).astype(BF16)

    def seg(a, b):
        return _mm(xn, w_ref[:, a:b])

    pb_ref[:, C_FQ:C_FK] = (seg(C_FQ, C_FK) * SCALE).astype(BF16)
    fkv = seg(C_FK, C_NQ)
    pb_ref[:, C_FK:C_NQ] = fkv.astype(BF16)
    sf_ref[:, 0:512] = fkv
    pb_ref[:, C_NQ:C_NKV] = (seg(C_NQ, C_NKV) * SCALE).astype(BF16)
    nkv = seg(C_NKV, C_MQ)
    pb_ref[:, C_NKV:C_MQ] = nkv.astype(BF16)
    sf_ref[:, 512:1280] = nkv
    pb_ref[:, C_MQ:C_PU] = (seg(C_MQ, C_PU) * SCALE).astype(BF16)
    pz_ref[...] = seg(C_PU, C_SM)
    small = seg(C_SM, C_END)

    lane = _iota((tm, LANES), 1)
    lf = _log_sigmoid(small + b_ref[...])
    aux_ref[...] = jnp.where(lane < H_FOX, lf, jnp.where(lane < LANE_GATE0 + 3 * H_NSA,
                                                          _sigmoid(small), 0.0))
    if with_cum:
        cq_ref, ct_ref, carry_ref = rest
        lfm = jnp.where(lane < H_FOX, lf, 0.0)
        tri = (_iota((LANES, LANES), 0) >= _iota((LANES, LANES), 1)).astype(BF16)
        first = (pl.program_id(0) % tiles_per_seq) == 0
        carry = jnp.where(first, 0.0, carry_ref[0:1, :])
        for r in range(tm // LANES):
            blk = lfm[r * LANES:(r + 1) * LANES]
            cblk = _exact_mm(tri, blk) + carry
            carry = cblk[LANES - 1:LANES, :]
            cq_ref[r * LANES:(r + 1) * LANES, :] = cblk
            ct_ref[:, r * LANES:(r + 1) * LANES] = cblk.T[0:8, :]
        carry_ref[0:1, :] = carry


def _project(x2d, g, wp, bvec, *, tm, tiles_per_seq, with_cum):
    n, d = x2d.shape
    grid = (n // tm,)
    row = lambda i: (i, 0)
    const = lambda i: (0, 0)
    out_shape = [jax.ShapeDtypeStruct((n, N_PB), BF16),
                 jax.ShapeDtypeStruct((n, 1280), F32),
                 jax.ShapeDtypeStruct((n, 1280), F32),
                 jax.ShapeDtypeStruct((n, LANES), F32)]
    out_specs = [pl.BlockSpec((tm, N_PB), row), pl.BlockSpec((tm, 1280), row),
                 pl.BlockSpec((tm, 1280), row), pl.BlockSpec((tm, LANES), row)]
    scratch = []
    if with_cum:
        nseq = n // (tm * tiles_per_seq)
        t = tm * tiles_per_seq
        out_shape += [jax.ShapeDtypeStruct((n, LANES), F32),
                      jax.ShapeDtypeStruct((nseq, 8, t), F32)]
        out_specs += [pl.BlockSpec((tm, LANES), row),
                      pl.BlockSpec((None, 8, tm),
                                   lambda i: (i // tiles_per_seq, 0, i % tiles_per_seq))]
        scratch = [pltpu.VMEM((8, LANES), F32)]
    return pl.pallas_call(
        functools.partial(_proj_body, tm=tm, tiles_per_seq=tiles_per_seq, with_cum=with_cum),
        grid=grid,
        in_specs=[pl.BlockSpec((tm, d), row), pl.BlockSpec((1, d), const),
                  pl.BlockSpec((d, C_END), const), pl.BlockSpec((1, LANES), const)],
        out_specs=out_specs, out_shape=out_shape, scratch_shapes=scratch,
        compiler_params=pltpu.CompilerParams(dimension_semantics=("arbitrary",),
                                             vmem_limit_bytes=VMEM_LIMIT),
        name="proj",
    )(x2d, g, wp, bvec)


def _memkv_body(m_ref, w_ref, o_ref, ob_ref):
    r = _mm(m_ref[...].astype(BF16), w_ref[...])
    o_ref[...] = r
    ob_ref[...] = r.astype(BF16)


def _mem_project(mem2d, w):
    n, d = mem2d.shape
    e = w.shape[1]
    tm = 256
    return pl.pallas_call(
        _memkv_body, grid=(n // tm,),
        in_specs=[pl.BlockSpec((tm, d), lambda i: (i, 0)), pl.BlockSpec((d, e), lambda i: (0, 0))],
        out_specs=[pl.BlockSpec((tm, e), lambda i: (i, 0)), pl.BlockSpec((tm, e), lambda i: (i, 0))],
        out_shape=[jax.ShapeDtypeStruct((n, e), F32), jax.ShapeDtypeStruct((n, e), BF16)],
        name="memkv",
    )(mem2d, w)


def _fox_body(q_ref, k_ref, v_ref, cq_ref, ct_ref, o_ref, m_ref, l_ref, acc_ref, *, tq, tk):
    qi = pl.program_id(1)
    ki = pl.program_id(2)

    @pl.when(ki == 0)
    def _():
        m_ref[...] = jnp.full(m_ref.shape, NEG_INF, F32)
        l_ref[...] = jnp.zeros(l_ref.shape, F32)
        acc_ref[...] = jnp.zeros(acc_ref.shape, F32)

    lane = _iota((tq, LANES), 1)

    def step(masked):
        if masked:
            causal = _iota((tq, tk), 0) >= _iota((tq, tk), 1)
        for h in range(H_FOX):
            pr, hh = divmod(h, 2)
            cs = slice(pr * LANES, (pr + 1) * LANES)
            q2 = q_ref[:, cs]
            qm = jnp.where((lane >= 64 * hh) & (lane < 64 * hh + 64), q2, jnp.zeros_like(q2))
            s = _nt(qm, k_ref[:, cs])
            s = s + cq_ref[:, h:h + 1] - ct_ref[h:h + 1, :]
            if masked:
                s = jnp.where(causal, s, NEG_INF)
            m_prev = m_ref[h]
            m_new = jnp.maximum(m_prev, jnp.max(s, axis=1, keepdims=True))
            alpha = jnp.exp(m_prev - m_new)
            p = jnp.exp(s - m_new)
            l_ref[h] = alpha * l_ref[h] + jnp.sum(p, axis=1, keepdims=True)
            acc_ref[h] = alpha * acc_ref[h] + _mm(p.astype(BF16), v_ref[:, cs])
            m_ref[h] = m_new

    @pl.when(ki < qi)
    def _():
        step(False)

    @pl.when(ki == qi)
    def _():
        step(True)
        for pr in range(H_FOX // 2):
            o0 = acc_ref[2 * pr] / l_ref[2 * pr]
            o1 = acc_ref[2 * pr + 1] / l_ref[2 * pr + 1]
            o_ref[:, pr * LANES:(pr + 1) * LANES] = jnp.where(lane < 64, o0, o1)


def _fox_prompt(pb3, cq3, ct, *, tq):
    b, t, _ = pb3.shape
    nq = t // tq
    kv = lambda bi, qi, ki: (bi, jnp.minimum(ki, qi), 0)
    return pl.pallas_call(
        functools.partial(_fox_body, tq=tq, tk=tq),
        grid=(b, nq, nq),
        in_specs=[pl.BlockSpec((None, tq, 256), lambda bi, qi, ki: (bi, qi, C_FQ // 256)),
                  pl.BlockSpec((None, tq, 256), lambda bi, qi, ki: (bi, jnp.minimum(ki, qi), C_FK // 256)),
                  pl.BlockSpec((None, tq, 256), lambda bi, qi, ki: (bi, jnp.minimum(ki, qi), C_FV // 256)),
                  pl.BlockSpec((None, tq, LANES), lambda bi, qi, ki: (bi, qi, 0)),
                  pl.BlockSpec((None, 8, tq), lambda bi, qi, ki: (bi, 0, jnp.minimum(ki, qi)))],
        out_specs=pl.BlockSpec((None, tq, 256), lambda bi, qi, ki: (bi, qi, 0)),
        out_shape=jax.ShapeDtypeStruct((b, t, 256), F32),
        scratch_shapes=[pltpu.VMEM((H_FOX, tq, 1), F32), pltpu.VMEM((H_FOX, tq, 1), F32),
                        pltpu.VMEM((H_FOX, tq, LANES), F32)],
        compiler_params=pltpu.CompilerParams(
            dimension_semantics=("parallel", "parallel", "arbitrary"),
            vmem_limit_bytes=VMEM_LIMIT),
        name="fox_prompt",
    )(pb3, pb3, pb3, cq3, ct)


def _cmp_body(x_ref, pe_ref, w1_ref, w2_ref, o_ref):
    xb = (x_ref[...] + pe_ref[...]).astype(BF16)
    h = _mm(xb, w1_ref[...])
    h = h * _sigmoid(h)
    o_ref[...] = _mm(h.astype(BF16), w2_ref[...])


def _compress_prompt(flat, pe_flat, w1, w2):
    _, m, f = flat.shape
    tm = min(m, 512)
    return pl.pallas_call(
        _cmp_body, grid=(2, m // tm),
        in_specs=[pl.BlockSpec((None, tm, f), lambda j, i: (j, i, 0)),
                  pl.BlockSpec((None, 1, f), lambda j, i: (j, 0, 0)),
                  pl.BlockSpec((None, f, 256), lambda j, i: (j, 0, 0)),
                  pl.BlockSpec((None, 256, HEAD_DIM), lambda j, i: (j, 0, 0))],
        out_specs=pl.BlockSpec((None, tm, HEAD_DIM), lambda j, i: (j, i, 0)),
        out_shape=jax.ShapeDtypeStruct((2, m, HEAD_DIM), F32),
        compiler_params=pltpu.CompilerParams(vmem_limit_bytes=VMEM_LIMIT),
        name="cmp_prompt",
    )(flat, pe_flat, w1, w2)


def _masked_softmax_cols(s, valid):
    s = jnp.where(valid, s, NEG_INF)
    e = jnp.where(valid, jnp.exp(s - jnp.max(s, axis=0, keepdims=True)), 0.0)
    return e / jnp.maximum(jnp.sum(e, axis=0, keepdims=True), 1e-30)


def _select_blocks(score, nb, topn):
    nbp, n = score.shape
    blk = _iota((nbp, n), 0)
    rank = jnp.zeros((nbp, n), F32)
    for m in range(nb):
        r = score[m:m + 1, :]
        rank = rank + jnp.where((r > score) | ((r == score) & (blk > m)), 1.0, 0.0)
    return jnp.where((rank < topn) & (score >= 0.0), 1.0, 0.0)


def _nsa_body(q_ref, ks_ref, vs_ref, kwc_ref, kwp_ref, vwc_ref, vwp_ref, kc_ref, vc_ref,
              aux_ref, o_ref, sel_ref, m_ref, l_ref, acc_ref, oc_ref, ow_ref, *, tq, nb, topn):
    qi = pl.program_id(1)
    ki = pl.program_id(2)
    q0 = qi * tq
    tk = tq
    lane = _iota((tq, LANES), 1)
    row2 = _iota((2 * tq, 1), 0)
    rq_col = jnp.where(row2 >= tq, row2 - tq, row2)

    def qstack(g):
        q2 = q_ref[:, g * LANES:(g + 1) * LANES].astype(F32)
        q2r = pltpu.roll(q2, 64, 1)
        ing = (lane >= 64 * g) & (lane < 64 * g + 64)
        qa, qb = (q2, q2r) if g == 0 else (q2r, q2)
        return jnp.concatenate([jnp.where(ing, qa, 0.0), jnp.where(ing, qb, 0.0)],
                               axis=0).astype(BF16)

    def slope_col(g):
        return jnp.where(row2 < tq, ALIBI_SLOPES[2 * g], ALIBI_SLOPES[2 * g + 1])

    @pl.when(ki == 0)
    def _():
        m_ref[...] = jnp.full(m_ref.shape, NEG_INF, F32)
        l_ref[...] = jnp.zeros(l_ref.shape, F32)
        acc_ref[...] = jnp.zeros(acc_ref.shape, F32)
        for g in range(G_NSA):
            qs = qstack(g)
            st = _nt(kc_ref[...], qs)
            blk = _iota((NBLK_PAD, 2 * tq), 0)
            col = _iota((NBLK_PAD, 2 * tq), 1)
            qpos = q0 + jnp.where(col >= tq, col - tq, col)
            slope = jnp.where(col < tq, ALIBI_SLOPES[2 * g], ALIBI_SLOPES[2 * g + 1])
            cend = blk * NSA_BLOCK + (NSA_BLOCK - 1)
            st = st - slope * (qpos - cend).astype(F32)
            pt = _masked_softmax_cols(st, (cend <= qpos) & (blk < nb))
            oc_ref[g] = _mm(pt.T.astype(BF16), vc_ref[...])
            imp = pt[:, :tq] + pt[:, tq:]
            nbp = -(-nb // 8) * 8
            blk2 = _iota((nbp, tq), 0)
            cur = (q0 + _iota((nbp, tq), 1)) // NSA_BLOCK
            forced = (blk2 == 0) | (blk2 == cur) | (blk2 == cur - 1)
            score = jnp.where(blk2 > cur, -1.0, jnp.where(forced, FORCE_SCORE, imp[:nbp]))
            selt = _select_blocks(score, nb, topn)
            if nbp < NBLK_PAD:
                selt = jnp.concatenate([selt, jnp.zeros((NBLK_PAD - nbp, tq), F32)], axis=0)
            sel_ref[g] = selt.T.astype(BF16)
            rq = jnp.where(_iota((2 * tq, tk), 0) >= tq, _iota((2 * tq, tk), 0) - tq,
                           _iota((2 * tq, tk), 0))
            d1 = rq - _iota((2 * tq, tk), 1)
            v1 = d1 >= 0
            s1 = jnp.where(v1, _nt(qs, kwc_ref[...]) - slope_col(g) * d1.astype(F32), NEG_INF)
            d2 = d1 + tq
            v2 = (d2 <= NSA_WINDOW) & (qi > 0)
            s2 = jnp.where(v2, _nt(qs, kwp_ref[...]) - slope_col(g) * d2.astype(F32), NEG_INF)
            mw = jnp.maximum(jnp.max(s1, axis=1, keepdims=True), jnp.max(s2, axis=1, keepdims=True))
            e1 = jnp.where(v1, jnp.exp(s1 - mw), 0.0)
            e2 = jnp.where(v2, jnp.exp(s2 - mw), 0.0)
            lw = jnp.sum(e1, axis=1, keepdims=True) + jnp.sum(e2, axis=1, keepdims=True)
            ow_ref[g] = (_mm(e1.astype(BF16), vwc_ref[...]) + _mm(e2.astype(BF16), vwp_ref[...])) \
                / jnp.maximum(lw, 1e-30)

    def sel_step(diag):
        expand = (_iota((NBLK_PAD, tk), 0) ==
                  ki * (tk // NSA_BLOCK) + _iota((NBLK_PAD, tk), 1) // NSA_BLOCK).astype(BF16)
        dist1 = _iota((tq, tk), 0) - _iota((tq, tk), 1)
        for g in range(G_NSA):
            qs = qstack(g)
            valid = _mm(sel_ref[g], expand) > 0.5
            if diag:
                valid = valid & (dist1 >= 0)
            valid2 = jnp.concatenate([valid, valid], axis=0)
            dist = jnp.concatenate([dist1, dist1], axis=0) + (q0 - ki * tk)
            s = _nt(qs, ks_ref[...]) - slope_col(g) * dist.astype(F32)
            s = jnp.where(valid2, s, NEG_INF)
            m_prev = m_ref[g]
            m_new = jnp.maximum(m_prev, jnp.max(s, axis=1, keepdims=True))
            alpha = jnp.exp(m_prev - m_new)
            p = jnp.where(valid2, jnp.exp(s - m_new), 0.0)
            l_ref[g] = alpha * l_ref[g] + jnp.sum(p, axis=1, keepdims=True)
            acc_ref[g] = alpha * acc_ref[g] + _mm(p.astype(BF16), vs_ref[...])
            m_ref[g] = m_new

    @pl.when(ki < qi)
    def _():
        sel_step(False)

    @pl.when(ki == qi)
    def _():
        sel_step(True)
        for g in range(G_NSA):
            o_s = acc_ref[g] / jnp.maximum(l_ref[g], 1e-30)
            o_c = oc_ref[g]
            o_w = ow_ref[g]
            outs = []
            for hh in range(2):
                h = 2 * g + hh
                rs = slice(hh * tq, (hh + 1) * tq)
                gl = LANE_GATE0 + 3 * h
                outs.append(aux_ref[:, gl:gl + 1] * o_c[rs] + aux_ref[:, gl + 1:gl + 2] * o_s[rs]
                            + aux_ref[:, gl + 2:gl + 3] * o_w[rs])
            oa, ob = outs
            if g == 0:
                ob = pltpu.roll(ob, 64, 1)
            else:
                oa = pltpu.roll(oa, 64, 1)
            o_ref[:, g * LANES:(g + 1) * LANES] = jnp.where(lane < 64, oa, ob)


def _nsa_prompt(pb3, kcmp, vcmp, aux3, *, tq, nb):
    b, t, _ = pb3.shape
    nq = t // tq
    ct = C_NKV // LANES
    cur = lambda c: (lambda bi, qi, ki: (bi, qi, c))
    prev = lambda c: (lambda bi, qi, ki: (bi, jnp.maximum(qi - 1, 0), c))
    kvt = lambda c: (lambda bi, qi, ki: (bi, jnp.minimum(ki, qi), c))
    cmp_spec = pl.BlockSpec((None, NBLK_PAD, LANES), lambda bi, qi, ki: (bi, 0, 0))
    tile = lambda f: pl.BlockSpec((None, tq, LANES), f)
    return pl.pallas_call(
        functools.partial(_nsa_body, tq=tq, nb=nb, topn=min(NSA_TOPN, nb)),
        grid=(b, nq, nq),
        in_specs=[pl.BlockSpec((None, tq, 256), lambda bi, qi, ki: (bi, qi, C_NQ // 256)),
                  tile(kvt(ct + 2)), tile(kvt(ct + 3)),
                  tile(cur(ct + 4)), tile(prev(ct + 4)), tile(cur(ct + 5)), tile(prev(ct + 5)),
                  cmp_spec, cmp_spec,
                  pl.BlockSpec((None, tq, LANES), lambda bi, qi, ki: (bi, qi, 0))],
        out_specs=pl.BlockSpec((None, tq, 256), lambda bi, qi, ki: (bi, qi, 0)),
        out_shape=jax.ShapeDtypeStruct((b, t, 256), F32),
        scratch_shapes=[pltpu.VMEM((G_NSA, tq, NBLK_PAD), BF16),
                        pltpu.VMEM((G_NSA, 2 * tq, 1), F32), pltpu.VMEM((G_NSA, 2 * tq, 1), F32),
                        pltpu.VMEM((G_NSA, 2 * tq, LANES), F32),
                        pltpu.VMEM((G_NSA, 2 * tq, LANES), F32),
                        pltpu.VMEM((G_NSA, 2 * tq, LANES), F32)],
        compiler_params=pltpu.CompilerParams(
            dimension_semantics=("parallel", "parallel", "arbitrary"),
            vmem_limit_bytes=VMEM_LIMIT),
        name="nsa_prompt",
    )(pb3, pb3, pb3, pb3, pb3, pb3, pb3, kcmp, vcmp, aux3)


def _mem_attend(mq_ref, mk_ref, mv_ref, rows):
    lane = _iota((rows, LANES), 1)
    pairs = []
    for pr in range(H_MEM // 2):
        cs = slice(pr * LANES, (pr + 1) * LANES)
        q2 = mq_ref[:, cs]
        halves = []
        for hh in range(2):
            qm = jnp.where((lane >= 64 * hh) & (lane < 64 * hh + 64), q2, jnp.zeros_like(q2))
            s = _nt(qm, mk_ref[:, cs])
            e = jnp.exp(s - jnp.max(s, axis=1, keepdims=True))
            p = e / jnp.sum(e, axis=1, keepdims=True)
            halves.append(_mm(p.astype(BF16), mv_ref[:, cs]))
        pairs.append(jnp.where(lane < 64, halves[0], halves[1]))
    return pairs


def _mix_out(x, parts, z_of, wout_ref, fg_ref, final):
    y = x
    for c, part in enumerate(parts):
        z = z_of(c)
        gated = (part * (z * _sigmoid(z))).astype(BF16)
        y = y + _mm(gated, wout_ref[c * LANES:(c + 1) * LANES, :])
    if final:
        y = y * lax.rsqrt(jnp.mean(y * y, axis=-1, keepdims=True) + RMS_EPS) * fg_ref[...]
    return y


def _pool_window_lane(shape):
    grp = _iota(shape, 1) // (D_GROUP // len(POOL_WINDOWS))
    wl = jnp.where(grp == 0, POOL_WINDOWS[0], jnp.where(grp == 1, POOL_WINDOWS[1],
                   jnp.where(grp == 2, POOL_WINDOWS[2], POOL_WINDOWS[3])))
    return grp, wl


HALO = 32


def _post_body(oa_ref, ob_ref, pz_ref, halo_ref, mq_ref, mk_ref, mv_ref, x_ref, wout_ref, pw_ref,
               ps_ref, fg_ref, y_ref, s0, s1, s2, s3, *, tm, final):
    i = pl.program_id(1)
    u = pz_ref[:, 0:256]
    s0[0:HALO, :] = jnp.where(i > 0, halo_ref[...], 0.0)
    s0[HALO:HALO + tm, :] = u
    n = tm + HALO
    s1[8:n, :] = s0[8:n, :] + s0[7:n - 1, :]
    s2[16:n, :] = s1[16:n, :] + s1[14:n - 2, :]
    s3[24:n, :] = s2[24:n, :] + s2[20:n - 4, :]
    a16 = s3[HALO:n, :] + s3[HALO - 8:n - 8, :]
    grp, wl = _pool_window_lane((tm, 256))
    tsum = jnp.where(grp == 0, s1[HALO:n, :], jnp.where(grp == 1, s2[HALO:n, :],
                     jnp.where(grp == 2, s3[HALO:n, :], a16)))
    pos = i * tm + _iota((tm, 256), 0)
    cnt = jnp.minimum(pos + 1, wl).astype(F32)
    o_pool = _mm((tsum / cnt - u).astype(BF16), pw_ref[...]) * ps_ref[...]
    o_mem = _mem_attend(mq_ref, mk_ref, mv_ref, tm)
    parts = [oa_ref[:, 0:LANES], oa_ref[:, LANES:256], ob_ref[:, 0:LANES], ob_ref[:, LANES:256],
             o_pool[:, 0:LANES], o_pool[:, LANES:256], o_mem[0], o_mem[1]]
    z_of = lambda c: pz_ref[:, 256 + c * LANES:256 + (c + 1) * LANES]
    y_ref[...] = _mix_out(x_ref[...], parts, z_of, wout_ref, fg_ref, final)


def _post_prompt(oa, ob, pz3, pb3, memb, x3, wout, pw_bd, ps, fg, *, tm, final):
    b, t, d = x3.shape
    nt = t // tm
    row = lambda bi, i: (bi, i, 0)
    const = lambda bi, i: (0, 0)
    return pl.pallas_call(
        functools.partial(_post_body, tm=tm, final=final),
        grid=(b, nt),
        in_specs=[pl.BlockSpec((None, tm, 256), row), pl.BlockSpec((None, tm, 256), row),
                  pl.BlockSpec((None, tm, 1280), row),
                  pl.BlockSpec((None, HALO, 256),
                               lambda bi, i: (bi, jnp.maximum(i * (tm // HALO) - 1, 0), 0)),
                  pl.BlockSpec((None, tm, 256), lambda bi, i: (bi, i, C_MQ // 256)),
                  pl.BlockSpec((None, 256, 256), lambda bi, i: (bi, 0, 0)),
                  pl.BlockSpec((None, 256, 256), lambda bi, i: (bi, 0, 1)),
                  pl.BlockSpec((None, tm, d), row),
                  pl.BlockSpec((d, d), const), pl.BlockSpec((256, 256), const),
                  pl.BlockSpec((1, 256), const), pl.BlockSpec((1, d), const)],
        out_specs=pl.BlockSpec((None, tm, d), row),
        out_shape=jax.ShapeDtypeStruct((b, t, d), F32),
        scratch_shapes=[pltpu.VMEM((tm + HALO, 256), F32)] * 4,
        compiler_params=pltpu.CompilerParams(dimension_semantics=("parallel", "arbitrary"),
                                             vmem_limit_bytes=VMEM_LIMIT),
        name="post_prompt",
    )(oa, ob, pz3, pz3, pb3, memb, memb, x3, wout, pw_bd, ps, fg)


ROWS_PER_PAGE = 8


def _cmp_sample_body(pt_ref, flat_hbm, pe_ref, w1_ref, w2_ref, o_ref, xbuf, sem, *, per_step):
    s = pl.program_id(0)
    nsteps = pl.num_programs(0)

    def copies(step, slot):
        out = []
        for i in range(per_step):
            page = pt_ref[step * per_step + i]
            out.append(pltpu.make_async_copy(
                flat_hbm.at[page], xbuf.at[slot, pl.ds(i * ROWS_PER_PAGE, ROWS_PER_PAGE), :],
                sem.at[slot]))
        return out

    @pl.when(s == 0)
    def _():
        for c in copies(0, 0):
            c.start()

    @pl.when(s + 1 < nsteps)
    def _():
        for c in copies(s + 1, (s + 1) % 2):
            c.start()

    slot = s % 2
    for c in copies(s, slot):
        c.wait()
    rows = per_step * ROWS_PER_PAGE
    x = xbuf[slot].reshape(per_step, ROWS_PER_PAGE, xbuf.shape[-1]) + pe_ref[...][None]
    xb = x.reshape(rows, xbuf.shape[-1]).astype(BF16)
    h = _mm(xb, w1_ref[...])
    h = (h * _sigmoid(h)).astype(BF16)
    o_ref[...] = jnp.concatenate([_mm(h[:, 0:256], w2_ref[0]), _mm(h[:, 256:512], w2_ref[1])], axis=1)


def _compress_sample(page_flat, cmpflat, pe8, w1cat, w2, *, ns):
    npages = page_flat.shape[0]
    f = cmpflat.shape[-1]
    rows = npages * ROWS_PER_PAGE
    nsteps = npages // ns
    step_rows = ns * ROWS_PER_PAGE
    grid_spec = pltpu.PrefetchScalarGridSpec(
        num_scalar_prefetch=1, grid=(nsteps,),
        in_specs=[pl.BlockSpec(memory_space=pl.ANY),
                  pl.BlockSpec((ROWS_PER_PAGE, f), lambda s, pt: (0, 0)),
                  pl.BlockSpec((f, 512), lambda s, pt: (0, 0)),
                  pl.BlockSpec((2, 256, HEAD_DIM), lambda s, pt: (0, 0, 0))],
        out_specs=pl.BlockSpec((step_rows, LANES), lambda s, pt: (s, 0)),
        scratch_shapes=[pltpu.VMEM((2, step_rows, f), F32), pltpu.SemaphoreType.DMA((2,))])
    return pl.pallas_call(
        functools.partial(_cmp_sample_body, per_step=ns),
        grid_spec=grid_spec,
        out_shape=jax.ShapeDtypeStruct((rows, LANES), F32),
        compiler_params=pltpu.CompilerParams(dimension_semantics=("arbitrary",),
                                             vmem_limit_bytes=VMEM_LIMIT),
        name="cmp_sample",
    )(page_flat, cmpflat, pe8, w1cat, w2)


HROWS = 16


def _rows_from_lanes(vec_row, pick):
    return jnp.sum(jnp.where(pick, vec_row, 0.0), axis=1, keepdims=True)


def _bf16r(x):
    return x.astype(BF16).astype(F32)


def _att_sample_body(pt_ref, fox_hbm, lf_hbm, sel_hbm, pb_ref, sf_ref, aux_ref, win_ref, mem_ref,
                     kc_ref, vc_ref, exp_ref, o_ref, fbuf, lbuf, sbuf, sem, *, npg, past):
    b = pl.program_id(0)
    nb = pl.num_programs(0)

    def copies(seq, slot):
        out = []
        for i in range(npg):
            page = pt_ref[seq * npg + i]
            out.append(pltpu.make_async_copy(
                fox_hbm.at[page], fbuf.at[slot, pl.ds(i * PAGE_SIZE, PAGE_SIZE), :], sem.at[0, slot]))
            out.append(pltpu.make_async_copy(
                lf_hbm.at[page], lbuf.at[slot, pl.ds(i * 8, 8), :], sem.at[1, slot]))
            out.append(pltpu.make_async_copy(
                sel_hbm.at[page], sbuf.at[slot, pl.ds(i * PAGE_SIZE, PAGE_SIZE), :], sem.at[2, slot]))
        return out

    @pl.when(b == 0)
    def _():
        for c in copies(0, 0):
            c.start()

    @pl.when(b + 1 < nb)
    def _():
        for c in copies(b + 1, (b + 1) % 2):
            c.start()

    slot = b % 2
    for c in copies(b, slot):
        c.wait()

    row8 = _iota((HROWS, LANES), 0)
    lane8 = _iota((HROWS, LANES), 1)
    aux = aux_ref[...]
    pbr = pb_ref[...].astype(F32)
    sfr = sf_ref[...]

    def head_rows(q256):
        r = _iota((HROWS, 256), 0)
        l = _iota((HROWS, 256), 1)
        return jnp.where(l // HEAD_DIM == r, q256, 0.0).astype(BF16)

    def pick_heads(o8):
        r = _iota((HROWS, 256), 0)
        l = _iota((HROWS, 256), 1)
        return jnp.sum(jnp.where(l // HEAD_DIM == r, o8, 0.0), axis=0, keepdims=True)

    kslab = fbuf[slot]
    qf = head_rows(pbr[:, C_FQ:C_FK])
    s = _nt(qf, kslab[:, 0:256])
    lf2 = lbuf[slot]
    r2 = _iota((npg * 8, npg * 8), 0)
    c2 = _iota((npg * 8, npg * 8), 1)
    upper = (_iota((LANES, LANES), 0) > _iota((LANES, LANES), 1)).astype(BF16)
    within = _exact_mm_rhs(lf2, upper)
    tot = jnp.sum(lf2, axis=1, keepdims=True)
    later_pages = ((c2 % 8 == r2 % 8) & (c2 // 8 > r2 // 8)).astype(BF16)
    later = _exact_mm(later_pages, jnp.broadcast_to(tot, (npg * 8, LANES)))
    rr = _iota((npg * 8, LANES), 0)
    ll = _iota((npg * 8, LANES), 1)
    lf_new = _rows_from_lanes(aux, ll == rr % 8)
    bias = within + later + lf_new
    zpad = jnp.zeros((HROWS - 8, LANES), F32)
    s = jnp.concatenate(
        [s[:, i * PAGE_SIZE:(i + 1) * PAGE_SIZE] + jnp.concatenate([bias[i * 8:(i + 1) * 8, :], zpad], axis=0)
         for i in range(npg)], axis=1)
    knew = _bf16r(sfr[:, 0:256])
    vnew = _bf16r(sfr[:, 256:512])
    s_new = jnp.sum(qf.astype(F32) * knew, axis=1, keepdims=True)
    m = jnp.maximum(jnp.max(s, axis=1, keepdims=True), s_new)
    e = jnp.exp(s - m)
    e_new = jnp.exp(s_new - m)
    den = jnp.sum(e, axis=1, keepdims=True) + e_new
    o8 = (_mm(e.astype(BF16), kslab[:, 256:512]) + _bf16r(e_new) * vnew) / den
    o_ref[:, 0:256] = pick_heads(o8)

    nq = pbr[:, C_NQ:C_NKV]
    t0 = jnp.broadcast_to(nq[:, 0:LANES], (HROWS, LANES))
    t1 = jnp.broadcast_to(nq[:, LANES:256], (HROWS, LANES))
    t0r, t1r = pltpu.roll(t0, 64, 1), pltpu.roll(t1, 64, 1)
    lo = lane8 < 64
    hrows = [jnp.where(lo, t0, 0.0), jnp.where(lo, t0r, 0.0),
             jnp.where(lo, 0.0, t1r), jnp.where(lo, 0.0, t1)]
    qn = jnp.zeros((HROWS, LANES), F32)
    for h in range(H_NSA):
        qn = jnp.where(row8 == h, hrows[h], qn)
    qn_b = qn.astype(BF16)
    slope8 = jnp.zeros((HROWS, 1), F32)
    r81 = _iota((HROWS, 1), 0)
    for h in range(H_NSA):
        slope8 = jnp.where(r81 == h, ALIBI_SLOPES[h], slope8)
    nkv_new = sfr[:, 512:1280]
    nblk = past // NSA_BLOCK

    def new_score(k128):
        return jnp.sum(qn_b.astype(F32) * _bf16r(k128), axis=1, keepdims=True)

    sc = _nt(qn_b, kc_ref[...].astype(BF16))
    cend = _iota((HROWS, nblk), 1) * NSA_BLOCK + (NSA_BLOCK - 1)
    sc = sc - slope8 * (past - cend).astype(F32)
    ec = jnp.exp(sc - jnp.max(sc, axis=1, keepdims=True))
    pc = ec / jnp.maximum(jnp.sum(ec, axis=1, keepdims=True), 1e-30)
    o_c = _mm(pc.astype(BF16), vc_ref[...].astype(BF16))
    imp = jnp.concatenate([pc[2 * g:2 * g + 1] + pc[2 * g + 1:2 * g + 2] for g in range(G_NSA)]
                          + [jnp.zeros((HROWS - G_NSA, nblk), F32)], axis=0)
    if nblk < LANES:
        imp = jnp.concatenate([imp, jnp.zeros((HROWS, LANES - nblk), F32)], axis=1)
    forced = (lane8 == 0) | (lane8 == nblk) | (lane8 == nblk - 1)
    score = jnp.where(lane8 > nblk, -1.0, jnp.where(forced, FORCE_SCORE, imp))
    rank = jnp.zeros((HROWS, LANES), F32)
    for mblk in range(nblk + 1):
        r = score[:, mblk:mblk + 1]
        rank = rank + jnp.where((r > score) | ((r == score) & (lane8 > mblk)), 1.0, 0.0)
    sel_g = jnp.where((rank < min(NSA_TOPN, nblk + 1)) & (score >= 0.0), 1.0, 0.0)
    sel_h = jnp.zeros((HROWS, LANES), F32)
    for h in range(H_NSA):
        sel_h = jnp.where(row8 == h, sel_g[h // 2:h // 2 + 1, :], sel_h)
    valid = _mm(sel_h.astype(BF16), exp_ref[...]) > 0.5
    sslab = sbuf[slot]
    ss = _nt(qn_b, sslab[:, 0:LANES]) - slope8 * (past - _iota((HROWS, past), 1)).astype(F32)
    ss = jnp.where(valid, ss, NEG_INF)
    ss_new = new_score(nkv_new[:, 256:384])
    ms = jnp.maximum(jnp.max(ss, axis=1, keepdims=True), ss_new)
    es = jnp.where(valid, jnp.exp(ss - ms), 0.0)
    es_new = jnp.exp(ss_new - ms)
    o_s = (_mm(es.astype(BF16), sslab[:, LANES:256]) + _bf16r(es_new) * _bf16r(nkv_new[:, 384:512])) \
        / (jnp.sum(es, axis=1, keepdims=True) + es_new)
    wl_ = win_ref.shape[0]
    wb = win_ref[...]
    sw = _nt(qn_b, wb[:, 0:LANES]) - slope8 * (wl_ - _iota((HROWS, wl_), 1)).astype(F32)
    sw_new = new_score(nkv_new[:, 512:640])
    mw = jnp.maximum(jnp.max(sw, axis=1, keepdims=True), sw_new)
    ew = jnp.exp(sw - mw)
    ew_new = jnp.exp(sw_new - mw)
    o_w = (_mm(ew.astype(BF16), wb[:, LANES:256]) + _bf16r(ew_new) * _bf16r(nkv_new[:, 640:768])) \
        / (jnp.sum(ew, axis=1, keepdims=True) + ew_new)
    gates = [_rows_from_lanes(aux, lane8 == LANE_GATE0 + 3 * row8 + c) for c in range(3)]
    o_n = gates[0] * o_c + gates[1] * o_s + gates[2] * o_w
    o_nr = pltpu.roll(o_n, 64, 1)
    lo1 = _iota((1, LANES), 1) < 64
    o_ref[:, 256:384] = jnp.where(lo1, o_n[0:1], o_nr[1:2])
    o_ref[:, 384:512] = jnp.where(lo1, o_nr[2:3], o_n[3:4])

    qm = head_rows(pbr[:, C_MQ:C_PU])
    mb = mem_ref[...]
    sm = _nt(qm, mb[:, 0:256])
    em = jnp.exp(sm - jnp.max(sm, axis=1, keepdims=True))
    pm = em / jnp.sum(em, axis=1, keepdims=True)
    o_ref[:, 512:768] = pick_heads(_mm(pm.astype(BF16), mb[:, 256:512]))


def _exact_mm_rhs(x, m01):
    p1, p2, p3 = _split3(x)
    return _mm(p1, m01) + _mm(p2, m01) + _mm(p3, m01)


def _attend_sample(page_flat, foxb, lft, selb, pb3, sf3, aux3, winb, memb, kcmp, vcmp, expand,
                   *, npg):
    db = pb3.shape[0]
    past = npg * PAGE_SIZE
    per_seq = lambda blk: pl.BlockSpec((None,) + blk, lambda b, pt: (b, 0, 0))
    grid_spec = pltpu.PrefetchScalarGridSpec(
        num_scalar_prefetch=1, grid=(db,),
        in_specs=[pl.BlockSpec(memory_space=pl.ANY), pl.BlockSpec(memory_space=pl.ANY),
                  pl.BlockSpec(memory_space=pl.ANY),
                  per_seq((1, N_PB)), per_seq((1, 1280)), per_seq((1, LANES)),
                  per_seq(winb.shape[1:]), per_seq(memb.shape[1:]),
                  per_seq(kcmp.shape[1:]), per_seq(vcmp.shape[1:]),
                  pl.BlockSpec(expand.shape, lambda b, pt: (0, 0))],
        out_specs=per_seq((1, 768)),
        scratch_shapes=[pltpu.VMEM((2, past, 512), BF16), pltpu.VMEM((2, npg * 8, LANES), F32),
                        pltpu.VMEM((2, past, 256), BF16), pltpu.SemaphoreType.DMA((3, 2))])
    return pl.pallas_call(
        functools.partial(_att_sample_body, npg=npg, past=past),
        grid_spec=grid_spec,
        out_shape=jax.ShapeDtypeStruct((db, 1, 768), F32),
        compiler_params=pltpu.CompilerParams(dimension_semantics=("arbitrary",),
                                             vmem_limit_bytes=VMEM_LIMIT),
        name="att_sample",
    )(page_flat, foxb, lft, selb, pb3, sf3, aux3, winb, memb, kcmp, vcmp, expand)


def _post_sample_body(oatt_ref, pz_ref, sp_ref, x_ref, wout_ref, pw_ref, ps_ref, fg_ref, y_ref,
                      *, rows, past, final):
    u = pz_ref[:, 0:256]
    grp, wl = _pool_window_lane((rows, 256))
    tsum = u
    for j in range(1, POOL_BUF + 1):
        tsum = tsum + jnp.where(wl > j, sp_ref[:, POOL_BUF - j, :], 0.0)
    cnt = jnp.minimum(past + 1, wl).astype(F32)
    o_pool = _mm((tsum / cnt - u).astype(BF16), pw_ref[...]) * ps_ref[...]
    parts = [oatt_ref[:, c * LANES:(c + 1) * LANES] for c in range(4)]
    parts += [o_pool[:, 0:LANES], o_pool[:, LANES:256]]
    parts += [oatt_ref[:, 512:640], oatt_ref[:, 640:768]]
    z_of = lambda c: pz_ref[:, 256 + c * LANES:256 + (c + 1) * LANES]
    y_ref[...] = _mix_out(x_ref[...], parts, z_of, wout_ref, fg_ref, final)


def _post_sample(oatt, pz, spool, x2d, wout, pw_bd, ps, fg, *, past, final):
    rows, d = x2d.shape
    return pl.pallas_call(
        functools.partial(_post_sample_body, rows=rows, past=past, final=final),
        out_shape=jax.ShapeDtypeStruct((rows, d), F32),
        compiler_params=pltpu.CompilerParams(vmem_limit_bytes=VMEM_LIMIT),
        name="post_sample",
    )(oatt, pz, spool, x2d, wout, pw_bd, ps, fg)


def _prep_layer_weights(norm_g, w_in, b_fgt, nsa_pe, nsa_w1, nsa_w2, pool_w, pool_scale,
                        w_mem_kv, w_out):
    d = w_in.shape[0]
    o = 0
    segs = {}
    for name, width in (("fqkv", 768), ("ff", H_FOX), ("nq", 256), ("nkv", 768),
                        ("ng", 3 * H_NSA), ("pu", 256), ("mq", 256), ("z", d)):
        segs[name] = w_in[:, o:o + width]
        o += width
    pad = jnp.zeros((d, LANES - H_FOX - 3 * H_NSA), w_in.dtype)
    wp = jnp.concatenate([segs["fqkv"], segs["nq"], segs["nkv"], segs["mq"], segs["pu"],
                          segs["z"], segs["ff"], segs["ng"], pad], axis=1).astype(BF16)
    bvec = jnp.zeros((1, LANES), F32).at[0, 0:H_FOX].set(b_fgt.astype(F32))
    gw = D_GROUP // len(POOL_WINDOWS)
    pw_bd = jnp.zeros((D_GROUP, D_GROUP), F32)
    for g in range(len(POOL_WINDOWS)):
        pw_bd = pw_bd.at[g * gw:(g + 1) * gw, g * gw:(g + 1) * gw].set(pool_w[g])
    return dict(g=norm_g.reshape(1, d), wp=wp, bvec=bvec,
                pe_flat=nsa_pe.reshape(2, 1, NSA_BLOCK * HEAD_DIM),
                w1=nsa_w1.astype(BF16), w2=nsa_w2.astype(BF16),
                w1cat=jnp.concatenate([nsa_w1[0], nsa_w1[1]], axis=1).astype(BF16),
                pw_bd=pw_bd.astype(BF16), ps=pool_scale.reshape(1, D_GROUP).astype(F32),
                wmkv=w_mem_kv.astype(BF16), wout=w_out.astype(BF16))


def _layer_prompt(x, mem, lw, fg, final, tile):
    b, t, d = x.shape
    n = b * t
    nb = t // NSA_BLOCK
    pb, sf, pz, aux, cq, ct = _project(x.reshape(n, d), lw["g"], lw["wp"], lw["bvec"],
                                       tm=tile, tiles_per_seq=t // tile, with_cum=True)
    pb3 = pb.reshape(b, t, N_PB)
    o_a = _fox_prompt(pb3, cq.reshape(b, t, LANES), ct, tq=tile)
    kvc = sf[:, 512:768].reshape(b, nb, NSA_BLOCK, 2, G_NSA, HEAD_DIM)
    flat = jnp.transpose(kvc, (3, 0, 1, 4, 2, 5)).reshape(2, b * nb * G_NSA, NSA_BLOCK * HEAD_DIM)
    cmp = _compress_prompt(flat, lw["pe_flat"], lw["w1"], lw["w2"]).reshape(2, b, nb, LANES)
    cmp = jnp.pad(cmp, ((0, 0), (0, 0), (0, NBLK_PAD - nb), (0, 0))).astype(BF16)
    o_b = _nsa_prompt(pb3, cmp[0], cmp[1], aux.reshape(b, t, LANES), tq=tile, nb=nb)
    n_mem = mem.shape[1]
    mem_kv, memb = _mem_project(mem.reshape(b * n_mem, d), lw["wmkv"])
    y = _post_prompt(o_a, o_b, pz.reshape(b, t, 1280), pb3, memb.reshape(b, n_mem, 512), x,
                     lw["wout"], lw["pw_bd"], lw["ps"], fg, tm=tile, final=final)
    w_keep = min(NSA_WINDOW, t)
    sf3 = sf.reshape(b, t, 1280)
    state = (sf3[:, :, 0:512].reshape(b, t, 2, H_FOX, HEAD_DIM),
             aux.reshape(b, t, LANES)[:, :, 0:H_FOX],
             sf3[:, :, 512:1024].reshape(b, t, 4, G_NSA, HEAD_DIM),
             sf3[:, t - w_keep:, 1024:1280].reshape(b, w_keep, 2, G_NSA, HEAD_DIM),
             pz.reshape(b, t, 1280)[:, t - POOL_BUF:, 0:256],
             mem_kv.reshape(b, n_mem, 2, H_MEM, HEAD_DIM))
    return y, state


def _layer_sample(x, c_fox_kv, c_fox_lf, c_nsa_kv, s_win, s_pool, c_mem_kv, page_table, lw, fg,
                  final):
    db, ds, d = x.shape
    npg = page_table.shape[1]
    past = npg * PAGE_SIZE
    n_phys = c_fox_kv.shape[0]
    nblk = past // NSA_BLOCK
    pb, sf, pz, aux = _project(x.reshape(db, d), lw["g"], lw["wp"], lw["bvec"],
                               tm=db, tiles_per_seq=1, with_cum=False)
    page_flat = page_table.reshape(db * npg).astype(I32)
    foxb = c_fox_kv.reshape(n_phys, PAGE_SIZE, 512).astype(BF16)
    lft = jnp.pad(jnp.transpose(c_fox_lf.astype(F32), (0, 2, 1)), ((0, 0), (0, 8 - H_FOX), (0, 0)))
    selb = c_nsa_kv[:, :, 2:4].reshape(n_phys, PAGE_SIZE, 256).astype(BF16)
    cmpflat = jnp.transpose(
        c_nsa_kv[:, :, 0:2].reshape(n_phys, PAGE_SIZE // NSA_BLOCK, NSA_BLOCK, 2, G_NSA, HEAD_DIM),
        (0, 3, 4, 1, 2, 5)).reshape(n_phys, ROWS_PER_PAGE, NSA_BLOCK * HEAD_DIM)
    pe8 = jnp.repeat(lw["pe_flat"].reshape(2, NSA_BLOCK * HEAD_DIM), ROWS_PER_PAGE // 2, axis=0)
    cmp = _compress_sample(page_flat, cmpflat, pe8, lw["w1cat"], lw["w2"], ns=64)
    cmp = cmp.reshape(db, npg, 2, G_NSA, PAGE_SIZE // NSA_BLOCK, LANES)
    kcmp = jnp.transpose(cmp[:, :, 0, :, :, 0:HEAD_DIM], (0, 1, 3, 2, 4)).reshape(db, nblk, LANES)
    vcmp = jnp.transpose(cmp[:, :, 1, :, :, HEAD_DIM:LANES], (0, 1, 3, 2, 4)).reshape(db, nblk, LANES)
    w_buf = s_win.shape[1]
    winb = s_win.reshape(db, w_buf, 256).astype(BF16)
    n_mem = c_mem_kv.shape[1]
    memb = c_mem_kv.reshape(db, n_mem, 512).astype(BF16)
    expand = (np.arange(LANES)[:, None] == (np.arange(past)[None, :] // NSA_BLOCK)).astype(np.float32)
    oatt = _attend_sample(page_flat, foxb, lft, selb, pb.reshape(db, 1, N_PB), sf.reshape(db, 1, 1280),
                          aux.reshape(db, 1, LANES), winb, memb, kcmp, vcmp,
                          jnp.asarray(expand, BF16), npg=npg)
    y = _post_sample(oatt.reshape(db, 768), pz, s_pool, x.reshape(db, d), lw["wout"], lw["pw_bd"],
                     lw["ps"], fg, past=past, final=final)
    new_win = jnp.concatenate([s_win[:, 1:], sf[:, 1024:1280].reshape(db, 1, 2, G_NSA, HEAD_DIM)],
                              axis=1)
    new_pool = jnp.concatenate([s_pool[:, 1:], pz[:, None, 0:256]], axis=1)
    state = (sf[:, 0:512].reshape(db, ds, 2, H_FOX, HEAD_DIM), aux[:, 0:H_FOX].reshape(db, ds, H_FOX),
             sf[:, 512:1024].reshape(db, ds, 4, G_NSA, HEAD_DIM), new_win, new_pool)
    return y.reshape(db, ds, d), state


def kernel(x_prompt, x_sample, cache_fox_kv, cache_fox_lf, cache_nsa_kv, state_nsa_win, state_pool,
           cache_mem_kv, page_table, mem_prompt, norm_g, w_in, b_fgt, nsa_pe, nsa_w1, nsa_w2, pool_w,
           pool_scale, w_mem_kv, w_out, final_g):
    depth = norm_g.shape[0]
    t = x_prompt.shape[1]
    assert x_sample.shape[1] == 1 and t % NSA_WINDOW == 0 and t // NSA_BLOCK <= NBLK_PAD
    hp, hs = x_prompt, x_sample
    fg = final_g.reshape(1, -1).astype(F32)
    sp = [[] for _ in range(6)]
    ss = [[] for _ in range(5)]
    for l in range(depth):
        lw = _prep_layer_weights(norm_g[l], w_in[l], b_fgt[l], nsa_pe[l], nsa_w1[l], nsa_w2[l],
                                 pool_w[l], pool_scale[l], w_mem_kv[l], w_out[l])
        final = l == depth - 1
        hp, st_p = _layer_prompt(hp, mem_prompt, lw, fg, final, NSA_WINDOW)
        hs, st_s = _layer_sample(hs, cache_fox_kv[l], cache_fox_lf[l], cache_nsa_kv[l],
                                 state_nsa_win[l], state_pool[l], cache_mem_kv[l], page_table, lw,
                                 fg, final)
        for acc, s in zip(sp, st_p):
            acc.append(s)
        for acc, s in zip(ss, st_s):
            acc.append(s)
    return (hp, hs) + tuple(jnp.stack(a) for a in sp) + tuple(jnp.stack(a) for a in ss)
```

```python
import functools

import numpy as np
import jax
import jax.numpy as jnp
from jax import lax
from jax.experimental import pallas as pl
from jax.experimental.pallas import tpu as pltpu

F32 = jnp.float32
BF16 = jnp.bfloat16
I32 = jnp.int32

HEAD_DIM = 64
H_FOX = 4
H_NSA = 4
G_NSA = 2
H_MEM = 4
D_GROUP = 256
POOL_WINDOWS = (2, 4, 8, 16)
POOL_BUF = 15
NSA_BLOCK = 64
NSA_TOPN = 16
NSA_WINDOW = 512
PAGE_SIZE = 128
RMS_EPS = 1e-6
NEG_INF = -1e30
FORCE_SCORE = 1e4
SCALE = HEAD_DIM ** -0.5
ALIBI_SLOPES = tuple(2.0 ** (-8.0 * (h + 1) / H_NSA) for h in range(H_NSA))

LANES = 128
NBLK_PAD = 128
VMEM_LIMIT = 56 * 1024 * 1024

C_FQ, C_FK, C_FV, C_NQ, C_NKV, C_MQ, C_PU, C_Z, C_SM, C_END = (
    0, 256, 512, 768, 1024, 1792, 2048, 2304, 3328, 3456)
N_PB = C_PU
LANE_GATE0 = 4


def _nt(a, b):
    return lax.dot_general(a, b, (((1,), (1,)), ((), ())), preferred_element_type=F32)


def _mm(a, b):
    return jnp.dot(a, b, preferred_element_type=F32)


def _split3(x):
    p1 = x.astype(BF16)
    r1 = x - p1.astype(F32)
    p2 = r1.astype(BF16)
    p3 = (r1 - p2.astype(F32)).astype(BF16)
    return p1, p2, p3


def _exact_mm(m01, x):
    p1, p2, p3 = _split3(x)
    return _mm(m01, p1) + _mm(m01, p2) + _mm(m01, p3)


def _sigmoid(x):
    return 1.0 / (1.0 + jnp.exp(-x))


def _log_sigmoid(x):
    return jnp.minimum(x, 0.0) - jnp.log1p(jnp.exp(-jnp.abs(x)))


def _iota(shape, dim):
    return lax.broadcasted_iota(I32, shape, dim)


def _proj_body(x_ref, g_ref, w_ref, b_ref, pb_ref, pz_ref, aux_ref, *rest, tm, tiles_per_seq, prompt):
    x = x_ref[...]
    ms = jnp.mean(x * x, axis=-1, keepdims=True)
    xn = (x * lax.rsqrt(ms + RMS_EPS) * g_ref[...]).astype(BF16)

    def seg(a, b):
        return _mm(xn, w_ref[:, a:b])

    pb_ref[:, C_FQ:C_FK] = (seg(C_FQ, C_FK) * SCALE).astype(BF16)
    fkv = seg(C_FK, C_NQ)
    pb_ref[:, C_FK:C_NQ] = fkv.astype(BF16)
    pb_ref[:, C_NQ:C_NKV] = (seg(C_NQ, C_NKV) * SCALE).astype(BF16)
    nkv = seg(C_NKV, C_MQ)
    pb_ref[:, C_NKV:C_MQ] = nkv.astype(BF16)
    pb_ref[:, C_MQ:C_PU] = (seg(C_MQ, C_PU) * SCALE).astype(BF16)
    pz_ref[...] = seg(C_PU, C_SM)
    small = seg(C_SM, C_END)

    lane = _iota((tm, LANES), 1)
    lf = _log_sigmoid(small + b_ref[...])
    aux = jnp.where(lane < H_FOX, lf, jnp.where(lane < LANE_GATE0 + 3 * H_NSA, _sigmoid(small), 0.0))
    aux_ref[...] = aux
    if prompt:
        sfc_ref, cq_ref, ct_ref, lft_ref, fkvt_ref, nkvt_ref, nwt_ref, carry_ref = rest
        sfc_ref[...] = nkv[:, 0:256]
        fkvt_ref[...] = fkv.T
        nkvt_ref[...] = nkv[:, 0:512].T
        nwt_ref[...] = nkv[:, 512:768].T
        lfm = jnp.where(lane < H_FOX, lf, 0.0)
        lft_ref[...] = lfm.T[0:8, :]
        tri = (_iota((LANES, LANES), 0) >= _iota((LANES, LANES), 1)).astype(BF16)
        first = (pl.program_id(0) % tiles_per_seq) == 0
        carry = jnp.where(first, 0.0, carry_ref[0:1, :])
        for r in range(tm // LANES):
            blk = lfm[r * LANES:(r + 1) * LANES]
            cblk = _exact_mm(tri, blk) + carry
            carry = cblk[LANES - 1:LANES, :]
            cq_ref[r * LANES:(r + 1) * LANES, :] = cblk
            ct_ref[:, r * LANES:(r + 1) * LANES] = cblk.T[0:8, :]
        carry_ref[0:1, :] = carry
    else:
        sf_ref, sft_ref, auxt_ref = rest
        sf_ref[:, 0:512] = fkv
        sf_ref[:, 512:1280] = nkv
        sft_ref[0:512, :] = fkv.T
        sft_ref[512:1280, :] = nkv.T
        auxt_ref[...] = aux.T


def _project(x2d, g, wp, bvec, *, tm, tiles_per_seq, prompt):
    n, d = x2d.shape
    grid = (n // tm,)
    row = lambda i: (i, 0)
    const = lambda i: (0, 0)
    out_shape = [jax.ShapeDtypeStruct((n, N_PB), BF16), jax.ShapeDtypeStruct((n, 1280), F32),
                 jax.ShapeDtypeStruct((n, LANES), F32)]
    out_specs = [pl.BlockSpec((tm, N_PB), row), pl.BlockSpec((tm, 1280), row),
                 pl.BlockSpec((tm, LANES), row)]
    scratch = []
    if prompt:
        nseq = n // (tm * tiles_per_seq)
        t = tm * tiles_per_seq
        seq_t = lambda rows: pl.BlockSpec(
            (None, rows, tm), lambda i: (i // tiles_per_seq, 0, i % tiles_per_seq))
        out_shape += [jax.ShapeDtypeStruct((n, 256), F32), jax.ShapeDtypeStruct((n, LANES), F32),
                      jax.ShapeDtypeStruct((nseq, 8, t), F32), jax.ShapeDtypeStruct((nseq, 8, t), F32),
                      jax.ShapeDtypeStruct((nseq, 512, t), F32), jax.ShapeDtypeStruct((nseq, 512, t), F32),
                      jax.ShapeDtypeStruct((nseq, 256, t), F32)]
        out_specs += [pl.BlockSpec((tm, 256), row), pl.BlockSpec((tm, LANES), row),
                      seq_t(8), seq_t(8), seq_t(512), seq_t(512), seq_t(256)]
        scratch = [pltpu.VMEM((8, LANES), F32)]
    else:
        assert grid == (1,)
        out_shape += [jax.ShapeDtypeStruct((n, 1280), F32), jax.ShapeDtypeStruct((1280, n), F32),
                      jax.ShapeDtypeStruct((LANES, n), F32)]
        out_specs += [pl.BlockSpec((tm, 1280), row), pl.BlockSpec((1280, tm), const),
                      pl.BlockSpec((LANES, tm), const)]
    return pl.pallas_call(
        functools.partial(_proj_body, tm=tm, tiles_per_seq=tiles_per_seq, prompt=prompt),
        grid=grid,
        in_specs=[pl.BlockSpec((tm, d), row), pl.BlockSpec((1, d), const),
                  pl.BlockSpec((d, C_END), const), pl.BlockSpec((1, LANES), const)],
        out_specs=out_specs, out_shape=out_shape, scratch_shapes=scratch,
        compiler_params=pltpu.CompilerParams(dimension_semantics=("arbitrary",),
                                             vmem_limit_bytes=VMEM_LIMIT),
        name="proj",
    )(x2d, g, wp, bvec)


def _memkv_body(m_ref, w_ref, ot_ref, ob_ref):
    r = _mm(m_ref[...].astype(BF16), w_ref[...])
    ot_ref[...] = r.T
    ob_ref[...] = r.astype(BF16)


def _mem_project(mem3, w):
    b, n_mem, d = mem3.shape
    e = w.shape[1]
    return pl.pallas_call(
        _memkv_body, grid=(b,),
        in_specs=[pl.BlockSpec((None, n_mem, d), lambda i: (i, 0, 0)),
                  pl.BlockSpec((d, e), lambda i: (0, 0))],
        out_specs=[pl.BlockSpec((None, e, n_mem), lambda i: (i, 0, 0)),
                   pl.BlockSpec((None, n_mem, e), lambda i: (i, 0, 0))],
        out_shape=[jax.ShapeDtypeStruct((b, e, n_mem), F32), jax.ShapeDtypeStruct((b, n_mem, e), BF16)],
        name="memkv",
    )(mem3, w)


def _fox_body(q_ref, k_ref, v_ref, cq_ref, ct_ref, o_ref, m_ref, l_ref, acc_ref, *, tq, tk):
    qi = pl.program_id(1)
    ki = pl.program_id(2)

    @pl.when(ki == 0)
    def _():
        m_ref[...] = jnp.full(m_ref.shape, NEG_INF, F32)
        l_ref[...] = jnp.zeros(l_ref.shape, F32)
        acc_ref[...] = jnp.zeros(acc_ref.shape, F32)

    lane = _iota((tq, LANES), 1)

    def step(masked):
        if masked:
            causal = _iota((tq, tk), 0) >= _iota((tq, tk), 1)
        for h in range(H_FOX):
            pr, hh = divmod(h, 2)
            cs = slice(pr * LANES, (pr + 1) * LANES)
            q2 = q_ref[:, cs]
            qm = jnp.where((lane >= 64 * hh) & (lane < 64 * hh + 64), q2, jnp.zeros_like(q2))
            s = _nt(qm, k_ref[:, cs])
            s = s + cq_ref[:, h:h + 1] - ct_ref[h:h + 1, :]
            if masked:
                s = jnp.where(causal, s, NEG_INF)
            m_prev = m_ref[h]
            m_new = jnp.maximum(m_prev, jnp.max(s, axis=1, keepdims=True))
            alpha = jnp.exp(m_prev - m_new)
            p = jnp.exp(s - m_new)
            l_ref[h] = alpha * l_ref[h] + jnp.sum(p, axis=1, keepdims=True)
            acc_ref[h] = alpha * acc_ref[h] + _mm(p.astype(BF16), v_ref[:, cs])
            m_ref[h] = m_new

    @pl.when(ki < qi)
    def _():
        step(False)

    @pl.when(ki == qi)
    def _():
        step(True)
        for pr in range(H_FOX // 2):
            o0 = acc_ref[2 * pr] / l_ref[2 * pr]
            o1 = acc_ref[2 * pr + 1] / l_ref[2 * pr + 1]
            o_ref[:, pr * LANES:(pr + 1) * LANES] = jnp.where(lane < 64, o0, o1)


def _fox_prompt(pb3, cq3, ct, *, tq):
    b, t, _ = pb3.shape
    nq = t // tq
    kv = lambda bi, qi, ki: (bi, jnp.minimum(ki, qi), 0)
    return pl.pallas_call(
        functools.partial(_fox_body, tq=tq, tk=tq),
        grid=(b, nq, nq),
        in_specs=[pl.BlockSpec((None, tq, 256), lambda bi, qi, ki: (bi, qi, C_FQ // 256)),
                  pl.BlockSpec((None, tq, 256), lambda bi, qi, ki: (bi, jnp.minimum(ki, qi), C_FK // 256)),
                  pl.BlockSpec((None, tq, 256), lambda bi, qi, ki: (bi, jnp.minimum(ki, qi), C_FV // 256)),
                  pl.BlockSpec((None, tq, LANES), lambda bi, qi, ki: (bi, qi, 0)),
                  pl.BlockSpec((None, 8, tq), lambda bi, qi, ki: (bi, 0, jnp.minimum(ki, qi)))],
        out_specs=pl.BlockSpec((None, tq, 256), lambda bi, qi, ki: (bi, qi, 0)),
        out_shape=jax.ShapeDtypeStruct((b, t, 256), F32),
        scratch_shapes=[pltpu.VMEM((H_FOX, tq, 1), F32), pltpu.VMEM((H_FOX, tq, 1), F32),
                        pltpu.VMEM((H_FOX, tq, LANES), F32)],
        compiler_params=pltpu.CompilerParams(
            dimension_semantics=("parallel", "parallel", "arbitrary"),
            vmem_limit_bytes=VMEM_LIMIT),
        name="fox_prompt",
    )(pb3, pb3, pb3, cq3, ct)


def _cmp_body(x_ref, pe_ref, w1_ref, w2_ref, o_ref):
    xb = (x_ref[...] + pe_ref[...]).astype(BF16)
    h = _mm(xb, w1_ref[...])
    h = h * _sigmoid(h)
    o_ref[...] = _mm(h.astype(BF16), w2_ref[...])


def _compress_prompt(flat, pe_flat, w1, w2):
    _, m, f = flat.shape
    tm = min(m, 512)
    return pl.pallas_call(
        _cmp_body, grid=(2, m // tm),
        in_specs=[pl.BlockSpec((None, tm, f), lambda j, i: (j, i, 0)),
                  pl.BlockSpec((None, 1, f), lambda j, i: (j, 0, 0)),
                  pl.BlockSpec((None, f, 256), lambda j, i: (j, 0, 0)),
                  pl.BlockSpec((None, 256, HEAD_DIM), lambda j, i: (j, 0, 0))],
        out_specs=pl.BlockSpec((None, tm, HEAD_DIM), lambda j, i: (j, i, 0)),
        out_shape=jax.ShapeDtypeStruct((2, m, HEAD_DIM), F32),
        compiler_params=pltpu.CompilerParams(vmem_limit_bytes=VMEM_LIMIT),
        name="cmp_prompt",
    )(flat, pe_flat, w1, w2)


def _masked_softmax_cols(s, valid):
    s = jnp.where(valid, s, NEG_INF)
    e = jnp.where(valid, jnp.exp(s - jnp.max(s, axis=0, keepdims=True)), 0.0)
    return e / jnp.maximum(jnp.sum(e, axis=0, keepdims=True), 1e-30)


def _select_blocks(score, nb, topn):
    nbp, n = score.shape
    blk = _iota((nbp, n), 0)
    rank = jnp.zeros((nbp, n), F32)
    for m in range(nb):
        r = score[m:m + 1, :]
        rank = rank + jnp.where((r > score) | ((r == score) & (blk > m)), 1.0, 0.0)
    return jnp.where((rank < topn) & (score >= 0.0), 1.0, 0.0)


def _nsa_body(q_ref, ks_ref, vs_ref, kwc_ref, kwp_ref, vwc_ref, vwp_ref, kc_ref, vc_ref,
              aux_ref, o_ref, sel_ref, m_ref, l_ref, acc_ref, oc_ref, ow_ref, *, tq, nb, topn):
    qi = pl.program_id(1)
    ki = pl.program_id(2)
    q0 = qi * tq
    tk = tq
    lane = _iota((tq, LANES), 1)
    row2 = _iota((2 * tq, 1), 0)
    rq_col = jnp.where(row2 >= tq, row2 - tq, row2)

    def qstack(g):
        q2 = q_ref[:, g * LANES:(g + 1) * LANES].astype(F32)
        q2r = pltpu.roll(q2, 64, 1)
        ing = (lane >= 64 * g) & (lane < 64 * g + 64)
        qa, qb = (q2, q2r) if g == 0 else (q2r, q2)
        return jnp.concatenate([jnp.where(ing, qa, 0.0), jnp.where(ing, qb, 0.0)],
                               axis=0).astype(BF16)

    def slope_col(g):
        return jnp.where(row2 < tq, ALIBI_SLOPES[2 * g], ALIBI_SLOPES[2 * g + 1])

    @pl.when(ki == 0)
    def _():
        m_ref[...] = jnp.full(m_ref.shape, NEG_INF, F32)
        l_ref[...] = jnp.zeros(l_ref.shape, F32)
        acc_ref[...] = jnp.zeros(acc_ref.shape, F32)
        for g in range(G_NSA):
            qs = qstack(g)
            st = _nt(kc_ref[...], qs)
            blk = _iota((NBLK_PAD, 2 * tq), 0)
            col = _iota((NBLK_PAD, 2 * tq), 1)
            qpos = q0 + jnp.where(col >= tq, col - tq, col)
            slope = jnp.where(col < tq, ALIBI_SLOPES[2 * g], ALIBI_SLOPES[2 * g + 1])
            cend = blk * NSA_BLOCK + (NSA_BLOCK - 1)
            st = st - slope * (qpos - cend).astype(F32)
            pt = _masked_softmax_cols(st, (cend <= qpos) & (blk < nb))
            oc_ref[g] = _mm(pt.T.astype(BF16), vc_ref[...])
            imp = pt[:, :tq] + pt[:, tq:]
            nbp = -(-nb // 8) * 8
            blk2 = _iota((nbp, tq), 0)
            cur = (q0 + _iota((nbp, tq), 1)) // NSA_BLOCK
            forced = (blk2 == 0) | (blk2 == cur) | (blk2 == cur - 1)
            score = jnp.where(blk2 > cur, -1.0, jnp.where(forced, FORCE_SCORE, imp[:nbp]))
            selt = _select_blocks(score, nb, topn)
            if nbp < NBLK_PAD:
                selt = jnp.concatenate([selt, jnp.zeros((NBLK_PAD - nbp, tq), F32)], axis=0)
            sel_ref[g] = selt.T.astype(BF16)
            rq = jnp.where(_iota((2 * tq, tk), 0) >= tq, _iota((2 * tq, tk), 0) - tq,
                           _iota((2 * tq, tk), 0))
            d1 = rq - _iota((2 * tq, tk), 1)
            v1 = d1 >= 0
            s1 = jnp.where(v1, _nt(qs, kwc_ref[...]) - slope_col(g) * d1.astype(F32), NEG_INF)
            d2 = d1 + tq
            v2 = (d2 <= NSA_WINDOW) & (qi > 0)
            s2 = jnp.where(v2, _nt(qs, kwp_ref[...]) - slope_col(g) * d2.astype(F32), NEG_INF)
            mw = jnp.maximum(jnp.max(s1, axis=1, keepdims=True), jnp.max(s2, axis=1, keepdims=True))
            e1 = jnp.where(v1, jnp.exp(s1 - mw), 0.0)
            e2 = jnp.where(v2, jnp.exp(s2 - mw), 0.0)
            lw = jnp.sum(e1, axis=1, keepdims=True) + jnp.sum(e2, axis=1, keepdims=True)
            ow_ref[g] = (_mm(e1.astype(BF16), vwc_ref[...]) + _mm(e2.astype(BF16), vwp_ref[...])) \
                / jnp.maximum(lw, 1e-30)

    def sel_step(diag):
        expand = (_iota((NBLK_PAD, tk), 0) ==
                  ki * (tk // NSA_BLOCK) + _iota((NBLK_PAD, tk), 1) // NSA_BLOCK).astype(BF16)
        dist1 = _iota((tq, tk), 0) - _iota((tq, tk), 1)
        for g in range(G_NSA):
            qs = qstack(g)
            valid = _mm(sel_ref[g], expand) > 0.5
            if diag:
                valid = valid & (dist1 >= 0)
            valid2 = jnp.concatenate([valid, valid], axis=0)
            dist = jnp.concatenate([dist1, dist1], axis=0) + (q0 - ki * tk)
            s = _nt(qs, ks_ref[...]) - slope_col(g) * dist.astype(F32)
            s = jnp.where(valid2, s, NEG_INF)
            m_prev = m_ref[g]
            m_new = jnp.maximum(m_prev, jnp.max(s, axis=1, keepdims=True))
            alpha = jnp.exp(m_prev - m_new)
            p = jnp.where(valid2, jnp.exp(s - m_new), 0.0)
            l_ref[g] = alpha * l_ref[g] + jnp.sum(p, axis=1, keepdims=True)
            acc_ref[g] = alpha * acc_ref[g] + _mm(p.astype(BF16), vs_ref[...])
            m_ref[g] = m_new

    @pl.when(ki < qi)
    def _():
        sel_step(False)

    @pl.when(ki == qi)
    def _():
        sel_step(True)
        for g in range(G_NSA):
            o_s = acc_ref[g] / jnp.maximum(l_ref[g], 1e-30)
            o_c = oc_ref[g]
            o_w = ow_ref[g]
            outs = []
            for hh in range(2):
                h = 2 * g + hh
                rs = slice(hh * tq, (hh + 1) * tq)
                gl = LANE_GATE0 + 3 * h
                outs.append(aux_ref[:, gl:gl + 1] * o_c[rs] + aux_ref[:, gl + 1:gl + 2] * o_s[rs]
                            + aux_ref[:, gl + 2:gl + 3] * o_w[rs])
            oa, ob = outs
            if g == 0:
                ob = pltpu.roll(ob, 64, 1)
            else:
                oa = pltpu.roll(oa, 64, 1)
            o_ref[:, g * LANES:(g + 1) * LANES] = jnp.where(lane < 64, oa, ob)


def _nsa_prompt(pb3, kcmp, vcmp, aux3, *, tq, nb):
    b, t, _ = pb3.shape
    nq = t // tq
    ct = C_NKV // LANES
    cur = lambda c: (lambda bi, qi, ki: (bi, qi, c))
    prev = lambda c: (lambda bi, qi, ki: (bi, jnp.maximum(qi - 1, 0), c))
    kvt = lambda c: (lambda bi, qi, ki: (bi, jnp.minimum(ki, qi), c))
    cmp_spec = pl.BlockSpec((None, NBLK_PAD, LANES), lambda bi, qi, ki: (bi, 0, 0))
    tile = lambda f: pl.BlockSpec((None, tq, LANES), f)
    return pl.pallas_call(
        functools.partial(_nsa_body, tq=tq, nb=nb, topn=min(NSA_TOPN, nb)),
        grid=(b, nq, nq),
        in_specs=[pl.BlockSpec((None, tq, 256), lambda bi, qi, ki: (bi, qi, C_NQ // 256)),
                  tile(kvt(ct + 2)), tile(kvt(ct + 3)),
                  tile(cur(ct + 4)), tile(prev(ct + 4)), tile(cur(ct + 5)), tile(prev(ct + 5)),
                  cmp_spec, cmp_spec,
                  pl.BlockSpec((None, tq, LANES), lambda bi, qi, ki: (bi, qi, 0))],
        out_specs=pl.BlockSpec((None, tq, 256), lambda bi, qi, ki: (bi, qi, 0)),
        out_shape=jax.ShapeDtypeStruct((b, t, 256), F32),
        scratch_shapes=[pltpu.VMEM((G_NSA, tq, NBLK_PAD), BF16),
                        pltpu.VMEM((G_NSA, 2 * tq, 1), F32), pltpu.VMEM((G_NSA, 2 * tq, 1), F32),
                        pltpu.VMEM((G_NSA, 2 * tq, LANES), F32),
                        pltpu.VMEM((G_NSA, 2 * tq, LANES), F32),
                        pltpu.VMEM((G_NSA, 2 * tq, LANES), F32)],
        compiler_params=pltpu.CompilerParams(
            dimension_semantics=("parallel", "parallel", "arbitrary"),
            vmem_limit_bytes=VMEM_LIMIT),
        name="nsa_prompt",
    )(pb3, pb3, pb3, pb3, pb3, pb3, pb3, kcmp, vcmp, aux3)


def _mem_attend(mq_ref, mk_ref, mv_ref, rows):
    lane = _iota((rows, LANES), 1)
    pairs = []
    for pr in range(H_MEM // 2):
        cs = slice(pr * LANES, (pr + 1) * LANES)
        q2 = mq_ref[:, cs]
        halves = []
        for hh in range(2):
            qm = jnp.where((lane >= 64 * hh) & (lane < 64 * hh + 64), q2, jnp.zeros_like(q2))
            s = _nt(qm, mk_ref[:, cs])
            e = jnp.exp(s - jnp.max(s, axis=1, keepdims=True))
            p = e / jnp.sum(e, axis=1, keepdims=True)
            halves.append(_mm(p.astype(BF16), mv_ref[:, cs]))
        pairs.append(jnp.where(lane < 64, halves[0], halves[1]))
    return pairs


def _mix_out(x, parts, z_of, wout_ref, fg_ref, final):
    y = x
    for c, part in enumerate(parts):
        z = z_of(c)
        gated = (part * (z * _sigmoid(z))).astype(BF16)
        y = y + _mm(gated, wout_ref[c * LANES:(c + 1) * LANES, :])
    if final:
        y = y * lax.rsqrt(jnp.mean(y * y, axis=-1, keepdims=True) + RMS_EPS) * fg_ref[...]
    return y


def _pool_window_lane(shape):
    grp = _iota(shape, 1) // (D_GROUP // len(POOL_WINDOWS))
    wl = jnp.where(grp == 0, POOL_WINDOWS[0], jnp.where(grp == 1, POOL_WINDOWS[1],
                   jnp.where(grp == 2, POOL_WINDOWS[2], POOL_WINDOWS[3])))
    return grp, wl


HALO = 32


def _post_body(oa_ref, ob_ref, pz_ref, halo_ref, mq_ref, mk_ref, mv_ref, x_ref, wout_ref, pw_ref,
               ps_ref, fg_ref, y_ref, s0, s1, s2, s3, *, tm, final):
    i = pl.program_id(1)
    u = pz_ref[:, 0:256]
    s0[0:HALO, :] = jnp.where(i > 0, halo_ref[...], 0.0)
    s0[HALO:HALO + tm, :] = u
    n = tm + HALO
    s1[8:n, :] = s0[8:n, :] + s0[7:n - 1, :]
    s2[16:n, :] = s1[16:n, :] + s1[14:n - 2, :]
    s3[24:n, :] = s2[24:n, :] + s2[20:n - 4, :]
    a16 = s3[HALO:n, :] + s3[HALO - 8:n - 8, :]
    grp, wl = _pool_window_lane((tm, 256))
    tsum = jnp.where(grp == 0, s1[HALO:n, :], jnp.where(grp == 1, s2[HALO:n, :],
                     jnp.where(grp == 2, s3[HALO:n, :], a16)))
    pos = i * tm + _iota((tm, 256), 0)
    cnt = jnp.minimum(pos + 1, wl).astype(F32)
    o_pool = _mm((tsum / cnt - u).astype(BF16), pw_ref[...]) * ps_ref[...]
    o_mem = _mem_attend(mq_ref, mk_ref, mv_ref, tm)
    parts = [oa_ref[:, 0:LANES], oa_ref[:, LANES:256], ob_ref[:, 0:LANES], ob_ref[:, LANES:256],
             o_pool[:, 0:LANES], o_pool[:, LANES:256], o_mem[0], o_mem[1]]
    z_of = lambda c: pz_ref[:, 256 + c * LANES:256 + (c + 1) * LANES]
    y_ref[...] = _mix_out(x_ref[...], parts, z_of, wout_ref, fg_ref, final)


def _post_prompt(oa, ob, pz3, pb3, memb, x3, wout, pw_bd, ps, fg, *, tm, final):
    b, t, d = x3.shape
    nt = t // tm
    row = lambda bi, i: (bi, i, 0)
    const = lambda bi, i: (0, 0)
    return pl.pallas_call(
        functools.partial(_post_body, tm=tm, final=final),
        grid=(b, nt),
        in_specs=[pl.BlockSpec((None, tm, 256), row), pl.BlockSpec((None, tm, 256), row),
                  pl.BlockSpec((None, tm, 1280), row),
                  pl.BlockSpec((None, HALO, 256),
                               lambda bi, i: (bi, jnp.maximum(i * (tm // HALO) - 1, 0), 0)),
                  pl.BlockSpec((None, tm, 256), lambda bi, i: (bi, i, C_MQ // 256)),
                  pl.BlockSpec((None, 256, 256), lambda bi, i: (bi, 0, 0)),
                  pl.BlockSpec((None, 256, 256), lambda bi, i: (bi, 0, 1)),
                  pl.BlockSpec((None, tm, d), row),
                  pl.BlockSpec((d, d), const), pl.BlockSpec((256, 256), const),
                  pl.BlockSpec((1, 256), const), pl.BlockSpec((1, d), const)],
        out_specs=pl.BlockSpec((None, tm, d), row),
        out_shape=jax.ShapeDtypeStruct((b, t, d), F32),
        scratch_shapes=[pltpu.VMEM((tm + HALO, 256), F32)] * 4,
        compiler_params=pltpu.CompilerParams(dimension_semantics=("parallel", "arbitrary"),
                                             vmem_limit_bytes=VMEM_LIMIT),
        name="post_prompt",
    )(oa, ob, pz3, pz3, pb3, memb, memb, x3, wout, pw_bd, ps, fg)


ROWS_PER_PAGE = 8


def _cmp_sample_body(pt_ref, flat_hbm, pe_ref, w1_ref, w2_ref, o_ref, xbuf, sem, *, per_step, layer):
    s = pl.program_id(0)
    nsteps = pl.num_programs(0)

    def copies(step, slot):
        out = []
        for i in range(per_step):
            page = pt_ref[step * per_step + i]
            out.append(pltpu.make_async_copy(
                flat_hbm.at[layer, page], xbuf.at[slot, pl.ds(i * ROWS_PER_PAGE, ROWS_PER_PAGE), :],
                sem.at[slot]))
        return out

    @pl.when(s == 0)
    def _():
        for c in copies(0, 0):
            c.start()

    @pl.when(s + 1 < nsteps)
    def _():
        for c in copies(s + 1, (s + 1) % 2):
            c.start()

    slot = s % 2
    for c in copies(s, slot):
        c.wait()
    rows = per_step * ROWS_PER_PAGE
    x = xbuf[slot].reshape(per_step, ROWS_PER_PAGE, xbuf.shape[-1]) + pe_ref[...][None]
    xb = x.reshape(rows, xbuf.shape[-1]).astype(BF16)
    h = _mm(xb, w1_ref[...])
    h = (h * _sigmoid(h)).astype(BF16)
    o_ref[...] = jnp.concatenate([_mm(h[:, 0:256], w2_ref[0]), _mm(h[:, 256:512], w2_ref[1])], axis=1)


def _compress_sample(page_flat, cmpflat, pe8, w1cat, w2, *, ns, layer):
    npages = page_flat.shape[0]
    f = cmpflat.shape[-1]
    rows = npages * ROWS_PER_PAGE
    nsteps = npages // ns
    step_rows = ns * ROWS_PER_PAGE
    grid_spec = pltpu.PrefetchScalarGridSpec(
        num_scalar_prefetch=1, grid=(nsteps,),
        in_specs=[pl.BlockSpec(memory_space=pl.ANY),
                  pl.BlockSpec((ROWS_PER_PAGE, f), lambda s, pt: (0, 0)),
                  pl.BlockSpec((f, 512), lambda s, pt: (0, 0)),
                  pl.BlockSpec((2, 256, HEAD_DIM), lambda s, pt: (0, 0, 0))],
        out_specs=pl.BlockSpec((step_rows, LANES), lambda s, pt: (s, 0)),
        scratch_shapes=[pltpu.VMEM((2, step_rows, f), F32), pltpu.SemaphoreType.DMA((2,))])
    return pl.pallas_call(
        functools.partial(_cmp_sample_body, per_step=ns, layer=layer),
        grid_spec=grid_spec,
        out_shape=jax.ShapeDtypeStruct((rows, LANES), F32),
        compiler_params=pltpu.CompilerParams(dimension_semantics=("arbitrary",),
                                             vmem_limit_bytes=VMEM_LIMIT),
        name="cmp_sample",
    )(page_flat, cmpflat, pe8, w1cat, w2)


HROWS = 16


def _rows_from_lanes(vec_row, pick):
    return jnp.sum(jnp.where(pick, vec_row, 0.0), axis=1, keepdims=True)


def _bf16r(x):
    return x.astype(BF16).astype(F32)


def _att_sample_body(pt_ref, fox_hbm, lf_hbm, nsa_hbm, pb_ref, sf_ref, aux_ref, win_ref, mem_ref,
                     kc_ref, vc_ref, exp_ref, o_ref, wout_ref, fbuf, lbuf, sbuf, sem,
                     *, npg, past, layer):
    b = pl.program_id(0)
    nb = pl.num_programs(0)

    def copies(seq, slot):
        out = []
        for i in range(npg):
            page = pt_ref[seq * npg + i]
            lanes = pl.ds(i * PAGE_SIZE, PAGE_SIZE)
            out.append(pltpu.make_async_copy(
                fox_hbm.at[layer, page], fbuf.at[slot, :, lanes], sem.at[0, slot]))
            out.append(pltpu.make_async_copy(
                lf_hbm.at[layer, page], lbuf.at[slot, pl.ds(i * 8, 8), :], sem.at[1, slot]))
            out.append(pltpu.make_async_copy(
                nsa_hbm.at[layer, page, pl.ds(256, 256), :], sbuf.at[slot, :, lanes], sem.at[2, slot]))
        return out

    @pl.when(b == 0)
    def _():
        for c in copies(0, 0):
            c.start()

    @pl.when(b + 1 < nb)
    def _():
        for c in copies(b + 1, (b + 1) % 2):
            c.start()

    slot = b % 2
    for c in copies(b, slot):
        c.wait()

    row8 = _iota((HROWS, LANES), 0)
    lane8 = _iota((HROWS, LANES), 1)
    aux = aux_ref[...]
    pbr = pb_ref[...].astype(F32)
    sfr = sf_ref[...]

    def head_rows(q256):
        r = _iota((HROWS, 256), 0)
        l = _iota((HROWS, 256), 1)
        return jnp.where(l // HEAD_DIM == r, q256, 0.0).astype(BF16)

    def pick_heads(o8):
        r = _iota((HROWS, 256), 0)
        l = _iota((HROWS, 256), 1)
        return jnp.sum(jnp.where(l // HEAD_DIM == r, o8, 0.0), axis=0, keepdims=True)

    kt = fbuf[slot, 0:256, :].astype(BF16)
    vt = fbuf[slot, 256:512, :].astype(BF16)
    qf = head_rows(pbr[:, C_FQ:C_FK])
    s = _mm(qf, kt)
    lf2 = lbuf[slot]
    r2 = _iota((npg * 8, npg * 8), 0)
    c2 = _iota((npg * 8, npg * 8), 1)
    upper = (_iota((LANES, LANES), 0) > _iota((LANES, LANES), 1)).astype(BF16)
    within = _exact_mm_rhs(lf2, upper)
    tot = jnp.sum(lf2, axis=1, keepdims=True)
    later_pages = ((c2 % 8 == r2 % 8) & (c2 // 8 > r2 // 8)).astype(BF16)
    later = _exact_mm(later_pages, jnp.broadcast_to(tot, (npg * 8, LANES)))
    rr = _iota((npg * 8, LANES), 0)
    ll = _iota((npg * 8, LANES), 1)
    lf_new = _rows_from_lanes(aux, ll == rr % 8)
    bias = within + later + lf_new
    zpad = jnp.zeros((HROWS - 8, LANES), F32)
    s = jnp.concatenate(
        [s[:, i * PAGE_SIZE:(i + 1) * PAGE_SIZE] + jnp.concatenate([bias[i * 8:(i + 1) * 8, :], zpad], axis=0)
         for i in range(npg)], axis=1)
    knew = _bf16r(sfr[:, 0:256])
    vnew = _bf16r(sfr[:, 256:512])
    s_new = jnp.sum(qf.astype(F32) * knew, axis=1, keepdims=True)
    m = jnp.maximum(jnp.max(s, axis=1, keepdims=True), s_new)
    e = jnp.exp(s - m)
    e_new = jnp.exp(s_new - m)
    den = jnp.sum(e, axis=1, keepdims=True) + e_new
    o8 = (_nt(e.astype(BF16), vt) + _bf16r(e_new) * vnew) / den
    o_ref[:, 0:256] = pick_heads(o8)

    nq = pbr[:, C_NQ:C_NKV]
    t0 = jnp.broadcast_to(nq[:, 0:LANES], (HROWS, LANES))
    t1 = jnp.broadcast_to(nq[:, LANES:256], (HROWS, LANES))
    t0r, t1r = pltpu.roll(t0, 64, 1), pltpu.roll(t1, 64, 1)
    lo = lane8 < 64
    hrows = [jnp.where(lo, t0, 0.0), jnp.where(lo, t0r, 0.0),
             jnp.where(lo, 0.0, t1r), jnp.where(lo, 0.0, t1)]
    qn = jnp.zeros((HROWS, LANES), F32)
    for h in range(H_NSA):
        qn = jnp.where(row8 == h, hrows[h], qn)
    qn_b = qn.astype(BF16)
    slope8 = jnp.zeros((HROWS, 1), F32)
    r81 = _iota((HROWS, 1), 0)
    for h in range(H_NSA):
        slope8 = jnp.where(r81 == h, ALIBI_SLOPES[h], slope8)
    nkv_new = sfr[:, 512:1280]
    nblk = past // NSA_BLOCK

    def new_score(k128):
        return jnp.sum(qn_b.astype(F32) * _bf16r(k128), axis=1, keepdims=True)

    sc = _nt(qn_b, kc_ref[...].astype(BF16))
    cend = _iota((HROWS, nblk), 1) * NSA_BLOCK + (NSA_BLOCK - 1)
    sc = sc - slope8 * (past - cend).astype(F32)
    ec = jnp.exp(sc - jnp.max(sc, axis=1, keepdims=True))
    pc = ec / jnp.maximum(jnp.sum(ec, axis=1, keepdims=True), 1e-30)
    o_c = _mm(pc.astype(BF16), vc_ref[...].astype(BF16))
    imp = jnp.concatenate([pc[2 * g:2 * g + 1] + pc[2 * g + 1:2 * g + 2] for g in range(G_NSA)]
                          + [jnp.zeros((HROWS - G_NSA, nblk), F32)], axis=0)
    if nblk < LANES:
        imp = jnp.concatenate([imp, jnp.zeros((HROWS, LANES - nblk), F32)], axis=1)
    forced = (lane8 == 0) | (lane8 == nblk) | (lane8 == nblk - 1)
    score = jnp.where(lane8 > nblk, -1.0, jnp.where(forced, FORCE_SCORE, imp))
    rank = jnp.zeros((HROWS, LANES), F32)
    for mblk in range(nblk + 1):
        r = score[:, mblk:mblk + 1]
        rank = rank + jnp.where((r > score) | ((r == score) & (lane8 > mblk)), 1.0, 0.0)
    sel_g = jnp.where((rank < min(NSA_TOPN, nblk + 1)) & (score >= 0.0), 1.0, 0.0)
    sel_h = jnp.zeros((HROWS, LANES), F32)
    for h in range(H_NSA):
        sel_h = jnp.where(row8 == h, sel_g[h // 2:h // 2 + 1, :], sel_h)
    valid = _mm(sel_h.astype(BF16), exp_ref[...]) > 0.5
    kst = sbuf[slot, 0:LANES, :].astype(BF16)
    vst = sbuf[slot, LANES:256, :].astype(BF16)
    ss = _mm(qn_b, kst) - slope8 * (past - _iota((HROWS, past), 1)).astype(F32)
    ss = jnp.where(valid, ss, NEG_INF)
    ss_new = new_score(nkv_new[:, 256:384])
    ms = jnp.maximum(jnp.max(ss, axis=1, keepdims=True), ss_new)
    es = jnp.where(valid, jnp.exp(ss - ms), 0.0)
    es_new = jnp.exp(ss_new - ms)
    o_s = (_nt(es.astype(BF16), vst) + _bf16r(es_new) * _bf16r(nkv_new[:, 384:512])) \
        / (jnp.sum(es, axis=1, keepdims=True) + es_new)
    wl_ = win_ref.shape[1]
    win = win_ref[...]
    sw = _mm(qn_b, win[0:LANES].astype(BF16)) - slope8 * (wl_ - _iota((HROWS, wl_), 1)).astype(F32)
    sw_new = new_score(nkv_new[:, 512:640])
    mw = jnp.maximum(jnp.max(sw, axis=1, keepdims=True), sw_new)
    ew = jnp.exp(sw - mw)
    ew_new = jnp.exp(sw_new - mw)
    o_w = (_nt(ew.astype(BF16), win[LANES:256].astype(BF16))
           + _bf16r(ew_new) * _bf16r(nkv_new[:, 640:768])) / (jnp.sum(ew, axis=1, keepdims=True) + ew_new)
    rw = _iota((256, 256), 0)
    lw_ = _iota((256, 256), 1)
    new_col = jnp.sum(jnp.where(rw == lw_, nkv_new[:, 512:768], 0.0), axis=1, keepdims=True)
    wout_ref[...] = jnp.where(_iota((256, wl_), 1) == wl_ - 1, new_col, pltpu.roll(win, wl_ - 1, 1))
    gates = [_rows_from_lanes(aux, lane8 == LANE_GATE0 + 3 * row8 + c) for c in range(3)]
    o_n = gates[0] * o_c + gates[1] * o_s + gates[2] * o_w
    o_nr = pltpu.roll(o_n, 64, 1)
    lo1 = _iota((1, LANES), 1) < 64
    o_ref[:, 256:384] = jnp.where(lo1, o_n[0:1], o_nr[1:2])
    o_ref[:, 384:512] = jnp.where(lo1, o_nr[2:3], o_n[3:4])

    qm = head_rows(pbr[:, C_MQ:C_PU])
    sm = _mm(qm, mem_ref[0:256, :].astype(BF16))
    em = jnp.exp(sm - jnp.max(sm, axis=1, keepdims=True))
    pm = em / jnp.sum(em, axis=1, keepdims=True)
    o_ref[:, 512:768] = pick_heads(_nt(pm.astype(BF16), mem_ref[256:512, :].astype(BF16)))


def _exact_mm_rhs(x, m01):
    p1, p2, p3 = _split3(x)
    return _mm(p1, m01) + _mm(p2, m01) + _mm(p3, m01)


def _attend_sample(page_flat, fox_t, lf_t, nsa_t, pb3, sf3, aux3, win_t, mem_t, kcmp, vcmp, expand,
                   *, npg, layer):
    db = pb3.shape[0]
    past = npg * PAGE_SIZE
    per_seq = lambda blk: pl.BlockSpec((None,) + blk, lambda b, pt: (b, 0, 0))
    per_seq_layer = lambda blk: pl.BlockSpec((None, None) + blk, lambda b, pt: (layer, b, 0, 0))
    grid_spec = pltpu.PrefetchScalarGridSpec(
        num_scalar_prefetch=1, grid=(db,),
        in_specs=[pl.BlockSpec(memory_space=pl.ANY), pl.BlockSpec(memory_space=pl.ANY),
                  pl.BlockSpec(memory_space=pl.ANY),
                  per_seq((1, N_PB)), per_seq((1, 1280)), per_seq((1, LANES)),
                  per_seq_layer(win_t.shape[2:]), per_seq_layer(mem_t.shape[2:]),
                  per_seq(kcmp.shape[1:]), per_seq(vcmp.shape[1:]),
                  pl.BlockSpec(expand.shape, lambda b, pt: (0, 0))],
        out_specs=[per_seq((1, 768)), per_seq(win_t.shape[2:])],
        scratch_shapes=[pltpu.VMEM((2, 512, past), F32), pltpu.VMEM((2, npg * 8, LANES), F32),
                        pltpu.VMEM((2, 256, past), F32), pltpu.SemaphoreType.DMA((3, 2))])
    return pl.pallas_call(
        functools.partial(_att_sample_body, npg=npg, past=past, layer=layer),
        grid_spec=grid_spec,
        out_shape=[jax.ShapeDtypeStruct((db, 1, 768), F32),
                   jax.ShapeDtypeStruct((db,) + win_t.shape[2:], F32)],
        compiler_params=pltpu.CompilerParams(dimension_semantics=("arbitrary",),
                                             vmem_limit_bytes=VMEM_LIMIT),
        name="att_sample",
    )(page_flat, fox_t, lf_t, nsa_t, pb3, sf3, aux3, win_t, mem_t, kcmp, vcmp, expand)


def _post_sample_body(oatt_ref, pz_ref, sp_ref, x_ref, wout_ref, pw_ref, ps_ref, fg_ref, y_ref,
                      spo_ref, *, rows, past, final):
    u = pz_ref[:, 0:256]
    grp, wl = _pool_window_lane((rows, 256))
    tsum = u
    for j in range(1, POOL_BUF + 1):
        tsum = tsum + jnp.where(wl > j, sp_ref[POOL_BUF - j], 0.0)
    for j in range(POOL_BUF - 1):
        spo_ref[j] = sp_ref[j + 1]
    spo_ref[POOL_BUF - 1] = u
    cnt = jnp.minimum(past + 1, wl).astype(F32)
    o_pool = _mm((tsum / cnt - u).astype(BF16), pw_ref[...]) * ps_ref[...]
    parts = [oatt_ref[:, c * LANES:(c + 1) * LANES] for c in range(4)]
    parts += [o_pool[:, 0:LANES], o_pool[:, LANES:256]]
    parts += [oatt_ref[:, 512:640], oatt_ref[:, 640:768]]
    z_of = lambda c: pz_ref[:, 256 + c * LANES:256 + (c + 1) * LANES]
    y_ref[...] = _mix_out(x_ref[...], parts, z_of, wout_ref, fg_ref, final)


def _post_sample(oatt, pz, pool_t, x2d, wout, pw_bd, ps, fg, *, past, final, layer):
    rows, d = x2d.shape
    full = lambda a: pl.BlockSpec(a.shape, lambda i: (0,) * a.ndim)
    return pl.pallas_call(
        functools.partial(_post_sample_body, rows=rows, past=past, final=final),
        grid=(1,),
        in_specs=[full(oatt), full(pz),
                  pl.BlockSpec((None, POOL_BUF, rows, 256), lambda i: (layer, 0, 0, 0)),
                  full(x2d), full(wout), full(pw_bd), full(ps), full(fg)],
        out_specs=[pl.BlockSpec((rows, d), lambda i: (0, 0)),
                   pl.BlockSpec((POOL_BUF, rows, 256), lambda i: (0, 0, 0))],
        out_shape=[jax.ShapeDtypeStruct((rows, d), F32),
                   jax.ShapeDtypeStruct((POOL_BUF, rows, 256), F32)],
        compiler_params=pltpu.CompilerParams(vmem_limit_bytes=VMEM_LIMIT),
        name="post_sample",
    )(oatt, pz, pool_t, x2d, wout, pw_bd, ps, fg)


def _prep_layer_weights(norm_g, w_in, b_fgt, nsa_pe, nsa_w1, nsa_w2, pool_w, pool_scale,
                        w_mem_kv, w_out):
    d = w_in.shape[0]
    o = 0
    segs = {}
    for name, width in (("fqkv", 768), ("ff", H_FOX), ("nq", 256), ("nkv", 768),
                        ("ng", 3 * H_NSA), ("pu", 256), ("mq", 256), ("z", d)):
        segs[name] = w_in[:, o:o + width]
        o += width
    pad = jnp.zeros((d, LANES - H_FOX - 3 * H_NSA), w_in.dtype)
    wp = jnp.concatenate([segs["fqkv"], segs["nq"], segs["nkv"], segs["mq"], segs["pu"],
                          segs["z"], segs["ff"], segs["ng"], pad], axis=1).astype(BF16)
    bvec = jnp.zeros((1, LANES), F32).at[0, 0:H_FOX].set(b_fgt.astype(F32))
    gw = D_GROUP // len(POOL_WINDOWS)
    pw_bd = jnp.zeros((D_GROUP, D_GROUP), F32)
    for g in range(len(POOL_WINDOWS)):
        pw_bd = pw_bd.at[g * gw:(g + 1) * gw, g * gw:(g + 1) * gw].set(pool_w[g])
    return dict(g=norm_g.reshape(1, d), wp=wp, bvec=bvec,
                pe_flat=nsa_pe.reshape(2, 1, NSA_BLOCK * HEAD_DIM),
                w1=nsa_w1.astype(BF16), w2=nsa_w2.astype(BF16),
                w1cat=jnp.concatenate([nsa_w1[0], nsa_w1[1]], axis=1).astype(BF16),
                pw_bd=pw_bd.astype(BF16), ps=pool_scale.reshape(1, D_GROUP).astype(F32),
                wmkv=w_mem_kv.astype(BF16), wout=w_out.astype(BF16))


def _layer_prompt(x, mem, lw, fg, final, tile):
    b, t, d = x.shape
    n = b * t
    nb = t // NSA_BLOCK
    pb, pz, aux, sfc, cq, ct, lft, fkvt, nkvt, nwt = _project(
        x.reshape(n, d), lw["g"], lw["wp"], lw["bvec"], tm=tile, tiles_per_seq=t // tile, prompt=True)
    pb3 = pb.reshape(b, t, N_PB)
    o_a = _fox_prompt(pb3, cq.reshape(b, t, LANES), ct, tq=tile)
    kvc = sfc.reshape(b, nb, NSA_BLOCK, 2, G_NSA, HEAD_DIM)
    flat = jnp.transpose(kvc, (3, 0, 1, 4, 2, 5)).reshape(2, b * nb * G_NSA, NSA_BLOCK * HEAD_DIM)
    cmp = _compress_prompt(flat, lw["pe_flat"], lw["w1"], lw["w2"]).reshape(2, b, nb, LANES)
    cmp = jnp.pad(cmp, ((0, 0), (0, 0), (0, NBLK_PAD - nb), (0, 0))).astype(BF16)
    o_b = _nsa_prompt(pb3, cmp[0], cmp[1], aux.reshape(b, t, LANES), tq=tile, nb=nb)
    n_mem = mem.shape[1]
    mem_t, memb = _mem_project(mem, lw["wmkv"])
    pz3 = pz.reshape(b, t, 1280)
    y = _post_prompt(o_a, o_b, pz3, pb3, memb, x, lw["wout"], lw["pw_bd"], lw["ps"], fg,
                     tm=tile, final=final)
    w_keep = min(NSA_WINDOW, t)
    back = (0, 4, 1, 2, 3)
    state = (jnp.transpose(fkvt.reshape(b, 2, H_FOX, HEAD_DIM, t), back),
             jnp.transpose(lft[:, 0:H_FOX, :], (0, 2, 1)),
             jnp.transpose(nkvt.reshape(b, 4, G_NSA, HEAD_DIM, t), back),
             jnp.transpose(nwt[:, :, t - w_keep:].reshape(b, 2, G_NSA, HEAD_DIM, w_keep), back),
             pz3[:, t - POOL_BUF:, 0:256],
             jnp.transpose(mem_t.reshape(b, 2, H_MEM, HEAD_DIM, n_mem), back))
    return y, state


def _native_views(cache_fox_kv, cache_fox_lf, cache_nsa_kv, state_nsa_win, state_pool, cache_mem_kv):
    dp, n_phys = cache_fox_kv.shape[:2]
    db = state_nsa_win.shape[1]
    to_t = (0, 1, 3, 4, 5, 2)
    fox_t = jnp.transpose(cache_fox_kv, to_t).reshape(dp, n_phys, 512, PAGE_SIZE)
    lf_t = jnp.pad(jnp.transpose(cache_fox_lf.astype(F32), (0, 1, 3, 2)),
                   ((0, 0), (0, 0), (0, 8 - H_FOX), (0, 0)))
    nsa_t = jnp.transpose(cache_nsa_kv, to_t).reshape(dp, n_phys, 512, PAGE_SIZE)
    win_t = jnp.transpose(state_nsa_win, to_t).reshape(dp, db, 256, state_nsa_win.shape[2])
    mem_t = jnp.transpose(cache_mem_kv, to_t).reshape(dp, db, 512, cache_mem_kv.shape[2])
    pool_t = jnp.transpose(state_pool, (0, 2, 1, 3))
    cmpflat = jnp.transpose(
        nsa_t[:, :, 0:256].reshape(dp, n_phys, 2, G_NSA, HEAD_DIM, PAGE_SIZE // NSA_BLOCK, NSA_BLOCK),
        (0, 1, 2, 3, 5, 6, 4)).reshape(dp, n_phys, ROWS_PER_PAGE, NSA_BLOCK * HEAD_DIM)
    return dict(fox_t=fox_t, lf_t=lf_t, nsa_t=nsa_t, win_t=win_t, mem_t=mem_t, pool_t=pool_t,
                cmpflat=cmpflat)


def _layer_sample(x, nv, page_table, lw, fg, final, layer):
    db, ds, d = x.shape
    npg = page_table.shape[1]
    past = npg * PAGE_SIZE
    nblk = past // NSA_BLOCK
    pb, pz, aux, sf, sft, auxt = _project(x.reshape(db, d), lw["g"], lw["wp"], lw["bvec"],
                                          tm=db, tiles_per_seq=1, prompt=False)
    page_flat = page_table.reshape(db * npg).astype(I32)
    pe8 = jnp.repeat(lw["pe_flat"].reshape(2, NSA_BLOCK * HEAD_DIM), ROWS_PER_PAGE // 2, axis=0)
    cmp = _compress_sample(page_flat, nv["cmpflat"], pe8, lw["w1cat"], lw["w2"], ns=64, layer=layer)
    cmp = cmp.reshape(db, npg, 2, G_NSA, PAGE_SIZE // NSA_BLOCK, LANES)
    kcmp = jnp.transpose(cmp[:, :, 0, :, :, 0:HEAD_DIM], (0, 1, 3, 2, 4)).reshape(db, nblk, LANES)
    vcmp = jnp.transpose(cmp[:, :, 1, :, :, HEAD_DIM:LANES], (0, 1, 3, 2, 4)).reshape(db, nblk, LANES)
    expand = (np.arange(LANES)[:, None] == (np.arange(past)[None, :] // NSA_BLOCK)).astype(np.float32)
    oatt, win_new = _attend_sample(
        page_flat, nv["fox_t"], nv["lf_t"], nv["nsa_t"], pb.reshape(db, 1, N_PB),
        sf.reshape(db, 1, 1280), aux.reshape(db, 1, LANES), nv["win_t"], nv["mem_t"], kcmp, vcmp,
        jnp.asarray(expand, BF16), npg=npg, layer=layer)
    y, pool_new = _post_sample(oatt.reshape(db, 768), pz, nv["pool_t"], x.reshape(db, d), lw["wout"],
                               lw["pw_bd"], lw["ps"], fg, past=past, final=final, layer=layer)
    w_buf = win_new.shape[-1]
    seq_first = (3, 0, 1, 2)
    state = (jnp.transpose(sft[0:512].reshape(2, H_FOX, HEAD_DIM, db), seq_first)[:, None],
             jnp.transpose(auxt[0:H_FOX])[:, None, :],
             jnp.transpose(sft[512:1024].reshape(4, G_NSA, HEAD_DIM, db), seq_first)[:, None],
             jnp.transpose(win_new.reshape(db, 2, G_NSA, HEAD_DIM, w_buf), (0, 4, 1, 2, 3)),
             jnp.transpose(pool_new, (1, 0, 2)))
    return y.reshape(db, ds, d), state


def kernel(x_prompt, x_sample, cache_fox_kv, cache_fox_lf, cache_nsa_kv, state_nsa_win, state_pool,
           cache_mem_kv, page_table, mem_prompt, norm_g, w_in, b_fgt, nsa_pe, nsa_w1, nsa_w2, pool_w,
           pool_scale, w_mem_kv, w_out, final_g):
    depth = norm_g.shape[0]
    t = x_prompt.shape[1]
    assert x_sample.shape[1] == 1 and t % NSA_WINDOW == 0 and t // NSA_BLOCK <= NBLK_PAD
    hp, hs = x_prompt, x_sample
    fg = final_g.reshape(1, -1).astype(F32)
    nv = _native_views(cache_fox_kv, cache_fox_lf, cache_nsa_kv, state_nsa_win, state_pool,
                       cache_mem_kv)
    sp = [[] for _ in range(6)]
    ss = [[] for _ in range(5)]
    for l in range(depth):
        lw = _prep_layer_weights(norm_g[l], w_in[l], b_fgt[l], nsa_pe[l], nsa_w1[l], nsa_w2[l],
                                 pool_w[l], pool_scale[l], w_mem_kv[l], w_out[l])
        final = l == depth - 1
        hp, st_p = _layer_prompt(hp, mem_prompt, lw, fg, final, NSA_WINDOW)
        hs, st_s = _layer_sample(hs, nv, page_table, lw, fg, final, l)
        for acc, s in zip(sp, st_p):
            acc.append(s)
        for acc, s in zip(ss, st_s):
            acc.append(s)
    return (hp, hs) + tuple(jnp.stack(a) for a in sp) + tuple(jnp.stack(a) for a in ss)
```

```python
import functools

import numpy as np
import jax
import jax.numpy as jnp
from jax import lax
from jax.experimental import pallas as pl
from jax.experimental.pallas import tpu as pltpu

F32 = jnp.float32
BF16 = jnp.bfloat16
I32 = jnp.int32

HEAD_DIM = 64
H_FOX = 4
H_NSA = 4
G_NSA = 2
H_MEM = 4
D_GROUP = 256
POOL_WINDOWS = (2, 4, 8, 16)
POOL_BUF = 15
NSA_BLOCK = 64
NSA_TOPN = 16
NSA_WINDOW = 512
PAGE_SIZE = 128
RMS_EPS = 1e-6
NEG_INF = -1e30
FORCE_SCORE = 1e4
SCALE = HEAD_DIM ** -0.5
ALIBI_SLOPES = tuple(2.0 ** (-8.0 * (h + 1) / H_NSA) for h in range(H_NSA))

LANES = 128
NBLK_PAD = 128
VMEM_LIMIT = 56 * 1024 * 1024

C_FQ, C_FK, C_FV, C_NQ, C_NKV, C_MQ, C_PU, C_Z, C_SM, C_END = (
    0, 256, 512, 768, 1024, 1792, 2048, 2304, 3328, 3456)
N_PB = C_PU
LANE_GATE0 = 4


def _nt(a, b):
    return lax.dot_general(a, b, (((1,), (1,)), ((), ())), preferred_element_type=F32)


def _mm(a, b):
    return jnp.dot(a, b, preferred_element_type=F32)


def _split3(x):
    p1 = x.astype(BF16)
    r1 = x - p1.astype(F32)
    p2 = r1.astype(BF16)
    p3 = (r1 - p2.astype(F32)).astype(BF16)
    return p1, p2, p3


def _exact_mm(m01, x):
    p1, p2, p3 = _split3(x)
    return _mm(m01, p1) + _mm(m01, p2) + _mm(m01, p3)


def _sigmoid(x):
    return 1.0 / (1.0 + jnp.exp(-x))


def _log_sigmoid(x):
    return jnp.minimum(x, 0.0) - jnp.log1p(jnp.exp(-jnp.abs(x)))


def _iota(shape, dim):
    return lax.broadcasted_iota(I32, shape, dim)


def _proj_body(x_ref, g_ref, w_ref, b_ref, pb_ref, pz_ref, aux_ref, *rest, tm, tiles_per_seq, prompt):
    x = x_ref[...]
    ms = jnp.mean(x * x, axis=-1, keepdims=True)
    xn = (x * lax.rsqrt(ms + RMS_EPS) * g_ref[...]).astype(BF16)

    def seg(a, b):
        return _mm(xn, w_ref[:, a:b])

    pb_ref[:, C_FQ:C_FK] = (seg(C_FQ, C_FK) * SCALE).astype(BF16)
    fkv = seg(C_FK, C_NQ)
    pb_ref[:, C_FK:C_NQ] = fkv.astype(BF16)
    pb_ref[:, C_NQ:C_NKV] = (seg(C_NQ, C_NKV) * SCALE).astype(BF16)
    nkv = seg(C_NKV, C_MQ)
    pb_ref[:, C_NKV:C_MQ] = nkv.astype(BF16)
    pb_ref[:, C_MQ:C_PU] = (seg(C_MQ, C_PU) * SCALE).astype(BF16)
    pz_ref[...] = seg(C_PU, C_SM)
    small = seg(C_SM, C_END)

    lane = _iota((tm, LANES), 1)
    lf = _log_sigmoid(small + b_ref[...])
    aux = jnp.where(lane < H_FOX, lf, jnp.where(lane < LANE_GATE0 + 3 * H_NSA, _sigmoid(small), 0.0))
    aux_ref[...] = aux
    if prompt:
        sfc_ref, ct_ref, lft_ref, fkvt_ref, nkvt_ref, nwt_ref, carry_ref = rest
        sfc_ref[...] = nkv[:, 0:256]
        fkvt_ref[...] = fkv.T
        nkvt_ref[...] = nkv[:, 0:512].T
        nwt_ref[...] = nkv[:, 512:768].T
        lfm = jnp.where(lane < H_FOX, lf, 0.0)
        lft_ref[...] = lfm.T[0:8, :]
        tri = (_iota((LANES, LANES), 0) >= _iota((LANES, LANES), 1)).astype(BF16)
        first = (pl.program_id(0) % tiles_per_seq) == 0
        carry = jnp.where(first, 0.0, carry_ref[0:1, :])
        for r in range(tm // LANES):
            blk = lfm[r * LANES:(r + 1) * LANES]
            cblk = _exact_mm(tri, blk) + carry
            carry = cblk[LANES - 1:LANES, :]
            ct_ref[:, r * LANES:(r + 1) * LANES] = cblk.T[0:8, :]
        carry_ref[0:1, :] = carry
    else:
        sf_ref, sft_ref, auxt_ref = rest
        sf_ref[:, 0:512] = fkv
        sf_ref[:, 512:1280] = nkv
        sft_ref[0:512, :] = fkv.T
        sft_ref[512:1280, :] = nkv.T
        auxt_ref[...] = aux.T


def _project(x2d, g, wp, bvec, *, tm, tiles_per_seq, prompt):
    n, d = x2d.shape
    grid = (n // tm,)
    row = lambda i: (i, 0)
    const = lambda i: (0, 0)
    out_shape = [jax.ShapeDtypeStruct((n, N_PB), BF16), jax.ShapeDtypeStruct((n, 1280), F32),
                 jax.ShapeDtypeStruct((n, LANES), F32)]
    out_specs = [pl.BlockSpec((tm, N_PB), row), pl.BlockSpec((tm, 1280), row),
                 pl.BlockSpec((tm, LANES), row)]
    scratch = []
    if prompt:
        nseq = n // (tm * tiles_per_seq)
        t = tm * tiles_per_seq
        seq_t = lambda rows: pl.BlockSpec(
            (None, rows, tm), lambda i: (i // tiles_per_seq, 0, i % tiles_per_seq))
        out_shape += [jax.ShapeDtypeStruct((n, 256), F32),
                      jax.ShapeDtypeStruct((nseq, 8, t), F32), jax.ShapeDtypeStruct((nseq, 8, t), F32),
                      jax.ShapeDtypeStruct((nseq, 512, t), F32), jax.ShapeDtypeStruct((nseq, 512, t), F32),
                      jax.ShapeDtypeStruct((nseq, 256, t), F32)]
        out_specs += [pl.BlockSpec((tm, 256), row),
                      seq_t(8), seq_t(8), seq_t(512), seq_t(512), seq_t(256)]
        scratch = [pltpu.VMEM((8, LANES), F32)]
    else:
        assert grid == (1,)
        out_shape += [jax.ShapeDtypeStruct((n, 1280), F32), jax.ShapeDtypeStruct((1280, n), F32),
                      jax.ShapeDtypeStruct((LANES, n), F32)]
        out_specs += [pl.BlockSpec((tm, 1280), row), pl.BlockSpec((1280, tm), const),
                      pl.BlockSpec((LANES, tm), const)]
    return pl.pallas_call(
        functools.partial(_proj_body, tm=tm, tiles_per_seq=tiles_per_seq, prompt=prompt),
        grid=grid,
        in_specs=[pl.BlockSpec((tm, d), row), pl.BlockSpec((1, d), const),
                  pl.BlockSpec((d, C_END), const), pl.BlockSpec((1, LANES), const)],
        out_specs=out_specs, out_shape=out_shape, scratch_shapes=scratch,
        compiler_params=pltpu.CompilerParams(dimension_semantics=("arbitrary",),
                                             vmem_limit_bytes=VMEM_LIMIT),
        name="proj",
    )(x2d, g, wp, bvec)


def _memkv_body(m_ref, w_ref, ot_ref, ob_ref):
    r = _mm(m_ref[...].astype(BF16), w_ref[...])
    ot_ref[...] = r.T
    ob_ref[...] = r.astype(BF16)


def _mem_project(mem3, w):
    b, n_mem, d = mem3.shape
    e = w.shape[1]
    return pl.pallas_call(
        _memkv_body, grid=(b,),
        in_specs=[pl.BlockSpec((None, n_mem, d), lambda i: (i, 0, 0)),
                  pl.BlockSpec((d, e), lambda i: (0, 0))],
        out_specs=[pl.BlockSpec((None, e, n_mem), lambda i: (i, 0, 0)),
                   pl.BlockSpec((None, n_mem, e), lambda i: (i, 0, 0))],
        out_shape=[jax.ShapeDtypeStruct((b, e, n_mem), F32), jax.ShapeDtypeStruct((b, n_mem, e), BF16)],
        name="memkv",
    )(mem3, w)


def _fox_body(q_ref, k_ref, v_ref, ct_ref, o_ref, m_ref, l_ref, acc_ref, *, tq, tk):
    qi = pl.program_id(1)
    ki = pl.program_id(2)

    @pl.when(ki == 0)
    def _():
        m_ref[...] = jnp.full(m_ref.shape, NEG_INF, F32)
        l_ref[...] = jnp.zeros(l_ref.shape, F32)
        acc_ref[...] = jnp.zeros(acc_ref.shape, F32)

    lane = _iota((tq, LANES), 1)

    def step(masked):
        if masked:
            causal = _iota((tq, tk), 0) >= _iota((tq, tk), 1)
        scores = []
        for h in range(H_FOX):
            pr, hh = divmod(h, 2)
            cs = slice(pr * LANES, (pr + 1) * LANES)
            q2 = q_ref[:, cs]
            qm = jnp.where((lane >= 64 * hh) & (lane < 64 * hh + 64), q2, jnp.zeros_like(q2))
            scores.append(_nt(qm, k_ref[:, cs]))
        probs = []
        for h in range(H_FOX):
            s = scores[h] - ct_ref[h:h + 1, :]
            if masked:
                s = jnp.where(causal, s, NEG_INF)
            m_prev = m_ref[h]
            m_new = jnp.maximum(m_prev, jnp.max(s, axis=1, keepdims=True))
            alpha = jnp.exp(m_prev - m_new)
            p = jnp.exp(s - jnp.concatenate([m_new] * (tk // LANES), axis=1))
            l_ref[h] = alpha * l_ref[h] + jnp.sum(p, axis=1, keepdims=True)
            m_ref[h] = m_new
            probs.append((p.astype(BF16), alpha))
        for h in range(H_FOX):
            cs = slice((h // 2) * LANES, (h // 2 + 1) * LANES)
            p, alpha = probs[h]
            acc_ref[h] = alpha * acc_ref[h] + _mm(p, v_ref[:, cs])

    @pl.when(ki < qi)
    def _():
        step(False)

    @pl.when(ki == qi)
    def _():
        step(True)
        for pr in range(H_FOX // 2):
            o0 = acc_ref[2 * pr] / l_ref[2 * pr]
            o1 = acc_ref[2 * pr + 1] / l_ref[2 * pr + 1]
            o_ref[:, pr * LANES:(pr + 1) * LANES] = jnp.where(lane < 64, o0, o1)


def _fox_prompt(pb3, ct, *, tq):
    b, t, _ = pb3.shape
    nq = t // tq
    return pl.pallas_call(
        functools.partial(_fox_body, tq=tq, tk=tq),
        grid=(b, nq, nq),
        in_specs=[pl.BlockSpec((None, tq, 256), lambda bi, qi, ki: (bi, qi, C_FQ // 256)),
                  pl.BlockSpec((None, tq, 256), lambda bi, qi, ki: (bi, jnp.minimum(ki, qi), C_FK // 256)),
                  pl.BlockSpec((None, tq, 256), lambda bi, qi, ki: (bi, jnp.minimum(ki, qi), C_FV // 256)),
                  pl.BlockSpec((None, 8, tq), lambda bi, qi, ki: (bi, 0, jnp.minimum(ki, qi)))],
        out_specs=pl.BlockSpec((None, tq, 256), lambda bi, qi, ki: (bi, qi, 0)),
        out_shape=jax.ShapeDtypeStruct((b, t, 256), F32),
        scratch_shapes=[pltpu.VMEM((H_FOX, tq, LANES), F32), pltpu.VMEM((H_FOX, tq, LANES), F32),
                        pltpu.VMEM((H_FOX, tq, LANES), F32)],
        compiler_params=pltpu.CompilerParams(
            dimension_semantics=("parallel", "parallel", "arbitrary"),
            vmem_limit_bytes=VMEM_LIMIT),
        name="fox_prompt",
    )(pb3, pb3, pb3, ct)


def _cmp_body(x_ref, pe_ref, w1_ref, w2_ref, o_ref):
    xb = (x_ref[...] + pe_ref[...]).astype(BF16)
    h = _mm(xb, w1_ref[...])
    h = h * _sigmoid(h)
    o_ref[...] = _mm(h.astype(BF16), w2_ref[...])


def _compress_prompt(flat, pe_flat, w1, w2):
    _, m, f = flat.shape
    tm = min(m, 512)
    return pl.pallas_call(
        _cmp_body, grid=(2, m // tm),
        in_specs=[pl.BlockSpec((None, tm, f), lambda j, i: (j, i, 0)),
                  pl.BlockSpec((None, 1, f), lambda j, i: (j, 0, 0)),
                  pl.BlockSpec((None, f, 256), lambda j, i: (j, 0, 0)),
                  pl.BlockSpec((None, 256, HEAD_DIM), lambda j, i: (j, 0, 0))],
        out_specs=pl.BlockSpec((None, tm, HEAD_DIM), lambda j, i: (j, i, 0)),
        out_shape=jax.ShapeDtypeStruct((2, m, HEAD_DIM), F32),
        compiler_params=pltpu.CompilerParams(vmem_limit_bytes=VMEM_LIMIT),
        name="cmp_prompt",
    )(flat, pe_flat, w1, w2)


def _masked_softmax_cols(s, valid):
    s = jnp.where(valid, s, NEG_INF)
    e = jnp.where(valid, jnp.exp(s - jnp.max(s, axis=0, keepdims=True)), 0.0)
    return e / jnp.maximum(jnp.sum(e, axis=0, keepdims=True), 1e-30)


def _select_blocks(score, nb, topn):
    nbp, n = score.shape
    blk = _iota((nbp, n), 0)
    rank = jnp.zeros((nbp, n), F32)
    for m in range(nb):
        r = score[m:m + 1, :]
        rank = rank + jnp.where((r > score) | ((r == score) & (blk > m)), 1.0, 0.0)
    return jnp.where((rank < topn) & (score >= 0.0), 1.0, 0.0)


def _nsa_body(q_ref, ks_ref, vs_ref, kwc_ref, kwp_ref, vwc_ref, vwp_ref, kc_ref, vc_ref,
              aux_ref, o_ref, sel_ref, m_ref, l_ref, acc_ref, oc_ref, ow_ref, *, tq, nb, topn):
    qi = pl.program_id(1)
    ki = pl.program_id(2)
    q0 = qi * tq
    tk = tq
    lane = _iota((tq, LANES), 1)

    def qstack(g):
        q2 = q_ref[:, g * LANES:(g + 1) * LANES].astype(F32)
        q2r = pltpu.roll(q2, 64, 1)
        ing = (lane >= 64 * g) & (lane < 64 * g + 64)
        qa, qb = (q2, q2r) if g == 0 else (q2r, q2)
        return jnp.concatenate([jnp.where(ing, qa, 0.0), jnp.where(ing, qb, 0.0)],
                               axis=0).astype(BF16)

    def biased(raw, g, kd, valid):
        sa = jnp.where(valid, raw[:tq] + ALIBI_SLOPES[2 * g] * kd, NEG_INF)
        sb = jnp.where(valid, raw[tq:] + ALIBI_SLOPES[2 * g + 1] * kd, NEG_INF)
        return jnp.concatenate([sa, sb], axis=0)

    def lanes(x, n):
        return jnp.concatenate([x] * (n // LANES), axis=1)

    @pl.when(ki == 0)
    def _():
        m_ref[...] = jnp.full(m_ref.shape, NEG_INF, F32)
        l_ref[...] = jnp.zeros(l_ref.shape, F32)
        acc_ref[...] = jnp.zeros(acc_ref.shape, F32)
        dqk = _iota((tq, tk), 0) - _iota((tq, tk), 1)
        kd_cur = _iota((1, tk), 1).astype(F32)
        for g in range(G_NSA):
            qs = qstack(g)
            st = _nt(kc_ref[...], qs)
            blk = _iota((NBLK_PAD, 2 * tq), 0)
            col = _iota((NBLK_PAD, 2 * tq), 1)
            qpos = q0 + jnp.where(col >= tq, col - tq, col)
            slope = jnp.where(col < tq, ALIBI_SLOPES[2 * g], ALIBI_SLOPES[2 * g + 1])
            cend = blk * NSA_BLOCK + (NSA_BLOCK - 1)
            st = st + slope * (cend - q0).astype(F32)
            pt = _masked_softmax_cols(st, (cend <= qpos) & (blk < nb))
            oc_ref[g] = _mm(pt.T.astype(BF16), vc_ref[...])
            imp = pt[:, :tq] + pt[:, tq:]
            nbp = -(-nb // 8) * 8
            blk2 = _iota((nbp, tq), 0)
            cur = (q0 + _iota((nbp, tq), 1)) // NSA_BLOCK
            forced = (blk2 == 0) | (blk2 == cur) | (blk2 == cur - 1)
            score = jnp.where(blk2 > cur, -1.0, jnp.where(forced, FORCE_SCORE, imp[:nbp]))
            selt = _select_blocks(score, nb, topn)
            if nbp < NBLK_PAD:
                selt = jnp.concatenate([selt, jnp.zeros((NBLK_PAD - nbp, tq), F32)], axis=0)
            sel_ref[g] = selt.T.astype(BF16)
            s1 = biased(_nt(qs, kwc_ref[...]), g, kd_cur, dqk >= 0)
            s2 = biased(_nt(qs, kwp_ref[...]), g, kd_cur - tq, (dqk + tq <= NSA_WINDOW) & (qi > 0))
            mw = jnp.maximum(jnp.max(s1, axis=1, keepdims=True), jnp.max(s2, axis=1, keepdims=True))
            mw = jnp.broadcast_to(mw, (2 * tq, LANES))
            e1 = jnp.exp(s1 - lanes(mw, tk))
            e2 = jnp.exp(s2 - lanes(mw, tk))
            lw = jnp.sum(e1, axis=1, keepdims=True) + jnp.sum(e2, axis=1, keepdims=True)
            ow_ref[g] = (_mm(e1.astype(BF16), vwc_ref[...]) + _mm(e2.astype(BF16), vwp_ref[...])) \
                / jnp.maximum(lw, 1e-30)

    def sel_step(diag):
        expand = (_iota((NBLK_PAD, tk), 0) ==
                  ki * (tk // NSA_BLOCK) + _iota((NBLK_PAD, tk), 1) // NSA_BLOCK).astype(BF16)
        kd = (_iota((1, tk), 1) + (ki * tk - q0)).astype(F32)
        raws = [(_nt(qstack(g), ks_ref[...]), _mm(sel_ref[g], expand)) for g in range(G_NSA)]
        probs = []
        for g in range(G_NSA):
            raw, picked = raws[g]
            valid = picked > 0.5
            if diag:
                valid = valid & (_iota((tq, tk), 0) >= _iota((tq, tk), 1))
            s = biased(raw, g, kd, valid)
            m_prev = m_ref[g]
            m_new = jnp.maximum(m_prev, jnp.max(s, axis=1, keepdims=True))
            alpha = jnp.exp(m_prev - m_new)
            p = jnp.exp(s - lanes(m_new, tk))
            l_ref[g] = alpha * l_ref[g] + jnp.sum(p, axis=1, keepdims=True)
            m_ref[g] = m_new
            probs.append((p.astype(BF16), alpha))
        for g in range(G_NSA):
            p, alpha = probs[g]
            acc_ref[g] = alpha * acc_ref[g] + _mm(p, vs_ref[...])

    @pl.when(ki < qi)
    def _():
        sel_step(False)

    @pl.when(ki == qi)
    def _():
        sel_step(True)
        for g in range(G_NSA):
            o_s = acc_ref[g] / jnp.maximum(l_ref[g], 1e-30)
            o_c = oc_ref[g]
            o_w = ow_ref[g]
            outs = []
            for hh in range(2):
                h = 2 * g + hh
                rs = slice(hh * tq, (hh + 1) * tq)
                gl = LANE_GATE0 + 3 * h
                outs.append(aux_ref[:, gl:gl + 1] * o_c[rs] + aux_ref[:, gl + 1:gl + 2] * o_s[rs]
                            + aux_ref[:, gl + 2:gl + 3] * o_w[rs])
            oa, ob = outs
            if g == 0:
                ob = pltpu.roll(ob, 64, 1)
            else:
                oa = pltpu.roll(oa, 64, 1)
            o_ref[:, g * LANES:(g + 1) * LANES] = jnp.where(lane < 64, oa, ob)


def _nsa_prompt(pb3, kcmp, vcmp, aux3, *, tq, nb):
    b, t, _ = pb3.shape
    nq = t // tq
    ct = C_NKV // LANES
    cur = lambda c: (lambda bi, qi, ki: (bi, qi, c))
    prev = lambda c: (lambda bi, qi, ki: (bi, jnp.maximum(qi - 1, 0), c))
    kvt = lambda c: (lambda bi, qi, ki: (bi, jnp.minimum(ki, qi), c))
    cmp_spec = pl.BlockSpec((None, NBLK_PAD, LANES), lambda bi, qi, ki: (bi, 0, 0))
    tile = lambda f: pl.BlockSpec((None, tq, LANES), f)
    return pl.pallas_call(
        functools.partial(_nsa_body, tq=tq, nb=nb, topn=min(NSA_TOPN, nb)),
        grid=(b, nq, nq),
        in_specs=[pl.BlockSpec((None, tq, 256), lambda bi, qi, ki: (bi, qi, C_NQ // 256)),
                  tile(kvt(ct + 2)), tile(kvt(ct + 3)),
                  tile(cur(ct + 4)), tile(prev(ct + 4)), tile(cur(ct + 5)), tile(prev(ct + 5)),
                  cmp_spec, cmp_spec,
                  pl.BlockSpec((None, tq, LANES), lambda bi, qi, ki: (bi, qi, 0))],
        out_specs=pl.BlockSpec((None, tq, 256), lambda bi, qi, ki: (bi, qi, 0)),
        out_shape=jax.ShapeDtypeStruct((b, t, 256), F32),
        scratch_shapes=[pltpu.VMEM((G_NSA, tq, NBLK_PAD), BF16),
                        pltpu.VMEM((G_NSA, 2 * tq, LANES), F32), pltpu.VMEM((G_NSA, 2 * tq, LANES), F32),
                        pltpu.VMEM((G_NSA, 2 * tq, LANES), F32),
                        pltpu.VMEM((G_NSA, 2 * tq, LANES), F32),
                        pltpu.VMEM((G_NSA, 2 * tq, LANES), F32)],
        compiler_params=pltpu.CompilerParams(
            dimension_semantics=("parallel", "parallel", "arbitrary"),
            vmem_limit_bytes=VMEM_LIMIT),
        name="nsa_prompt",
    )(pb3, pb3, pb3, pb3, pb3, pb3, pb3, kcmp, vcmp, aux3)


def _mem_attend(mq_ref, mk_ref, mv_ref, rows):
    lane = _iota((rows, LANES), 1)
    pairs = []
    for pr in range(H_MEM // 2):
        cs = slice(pr * LANES, (pr + 1) * LANES)
        q2 = mq_ref[:, cs]
        halves = []
        for hh in range(2):
            qm = jnp.where((lane >= 64 * hh) & (lane < 64 * hh + 64), q2, jnp.zeros_like(q2))
            s = _nt(qm, mk_ref[:, cs])
            e = jnp.exp(s - jnp.max(s, axis=1, keepdims=True))
            p = e / jnp.sum(e, axis=1, keepdims=True)
            halves.append(_mm(p.astype(BF16), mv_ref[:, cs]))
        pairs.append(jnp.where(lane < 64, halves[0], halves[1]))
    return pairs


def _mix_out(x, parts, z_of, wout_ref, fg_ref, final):
    y = x
    for c, part in enumerate(parts):
        z = z_of(c)
        gated = (part * (z * _sigmoid(z))).astype(BF16)
        y = y + _mm(gated, wout_ref[c * LANES:(c + 1) * LANES, :])
    if final:
        y = y * lax.rsqrt(jnp.mean(y * y, axis=-1, keepdims=True) + RMS_EPS) * fg_ref[...]
    return y


def _pool_window_lane(shape):
    grp = _iota(shape, 1) // (D_GROUP // len(POOL_WINDOWS))
    wl = jnp.where(grp == 0, POOL_WINDOWS[0], jnp.where(grp == 1, POOL_WINDOWS[1],
                   jnp.where(grp == 2, POOL_WINDOWS[2], POOL_WINDOWS[3])))
    return grp, wl


HALO = 32


def _post_body(oa_ref, ob_ref, pz_ref, halo_ref, mq_ref, mk_ref, mv_ref, x_ref, wout_ref, pw_ref,
               ps_ref, fg_ref, y_ref, s0, s1, s2, s3, *, tm, final):
    i = pl.program_id(1)
    u = pz_ref[:, 0:256]
    s0[0:HALO, :] = jnp.where(i > 0, halo_ref[...], 0.0)
    s0[HALO:HALO + tm, :] = u
    n = tm + HALO
    s1[8:n, :] = s0[8:n, :] + s0[7:n - 1, :]
    s2[16:n, :] = s1[16:n, :] + s1[14:n - 2, :]
    s3[24:n, :] = s2[24:n, :] + s2[20:n - 4, :]
    a16 = s3[HALO:n, :] + s3[HALO - 8:n - 8, :]
    grp, wl = _pool_window_lane((tm, 256))
    tsum = jnp.where(grp == 0, s1[HALO:n, :], jnp.where(grp == 1, s2[HALO:n, :],
                     jnp.where(grp == 2, s3[HALO:n, :], a16)))
    pos = i * tm + _iota((tm, 256), 0)
    cnt = jnp.minimum(pos + 1, wl).astype(F32)
    o_pool = _mm((tsum / cnt - u).astype(BF16), pw_ref[...]) * ps_ref[...]
    o_mem = _mem_attend(mq_ref, mk_ref, mv_ref, tm)
    parts = [oa_ref[:, 0:LANES], oa_ref[:, LANES:256], ob_ref[:, 0:LANES], ob_ref[:, LANES:256],
             o_pool[:, 0:LANES], o_pool[:, LANES:256], o_mem[0], o_mem[1]]
    z_of = lambda c: pz_ref[:, 256 + c * LANES:256 + (c + 1) * LANES]
    y_ref[...] = _mix_out(x_ref[...], parts, z_of, wout_ref, fg_ref, final)


def _post_prompt(oa, ob, pz3, pb3, memb, x3, wout, pw_bd, ps, fg, *, tm, final):
    b, t, d = x3.shape
    nt = t // tm
    row = lambda bi, i: (bi, i, 0)
    const = lambda bi, i: (0, 0)
    return pl.pallas_call(
        functools.partial(_post_body, tm=tm, final=final),
        grid=(b, nt),
        in_specs=[pl.BlockSpec((None, tm, 256), row), pl.BlockSpec((None, tm, 256), row),
                  pl.BlockSpec((None, tm, 1280), row),
                  pl.BlockSpec((None, HALO, 256),
                               lambda bi, i: (bi, jnp.maximum(i * (tm // HALO) - 1, 0), 0)),
                  pl.BlockSpec((None, tm, 256), lambda bi, i: (bi, i, C_MQ // 256)),
                  pl.BlockSpec((None, 256, 256), lambda bi, i: (bi, 0, 0)),
                  pl.BlockSpec((None, 256, 256), lambda bi, i: (bi, 0, 1)),
                  pl.BlockSpec((None, tm, d), row),
                  pl.BlockSpec((d, d), const), pl.BlockSpec((256, 256), const),
                  pl.BlockSpec((1, 256), const), pl.BlockSpec((1, d), const)],
        out_specs=pl.BlockSpec((None, tm, d), row),
        out_shape=jax.ShapeDtypeStruct((b, t, d), F32),
        scratch_shapes=[pltpu.VMEM((tm + HALO, 256), F32)] * 4,
        compiler_params=pltpu.CompilerParams(dimension_semantics=("parallel", "arbitrary"),
                                             vmem_limit_bytes=VMEM_LIMIT),
        name="post_prompt",
    )(oa, ob, pz3, pz3, pb3, memb, memb, x3, wout, pw_bd, ps, fg)


ROWS_PER_PAGE = 8


def _cmp_sample_body(pt_ref, flat_hbm, pe_ref, w1_ref, w2_ref, o_ref, xbuf, sem, *, per_step, layer):
    s = pl.program_id(0)
    nsteps = pl.num_programs(0)

    def copies(step, slot):
        out = []
        for i in range(per_step):
            page = pt_ref[step * per_step + i]
            out.append(pltpu.make_async_copy(
                flat_hbm.at[layer, page], xbuf.at[slot, pl.ds(i * ROWS_PER_PAGE, ROWS_PER_PAGE), :],
                sem.at[slot]))
        return out

    @pl.when(s == 0)
    def _():
        for c in copies(0, 0):
            c.start()

    @pl.when(s + 1 < nsteps)
    def _():
        for c in copies(s + 1, (s + 1) % 2):
            c.start()

    slot = s % 2
    for c in copies(s, slot):
        c.wait()
    rows = per_step * ROWS_PER_PAGE
    x = xbuf[slot].reshape(per_step, ROWS_PER_PAGE, xbuf.shape[-1]) + pe_ref[...][None]
    xb = x.reshape(rows, xbuf.shape[-1]).astype(BF16)
    h = _mm(xb, w1_ref[...])
    h = (h * _sigmoid(h)).astype(BF16)
    o_ref[...] = jnp.concatenate([_mm(h[:, 0:256], w2_ref[0]), _mm(h[:, 256:512], w2_ref[1])], axis=1)


def _compress_sample(page_flat, cmpflat, pe8, w1cat, w2, *, ns, layer):
    npages = page_flat.shape[0]
    f = cmpflat.shape[-1]
    rows = npages * ROWS_PER_PAGE
    nsteps = npages // ns
    step_rows = ns * ROWS_PER_PAGE
    grid_spec = pltpu.PrefetchScalarGridSpec(
        num_scalar_prefetch=1, grid=(nsteps,),
        in_specs=[pl.BlockSpec(memory_space=pl.ANY),
                  pl.BlockSpec((ROWS_PER_PAGE, f), lambda s, pt: (0, 0)),
                  pl.BlockSpec((f, 512), lambda s, pt: (0, 0)),
                  pl.BlockSpec((2, 256, HEAD_DIM), lambda s, pt: (0, 0, 0))],
        out_specs=pl.BlockSpec((step_rows, LANES), lambda s, pt: (s, 0)),
        scratch_shapes=[pltpu.VMEM((2, step_rows, f), F32), pltpu.SemaphoreType.DMA((2,))])
    return pl.pallas_call(
        functools.partial(_cmp_sample_body, per_step=ns, layer=layer),
        grid_spec=grid_spec,
        out_shape=jax.ShapeDtypeStruct((rows, LANES), F32),
        compiler_params=pltpu.CompilerParams(dimension_semantics=("arbitrary",),
                                             vmem_limit_bytes=VMEM_LIMIT),
        name="cmp_sample",
    )(page_flat, cmpflat, pe8, w1cat, w2)


HROWS = 16


def _rows_from_lanes(vec_row, pick):
    return jnp.sum(jnp.where(pick, vec_row, 0.0), axis=1, keepdims=True)


def _bf16r(x):
    return x.astype(BF16).astype(F32)


def _att_sample_body(pt_ref, fox_hbm, lf_hbm, nsa_hbm, pb_ref, sf_ref, aux_ref, win_ref, mem_ref,
                     kc_ref, vc_ref, exp_ref, o_ref, wout_ref, fbuf, lbuf, sbuf, sem,
                     *, npg, past, layer):
    b = pl.program_id(0)
    nb = pl.num_programs(0)

    def copies(seq, slot):
        out = []
        for i in range(npg):
            page = pt_ref[seq * npg + i]
            lanes = pl.ds(i * PAGE_SIZE, PAGE_SIZE)
            out.append(pltpu.make_async_copy(
                fox_hbm.at[layer, page], fbuf.at[slot, :, lanes], sem.at[0, slot]))
            out.append(pltpu.make_async_copy(
                lf_hbm.at[layer, page], lbuf.at[slot, pl.ds(i * 8, 8), :], sem.at[1, slot]))
            out.append(pltpu.make_async_copy(
                nsa_hbm.at[layer, page, pl.ds(256, 256), :], sbuf.at[slot, :, lanes], sem.at[2, slot]))
        return out

    @pl.when(b == 0)
    def _():
        for c in copies(0, 0):
            c.start()

    @pl.when(b + 1 < nb)
    def _():
        for c in copies(b + 1, (b + 1) % 2):
            c.start()

    slot = b % 2
    for c in copies(b, slot):
        c.wait()

    row8 = _iota((HROWS, LANES), 0)
    lane8 = _iota((HROWS, LANES), 1)
    aux = aux_ref[...]
    pbr = pb_ref[...].astype(F32)
    sfr = sf_ref[...]

    def head_rows(q256):
        r = _iota((HROWS, 256), 0)
        l = _iota((HROWS, 256), 1)
        return jnp.where(l // HEAD_DIM == r, q256, 0.0).astype(BF16)

    def pick_heads(o8):
        r = _iota((HROWS, 256), 0)
        l = _iota((HROWS, 256), 1)
        return jnp.sum(jnp.where(l // HEAD_DIM == r, o8, 0.0), axis=0, keepdims=True)

    kt = fbuf[slot, 0:256, :].astype(BF16)
    vt = fbuf[slot, 256:512, :].astype(BF16)
    qf = head_rows(pbr[:, C_FQ:C_FK])
    s = _mm(qf, kt)
    lf2 = lbuf[slot]
    r2 = _iota((npg * 8, npg * 8), 0)
    c2 = _iota((npg * 8, npg * 8), 1)
    upper = (_iota((LANES, LANES), 0) > _iota((LANES, LANES), 1)).astype(BF16)
    within = _exact_mm_rhs(lf2, upper)
    tot = jnp.sum(lf2, axis=1, keepdims=True)
    later_pages = ((c2 % 8 == r2 % 8) & (c2 // 8 > r2 // 8)).astype(BF16)
    later = _exact_mm(later_pages, jnp.broadcast_to(tot, (npg * 8, LANES)))
    rr = _iota((npg * 8, LANES), 0)
    ll = _iota((npg * 8, LANES), 1)
    lf_new = _rows_from_lanes(aux, ll == rr % 8)
    bias = within + later + lf_new
    zpad = jnp.zeros((HROWS - 8, LANES), F32)
    s = jnp.concatenate(
        [s[:, i * PAGE_SIZE:(i + 1) * PAGE_SIZE] + jnp.concatenate([bias[i * 8:(i + 1) * 8, :], zpad], axis=0)
         for i in range(npg)], axis=1)
    knew = _bf16r(sfr[:, 0:256])
    vnew = _bf16r(sfr[:, 256:512])
    s_new = jnp.sum(qf.astype(F32) * knew, axis=1, keepdims=True)
    m = jnp.maximum(jnp.max(s, axis=1, keepdims=True), s_new)
    e = jnp.exp(s - m)
    e_new = jnp.exp(s_new - m)
    den = jnp.sum(e, axis=1, keepdims=True) + e_new
    o8 = (_nt(e.astype(BF16), vt) + _bf16r(e_new) * vnew) / den
    o_ref[:, 0:256] = pick_heads(o8)

    nq = pbr[:, C_NQ:C_NKV]
    t0 = jnp.broadcast_to(nq[:, 0:LANES], (HROWS, LANES))
    t1 = jnp.broadcast_to(nq[:, LANES:256], (HROWS, LANES))
    t0r, t1r = pltpu.roll(t0, 64, 1), pltpu.roll(t1, 64, 1)
    lo = lane8 < 64
    hrows = [jnp.where(lo, t0, 0.0), jnp.where(lo, t0r, 0.0),
             jnp.where(lo, 0.0, t1r), jnp.where(lo, 0.0, t1)]
    qn = jnp.zeros((HROWS, LANES), F32)
    for h in range(H_NSA):
        qn = jnp.where(row8 == h, hrows[h], qn)
    qn_b = qn.astype(BF16)
    slope8 = jnp.zeros((HROWS, 1), F32)
    r81 = _iota((HROWS, 1), 0)
    for h in range(H_NSA):
        slope8 = jnp.where(r81 == h, ALIBI_SLOPES[h], slope8)
    nkv_new = sfr[:, 512:1280]
    nblk = past // NSA_BLOCK

    def new_score(k128):
        return jnp.sum(qn_b.astype(F32) * _bf16r(k128), axis=1, keepdims=True)

    sc = _nt(qn_b, kc_ref[...].astype(BF16))
    cend = _iota((HROWS, nblk), 1) * NSA_BLOCK + (NSA_BLOCK - 1)
    sc = sc - slope8 * (past - cend).astype(F32)
    ec = jnp.exp(sc - jnp.max(sc, axis=1, keepdims=True))
    pc = ec / jnp.maximum(jnp.sum(ec, axis=1, keepdims=True), 1e-30)
    o_c = _mm(pc.astype(BF16), vc_ref[...].astype(BF16))
    imp = jnp.concatenate([pc[2 * g:2 * g + 1] + pc[2 * g + 1:2 * g + 2] for g in range(G_NSA)]
                          + [jnp.zeros((HROWS - G_NSA, nblk), F32)], axis=0)
    if nblk < LANES:
        imp = jnp.concatenate([imp, jnp.zeros((HROWS, LANES - nblk), F32)], axis=1)
    forced = (lane8 == 0) | (lane8 == nblk) | (lane8 == nblk - 1)
    score = jnp.where(lane8 > nblk, -1.0, jnp.where(forced, FORCE_SCORE, imp))
    rank = jnp.zeros((HROWS, LANES), F32)
    for mblk in range(nblk + 1):
        r = score[:, mblk:mblk + 1]
        rank = rank + jnp.where((r > score) | ((r == score) & (lane8 > mblk)), 1.0, 0.0)
    sel_g = jnp.where((rank < min(NSA_TOPN, nblk + 1)) & (score >= 0.0), 1.0, 0.0)
    sel_h = jnp.zeros((HROWS, LANES), F32)
    for h in range(H_NSA):
        sel_h = jnp.where(row8 == h, sel_g[h // 2:h // 2 + 1, :], sel_h)
    valid = _mm(sel_h.astype(BF16), exp_ref[...]) > 0.5
    kst = sbuf[slot, 0:LANES, :].astype(BF16)
    vst = sbuf[slot, LANES:256, :].astype(BF16)
    ss = _mm(qn_b, kst) - slope8 * (past - _iota((HROWS, past), 1)).astype(F32)
    ss = jnp.where(valid, ss, NEG_INF)
    ss_new = new_score(nkv_new[:, 256:384])
    ms = jnp.maximum(jnp.max(ss, axis=1, keepdims=True), ss_new)
    es = jnp.where(valid, jnp.exp(ss - ms), 0.0)
    es_new = jnp.exp(ss_new - ms)
    o_s = (_nt(es.astype(BF16), vst) + _bf16r(es_new) * _bf16r(nkv_new[:, 384:512])) \
        / (jnp.sum(es, axis=1, keepdims=True) + es_new)
    wl_ = win_ref.shape[1]
    win = win_ref[...]
    sw = _mm(qn_b, win[0:LANES].astype(BF16)) - slope8 * (wl_ - _iota((HROWS, wl_), 1)).astype(F32)
    sw_new = new_score(nkv_new[:, 512:640])
    mw = jnp.maximum(jnp.max(sw, axis=1, keepdims=True), sw_new)
    ew = jnp.exp(sw - mw)
    ew_new = jnp.exp(sw_new - mw)
    o_w = (_nt(ew.astype(BF16), win[LANES:256].astype(BF16))
           + _bf16r(ew_new) * _bf16r(nkv_new[:, 640:768])) / (jnp.sum(ew, axis=1, keepdims=True) + ew_new)
    rw = _iota((256, 256), 0)
    lw_ = _iota((256, 256), 1)
    new_col = jnp.sum(jnp.where(rw == lw_, nkv_new[:, 512:768], 0.0), axis=1, keepdims=True)
    wout_ref[...] = jnp.where(_iota((256, wl_), 1) == wl_ - 1, new_col, pltpu.roll(win, wl_ - 1, 1))
    gates = [_rows_from_lanes(aux, lane8 == LANE_GATE0 + 3 * row8 + c) for c in range(3)]
    o_n = gates[0] * o_c + gates[1] * o_s + gates[2] * o_w
    o_nr = pltpu.roll(o_n, 64, 1)
    lo1 = _iota((1, LANES), 1) < 64
    o_ref[:, 256:384] = jnp.where(lo1, o_n[0:1], o_nr[1:2])
    o_ref[:, 384:512] = jnp.where(lo1, o_nr[2:3], o_n[3:4])

    qm = head_rows(pbr[:, C_MQ:C_PU])
    sm = _mm(qm, mem_ref[0:256, :].astype(BF16))
    em = jnp.exp(sm - jnp.max(sm, axis=1, keepdims=True))
    pm = em / jnp.sum(em, axis=1, keepdims=True)
    o_ref[:, 512:768] = pick_heads(_nt(pm.astype(BF16), mem_ref[256:512, :].astype(BF16)))


def _exact_mm_rhs(x, m01):
    p1, p2, p3 = _split3(x)
    return _mm(p1, m01) + _mm(p2, m01) + _mm(p3, m01)


def _attend_sample(page_flat, fox_t, lf_t, nsa_t, pb3, sf3, aux3, win_t, mem_t, kcmp, vcmp, expand,
                   *, npg, layer):
    db = pb3.shape[0]
    past = npg * PAGE_SIZE
    per_seq = lambda blk: pl.BlockSpec((None,) + blk, lambda b, pt: (b, 0, 0))
    per_seq_layer = lambda blk: pl.BlockSpec((None, None) + blk, lambda b, pt: (layer, b, 0, 0))
    grid_spec = pltpu.PrefetchScalarGridSpec(
        num_scalar_prefetch=1, grid=(db,),
        in_specs=[pl.BlockSpec(memory_space=pl.ANY), pl.BlockSpec(memory_space=pl.ANY),
                  pl.BlockSpec(memory_space=pl.ANY),
                  per_seq((1, N_PB)), per_seq((1, 1280)), per_seq((1, LANES)),
                  per_seq_layer(win_t.shape[2:]), per_seq_layer(mem_t.shape[2:]),
                  per_seq(kcmp.shape[1:]), per_seq(vcmp.shape[1:]),
                  pl.BlockSpec(expand.shape, lambda b, pt: (0, 0))],
        out_specs=[per_seq((1, 768)), per_seq(win_t.shape[2:])],
        scratch_shapes=[pltpu.VMEM((2, 512, past), F32), pltpu.VMEM((2, npg * 8, LANES), F32),
                        pltpu.VMEM((2, 256, past), F32), pltpu.SemaphoreType.DMA((3, 2))])
    return pl.pallas_call(
        functools.partial(_att_sample_body, npg=npg, past=past, layer=layer),
        grid_spec=grid_spec,
        out_shape=[jax.ShapeDtypeStruct((db, 1, 768), F32),
                   jax.ShapeDtypeStruct((db,) + win_t.shape[2:], F32)],
        compiler_params=pltpu.CompilerParams(dimension_semantics=("arbitrary",),
                                             vmem_limit_bytes=VMEM_LIMIT),
        name="att_sample",
    )(page_flat, fox_t, lf_t, nsa_t, pb3, sf3, aux3, win_t, mem_t, kcmp, vcmp, expand)


def _post_sample_body(oatt_ref, pz_ref, sp_ref, x_ref, wout_ref, pw_ref, ps_ref, fg_ref, y_ref,
                      spo_ref, *, rows, past, final):
    u = pz_ref[:, 0:256]
    grp, wl = _pool_window_lane((rows, 256))
    tsum = u
    for j in range(1, POOL_BUF + 1):
        tsum = tsum + jnp.where(wl > j, sp_ref[POOL_BUF - j], 0.0)
    for j in range(POOL_BUF - 1):
        spo_ref[j] = sp_ref[j + 1]
    spo_ref[POOL_BUF - 1] = u
    cnt = jnp.minimum(past + 1, wl).astype(F32)
    o_pool = _mm((tsum / cnt - u).astype(BF16), pw_ref[...]) * ps_ref[...]
    parts = [oatt_ref[:, c * LANES:(c + 1) * LANES] for c in range(4)]
    parts += [o_pool[:, 0:LANES], o_pool[:, LANES:256]]
    parts += [oatt_ref[:, 512:640], oatt_ref[:, 640:768]]
    z_of = lambda c: pz_ref[:, 256 + c * LANES:256 + (c + 1) * LANES]
    y_ref[...] = _mix_out(x_ref[...], parts, z_of, wout_ref, fg_ref, final)


def _post_sample(oatt, pz, pool_t, x2d, wout, pw_bd, ps, fg, *, past, final, layer):
    rows, d = x2d.shape
    full = lambda a: pl.BlockSpec(a.shape, lambda i: (0,) * a.ndim)
    return pl.pallas_call(
        functools.partial(_post_sample_body, rows=rows, past=past, final=final),
        grid=(1,),
        in_specs=[full(oatt), full(pz),
                  pl.BlockSpec((None, POOL_BUF, rows, 256), lambda i: (layer, 0, 0, 0)),
                  full(x2d), full(wout), full(pw_bd), full(ps), full(fg)],
        out_specs=[pl.BlockSpec((rows, d), lambda i: (0, 0)),
                   pl.BlockSpec((POOL_BUF, rows, 256), lambda i: (0, 0, 0))],
        out_shape=[jax.ShapeDtypeStruct((rows, d), F32),
                   jax.ShapeDtypeStruct((POOL_BUF, rows, 256), F32)],
        compiler_params=pltpu.CompilerParams(vmem_limit_bytes=VMEM_LIMIT),
        name="post_sample",
    )(oatt, pz, pool_t, x2d, wout, pw_bd, ps, fg)


def _prep_layer_weights(norm_g, w_in, b_fgt, nsa_pe, nsa_w1, nsa_w2, pool_w, pool_scale,
                        w_mem_kv, w_out):
    d = w_in.shape[0]
    o = 0
    segs = {}
    for name, width in (("fqkv", 768), ("ff", H_FOX), ("nq", 256), ("nkv", 768),
                        ("ng", 3 * H_NSA), ("pu", 256), ("mq", 256), ("z", d)):
        segs[name] = w_in[:, o:o + width]
        o += width
    pad = jnp.zeros((d, LANES - H_FOX - 3 * H_NSA), w_in.dtype)
    wp = jnp.concatenate([segs["fqkv"], segs["nq"], segs["nkv"], segs["mq"], segs["pu"],
                          segs["z"], segs["ff"], segs["ng"], pad], axis=1).astype(BF16)
    bvec = jnp.zeros((1, LANES), F32).at[0, 0:H_FOX].set(b_fgt.astype(F32))
    gw = D_GROUP // len(POOL_WINDOWS)
    pw_bd = jnp.zeros((D_GROUP, D_GROUP), F32)
    for g in range(len(POOL_WINDOWS)):
        pw_bd = pw_bd.at[g * gw:(g + 1) * gw, g * gw:(g + 1) * gw].set(pool_w[g])
    return dict(g=norm_g.reshape(1, d), wp=wp, bvec=bvec,
                pe_flat=nsa_pe.reshape(2, 1, NSA_BLOCK * HEAD_DIM),
                w1=nsa_w1.astype(BF16), w2=nsa_w2.astype(BF16),
                w1cat=jnp.concatenate([nsa_w1[0], nsa_w1[1]], axis=1).astype(BF16),
                pw_bd=pw_bd.astype(BF16), ps=pool_scale.reshape(1, D_GROUP).astype(F32),
                wmkv=w_mem_kv.astype(BF16), wout=w_out.astype(BF16))


def _layer_prompt(x, mem, lw, fg, final, tile):
    b, t, d = x.shape
    n = b * t
    nb = t // NSA_BLOCK
    pb, pz, aux, sfc, ct, lft, fkvt, nkvt, nwt = _project(
        x.reshape(n, d), lw["g"], lw["wp"], lw["bvec"], tm=tile, tiles_per_seq=t // tile, prompt=True)
    pb3 = pb.reshape(b, t, N_PB)
    o_a = _fox_prompt(pb3, ct, tq=tile)
    kvc = sfc.reshape(b, nb, NSA_BLOCK, 2, G_NSA, HEAD_DIM)
    flat = jnp.transpose(kvc, (3, 0, 1, 4, 2, 5)).reshape(2, b * nb * G_NSA, NSA_BLOCK * HEAD_DIM)
    cmp = _compress_prompt(flat, lw["pe_flat"], lw["w1"], lw["w2"]).reshape(2, b, nb, LANES)
    cmp = jnp.pad(cmp, ((0, 0), (0, 0), (0, NBLK_PAD - nb), (0, 0))).astype(BF16)
    o_b = _nsa_prompt(pb3, cmp[0], cmp[1], aux.reshape(b, t, LANES), tq=tile, nb=nb)
    n_mem = mem.shape[1]
    mem_t, memb = _mem_project(mem, lw["wmkv"])
    pz3 = pz.reshape(b, t, 1280)
    y = _post_prompt(o_a, o_b, pz3, pb3, memb, x, lw["wout"], lw["pw_bd"], lw["ps"], fg,
                     tm=tile, final=final)
    w_keep = min(NSA_WINDOW, t)
    back = (0, 4, 1, 2, 3)
    state = (jnp.transpose(fkvt.reshape(b, 2, H_FOX, HEAD_DIM, t), back),
             jnp.transpose(lft[:, 0:H_FOX, :], (0, 2, 1)),
             jnp.transpose(nkvt.reshape(b, 4, G_NSA, HEAD_DIM, t), back),
             jnp.transpose(nwt[:, :, t - w_keep:].reshape(b, 2, G_NSA, HEAD_DIM, w_keep), back),
             pz3[:, t - POOL_BUF:, 0:256],
             jnp.transpose(mem_t.reshape(b, 2, H_MEM, HEAD_DIM, n_mem), back))
    return y, state


def _native_views(cache_fox_kv, cache_fox_lf, cache_nsa_kv, state_nsa_win, state_pool, cache_mem_kv):
    dp, n_phys = cache_fox_kv.shape[:2]
    db = state_nsa_win.shape[1]
    to_t = (0, 1, 3, 4, 5, 2)
    fox_t = jnp.transpose(cache_fox_kv, to_t).reshape(dp, n_phys, 512, PAGE_SIZE)
    lf_t = jnp.pad(jnp.transpose(cache_fox_lf.astype(F32), (0, 1, 3, 2)),
                   ((0, 0), (0, 0), (0, 8 - H_FOX), (0, 0)))
    nsa_t = jnp.transpose(cache_nsa_kv, to_t).reshape(dp, n_phys, 512, PAGE_SIZE)
    win_t = jnp.transpose(state_nsa_win, to_t).reshape(dp, db, 256, state_nsa_win.shape[2])
    mem_t = jnp.transpose(cache_mem_kv, to_t).reshape(dp, db, 512, cache_mem_kv.shape[2])
    pool_t = jnp.transpose(state_pool, (0, 2, 1, 3))
    cmpflat = jnp.transpose(
        nsa_t[:, :, 0:256].reshape(dp, n_phys, 2, G_NSA, HEAD_DIM, PAGE_SIZE // NSA_BLOCK, NSA_BLOCK),
        (0, 1, 2, 3, 5, 6, 4)).reshape(dp, n_phys, ROWS_PER_PAGE, NSA_BLOCK * HEAD_DIM)
    return dict(fox_t=fox_t, lf_t=lf_t, nsa_t=nsa_t, win_t=win_t, mem_t=mem_t, pool_t=pool_t,
                cmpflat=cmpflat)


def _layer_sample(x, nv, page_table, lw, fg, final, layer):
    db, ds, d = x.shape
    npg = page_table.shape[1]
    past = npg * PAGE_SIZE
    nblk = past // NSA_BLOCK
    pb, pz, aux, sf, sft, auxt = _project(x.reshape(db, d), lw["g"], lw["wp"], lw["bvec"],
                                          tm=db, tiles_per_seq=1, prompt=False)
    page_flat = page_table.reshape(db * npg).astype(I32)
    pe8 = jnp.repeat(lw["pe_flat"].reshape(2, NSA_BLOCK * HEAD_DIM), ROWS_PER_PAGE // 2, axis=0)
    cmp = _compress_sample(page_flat, nv["cmpflat"], pe8, lw["w1cat"], lw["w2"], ns=64, layer=layer)
    cmp = cmp.reshape(db, npg, 2, G_NSA, PAGE_SIZE // NSA_BLOCK, LANES)
    kcmp = jnp.transpose(cmp[:, :, 0, :, :, 0:HEAD_DIM], (0, 1, 3, 2, 4)).reshape(db, nblk, LANES)
    vcmp = jnp.transpose(cmp[:, :, 1, :, :, HEAD_DIM:LANES], (0, 1, 3, 2, 4)).reshape(db, nblk, LANES)
    expand = (np.arange(LANES)[:, None] == (np.arange(past)[None, :] // NSA_BLOCK)).astype(np.float32)
    oatt, win_new = _attend_sample(
        page_flat, nv["fox_t"], nv["lf_t"], nv["nsa_t"], pb.reshape(db, 1, N_PB),
        sf.reshape(db, 1, 1280), aux.reshape(db, 1, LANES), nv["win_t"], nv["mem_t"], kcmp, vcmp,
        jnp.asarray(expand, BF16), npg=npg, layer=layer)
    y, pool_new = _post_sample(oatt.reshape(db, 768), pz, nv["pool_t"], x.reshape(db, d), lw["wout"],
                               lw["pw_bd"], lw["ps"], fg, past=past, final=final, layer=layer)
    w_buf = win_new.shape[-1]
    seq_first = (3, 0, 1, 2)
    state = (jnp.transpose(sft[0:512].reshape(2, H_FOX, HEAD_DIM, db), seq_first)[:, None],
             jnp.transpose(auxt[0:H_FOX])[:, None, :],
             jnp.transpose(sft[512:1024].reshape(4, G_NSA, HEAD_DIM, db), seq_first)[:, None],
             jnp.transpose(win_new.reshape(db, 2, G_NSA, HEAD_DIM, w_buf), (0, 4, 1, 2, 3)),
             jnp.transpose(pool_new, (1, 0, 2)))
    return y.reshape(db, ds, d), state


def kernel(x_prompt, x_sample, cache_fox_kv, cache_fox_lf, cache_nsa_kv, state_nsa_win, state_pool,
           cache_mem_kv, page_table, mem_prompt, norm_g, w_in, b_fgt, nsa_pe, nsa_w1, nsa_w2, pool_w,
           pool_scale, w_mem_kv, w_out, final_g):
    depth = norm_g.shape[0]
    t = x_prompt.shape[1]
    assert x_sample.shape[1] == 1 and t % NSA_WINDOW == 0 and t // NSA_BLOCK <= NBLK_PAD
    hp, hs = x_prompt, x_sample
    fg = final_g.reshape(1, -1).astype(F32)
    nv = _native_views(cache_fox_kv, cache_fox_lf, cache_nsa_kv, state_nsa_win, state_pool,
                       cache_mem_kv)
    sp = [[] for _ in range(6)]
    ss = [[] for _ in range(5)]
    for l in range(depth):
        lw = _prep_layer_weights(norm_g[l], w_in[l], b_fgt[l], nsa_pe[l], nsa_w1[l], nsa_w2[l],
                                 pool_w[l], pool_scale[l], w_mem_kv[l], w_out[l])
        final = l == depth - 1
        hp, st_p = _layer_prompt(hp, mem_prompt, lw, fg, final, NSA_WINDOW)
        hs, st_s = _layer_sample(hs, nv, page_table, lw, fg, final, l)
        for acc, s in zip(sp, st_p):
            acc.append(s)
        for acc, s in zip(ss, st_s):
            acc.append(s)
    return (hp, hs) + tuple(jnp.stack(a) for a in sp) + tuple(jnp.stack(a) for a in ss)
```

```python
import functools

import numpy as np
import jax
import jax.numpy as jnp
from jax import lax
from jax.experimental import pallas as pl
from jax.experimental.pallas import tpu as pltpu

F32 = jnp.float32
BF16 = jnp.bfloat16
I32 = jnp.int32

HEAD_DIM = 64
H_FOX = 4
H_NSA = 4
G_NSA = 2
H_MEM = 4
D_GROUP = 256
POOL_WINDOWS = (2, 4, 8, 16)
POOL_BUF = 15
NSA_BLOCK = 64
NSA_TOPN = 16
NSA_WINDOW = 512
PAGE_SIZE = 128
RMS_EPS = 1e-6
NEG_INF = -1e30
FORCE_SCORE = 1e4
SCALE = HEAD_DIM ** -0.5
ALIBI_SLOPES = tuple(2.0 ** (-8.0 * (h + 1) / H_NSA) for h in range(H_NSA))

LANES = 128
NBLK_PAD = 128
VMEM_LIMIT = 56 * 1024 * 1024

C_FQ, C_FK, C_FV, C_NQ, C_NKV, C_MQ, C_PU, C_Z, C_SM, C_END = (
    0, 256, 512, 768, 1024, 1792, 2048, 2304, 3328, 3456)
N_PB = C_PU
LANE_GATE0 = 4


def _nt(a, b):
    return lax.dot_general(a, b, (((1,), (1,)), ((), ())), preferred_element_type=F32)


def _mm(a, b):
    return jnp.dot(a, b, preferred_element_type=F32)


def _split3(x):
    p1 = x.astype(BF16)
    r1 = x - p1.astype(F32)
    p2 = r1.astype(BF16)
    p3 = (r1 - p2.astype(F32)).astype(BF16)
    return p1, p2, p3


def _exact_mm(m01, x):
    p1, p2, p3 = _split3(x)
    return _mm(m01, p1) + _mm(m01, p2) + _mm(m01, p3)


def _sigmoid(x):
    return 1.0 / (1.0 + jnp.exp(-x))


def _log_sigmoid(x):
    return jnp.minimum(x, 0.0) - jnp.log1p(jnp.exp(-jnp.abs(x)))


def _iota(shape, dim):
    return lax.broadcasted_iota(I32, shape, dim)


def _proj_body(x_ref, g_ref, w_ref, b_ref, pb_ref, pz_ref, aux_ref, *rest, tm, tiles_per_seq, prompt):
    x = x_ref[...]
    ms = jnp.mean(x * x, axis=-1, keepdims=True)
    xn = (x * lax.rsqrt(ms + RMS_EPS) * g_ref[...]).astype(BF16)

    def seg(a, b):
        return _mm(xn, w_ref[:, a:b])

    pb_ref[:, C_FQ:C_FK] = (seg(C_FQ, C_FK) * SCALE).astype(BF16)
    fkv = seg(C_FK, C_NQ)
    pb_ref[:, C_FK:C_NQ] = fkv.astype(BF16)
    pb_ref[:, C_NQ:C_NKV] = (seg(C_NQ, C_NKV) * SCALE).astype(BF16)
    nkv = seg(C_NKV, C_MQ)
    pb_ref[:, C_NKV:C_MQ] = nkv.astype(BF16)
    pb_ref[:, C_MQ:C_PU] = (seg(C_MQ, C_PU) * SCALE).astype(BF16)
    pz_ref[...] = seg(C_PU, C_SM)
    small = seg(C_SM, C_END)

    lane = _iota((tm, LANES), 1)
    lf = _log_sigmoid(small + b_ref[...])
    aux = jnp.where(lane < H_FOX, lf, jnp.where(lane < LANE_GATE0 + 3 * H_NSA, _sigmoid(small), 0.0))
    aux_ref[...] = aux
    if prompt:
        sfc_ref, ct_ref, lft_ref, fkvt_ref, nkvt_ref, nwt_ref, carry_ref = rest
        sfc_ref[...] = nkv[:, 0:256]
        fkvt_ref[...] = fkv.T
        nkvt_ref[...] = nkv[:, 0:512].T
        nwt_ref[...] = nkv[:, 512:768].T
        lfm = jnp.where(lane < H_FOX, lf, 0.0)
        lft_ref[...] = lfm.T[0:8, :]
        tri = (_iota((LANES, LANES), 0) >= _iota((LANES, LANES), 1)).astype(BF16)
        first = (pl.program_id(0) % tiles_per_seq) == 0
        carry = jnp.where(first, 0.0, carry_ref[0:1, :])
        for r in range(tm // LANES):
            blk = lfm[r * LANES:(r + 1) * LANES]
            cblk = _exact_mm(tri, blk) + carry
            carry = cblk[LANES - 1:LANES, :]
            ct_ref[:, r * LANES:(r + 1) * LANES] = cblk.T[0:8, :]
        carry_ref[0:1, :] = carry
    else:
        sf_ref, sft_ref, auxt_ref = rest
        sf_ref[:, 0:512] = fkv
        sf_ref[:, 512:1280] = nkv
        sft_ref[0:512, :] = fkv.T
        sft_ref[512:1280, :] = nkv.T
        auxt_ref[...] = aux.T


def _project(x2d, g, wp, bvec, *, tm, tiles_per_seq, prompt):
    n, d = x2d.shape
    grid = (n // tm,)
    row = lambda i: (i, 0)
    const = lambda i: (0, 0)
    out_shape = [jax.ShapeDtypeStruct((n, N_PB), BF16), jax.ShapeDtypeStruct((n, 1280), F32),
                 jax.ShapeDtypeStruct((n, LANES), F32)]
    out_specs = [pl.BlockSpec((tm, N_PB), row), pl.BlockSpec((tm, 1280), row),
                 pl.BlockSpec((tm, LANES), row)]
    scratch = []
    if prompt:
        nseq = n // (tm * tiles_per_seq)
        t = tm * tiles_per_seq
        seq_t = lambda rows: pl.BlockSpec(
            (None, rows, tm), lambda i: (i // tiles_per_seq, 0, i % tiles_per_seq))
        out_shape += [jax.ShapeDtypeStruct((n, 256), F32),
                      jax.ShapeDtypeStruct((nseq, 8, t), F32), jax.ShapeDtypeStruct((nseq, 8, t), F32),
                      jax.ShapeDtypeStruct((nseq, 512, t), F32), jax.ShapeDtypeStruct((nseq, 512, t), F32),
                      jax.ShapeDtypeStruct((nseq, 256, t), F32)]
        out_specs += [pl.BlockSpec((tm, 256), row),
                      seq_t(8), seq_t(8), seq_t(512), seq_t(512), seq_t(256)]
        scratch = [pltpu.VMEM((8, LANES), F32)]
    else:
        assert grid == (1,)
        out_shape += [jax.ShapeDtypeStruct((n, 1280), F32), jax.ShapeDtypeStruct((1280, n), F32),
                      jax.ShapeDtypeStruct((LANES, n), F32)]
        out_specs += [pl.BlockSpec((tm, 1280), row), pl.BlockSpec((1280, tm), const),
                      pl.BlockSpec((LANES, tm), const)]
    return pl.pallas_call(
        functools.partial(_proj_body, tm=tm, tiles_per_seq=tiles_per_seq, prompt=prompt),
        grid=grid,
        in_specs=[pl.BlockSpec((tm, d), row), pl.BlockSpec((1, d), const),
                  pl.BlockSpec((d, C_END), const), pl.BlockSpec((1, LANES), const)],
        out_specs=out_specs, out_shape=out_shape, scratch_shapes=scratch,
        compiler_params=pltpu.CompilerParams(dimension_semantics=("arbitrary",),
                                             vmem_limit_bytes=VMEM_LIMIT),
        name="proj",
    )(x2d, g, wp, bvec)


def _memkv_body(m_ref, w_ref, ot_ref, ob_ref):
    r = _mm(m_ref[...].astype(BF16), w_ref[...])
    ot_ref[...] = r.T
    ob_ref[...] = r.astype(BF16)


def _mem_project(mem3, w):
    b, n_mem, d = mem3.shape
    e = w.shape[1]
    return pl.pallas_call(
        _memkv_body, grid=(b,),
        in_specs=[pl.BlockSpec((None, n_mem, d), lambda i: (i, 0, 0)),
                  pl.BlockSpec((d, e), lambda i: (0, 0))],
        out_specs=[pl.BlockSpec((None, e, n_mem), lambda i: (i, 0, 0)),
                   pl.BlockSpec((None, n_mem, e), lambda i: (i, 0, 0))],
        out_shape=[jax.ShapeDtypeStruct((b, e, n_mem), F32), jax.ShapeDtypeStruct((b, n_mem, e), BF16)],
        name="memkv",
    )(mem3, w)


def _fox_body(qt_ref, kt_ref, q_ref, k_ref, v_ref, ct_ref, o_ref, m_ref, l_ref, acc_ref, *, tq, tk):
    qi = qt_ref[pl.program_id(1)]
    ki = kt_ref[pl.program_id(1)]

    @pl.when(ki == 0)
    def _():
        m_ref[...] = jnp.full(m_ref.shape, NEG_INF, F32)
        l_ref[...] = jnp.zeros(l_ref.shape, F32)
        acc_ref[...] = jnp.zeros(acc_ref.shape, F32)

    lane = _iota((tq, LANES), 1)

    def step(masked):
        if masked:
            causal = _iota((tq, tk), 0) >= _iota((tq, tk), 1)
        scores = []
        for h in range(H_FOX):
            pr, hh = divmod(h, 2)
            cs = slice(pr * LANES, (pr + 1) * LANES)
            q2 = q_ref[:, cs]
            qm = jnp.where((lane >= 64 * hh) & (lane < 64 * hh + 64), q2, jnp.zeros_like(q2))
            scores.append(_nt(qm, k_ref[:, cs]))
        probs = []
        for h in range(H_FOX):
            s = scores[h] - ct_ref[h:h + 1, :]
            if masked:
                s = jnp.where(causal, s, NEG_INF)
            m_prev = m_ref[h]
            m_new = jnp.maximum(m_prev, jnp.max(s, axis=1, keepdims=True))
            alpha = jnp.exp(m_prev - m_new)
            p = jnp.exp(s - jnp.concatenate([m_new] * (tk // LANES), axis=1))
            l_ref[h] = alpha * l_ref[h] + jnp.sum(p, axis=1, keepdims=True)
            m_ref[h] = m_new
            probs.append((p.astype(BF16), alpha))
        for h in range(H_FOX):
            cs = slice((h // 2) * LANES, (h // 2 + 1) * LANES)
            p, alpha = probs[h]
            acc_ref[h] = alpha * acc_ref[h] + _mm(p, v_ref[:, cs])

    @pl.when(ki < qi)
    def _():
        step(False)

    @pl.when(ki == qi)
    def _():
        step(True)
        for pr in range(H_FOX // 2):
            o0 = acc_ref[2 * pr] / l_ref[2 * pr]
            o1 = acc_ref[2 * pr + 1] / l_ref[2 * pr + 1]
            o_ref[:, pr * LANES:(pr + 1) * LANES] = jnp.where(lane < 64, o0, o1)


def _causal_pairs(nq):
    pairs = [(qi, ki) for qi in range(nq) for ki in range(qi + 1)]
    return (jnp.asarray([p[0] for p in pairs], I32), jnp.asarray([p[1] for p in pairs], I32))


def _fox_prompt(pb3, ct, *, tq):
    b, t, _ = pb3.shape
    qt, kt = _causal_pairs(t // tq)
    grid_spec = pltpu.PrefetchScalarGridSpec(
        num_scalar_prefetch=2, grid=(b, qt.shape[0]),
        in_specs=[pl.BlockSpec((None, tq, 256), lambda bi, s, qt, kt: (bi, qt[s], C_FQ // 256)),
                  pl.BlockSpec((None, tq, 256), lambda bi, s, qt, kt: (bi, kt[s], C_FK // 256)),
                  pl.BlockSpec((None, tq, 256), lambda bi, s, qt, kt: (bi, kt[s], C_FV // 256)),
                  pl.BlockSpec((None, 8, tq), lambda bi, s, qt, kt: (bi, 0, kt[s]))],
        out_specs=pl.BlockSpec((None, tq, 256), lambda bi, s, qt, kt: (bi, qt[s], 0)),
        scratch_shapes=[pltpu.VMEM((H_FOX, tq, LANES), F32), pltpu.VMEM((H_FOX, tq, LANES), F32),
                        pltpu.VMEM((H_FOX, tq, LANES), F32)])
    return pl.pallas_call(
        functools.partial(_fox_body, tq=tq, tk=tq),
        grid_spec=grid_spec,
        out_shape=jax.ShapeDtypeStruct((b, t, 256), F32),
        compiler_params=pltpu.CompilerParams(
            dimension_semantics=("parallel", "arbitrary"), vmem_limit_bytes=VMEM_LIMIT),
        name="fox_prompt",
    )(qt, kt, pb3, pb3, pb3, ct)


def _cmp_body(x_ref, pe_ref, w1_ref, w2_ref, o_ref):
    xb = (x_ref[...] + pe_ref[...]).astype(BF16)
    h = _mm(xb, w1_ref[...])
    h = h * _sigmoid(h)
    o_ref[...] = _mm(h.astype(BF16), w2_ref[...])


def _compress_prompt(flat, pe_flat, w1, w2):
    _, m, f = flat.shape
    tm = min(m, 512)
    return pl.pallas_call(
        _cmp_body, grid=(2, m // tm),
        in_specs=[pl.BlockSpec((None, tm, f), lambda j, i: (j, i, 0)),
                  pl.BlockSpec((None, 1, f), lambda j, i: (j, 0, 0)),
                  pl.BlockSpec((None, f, 256), lambda j, i: (j, 0, 0)),
                  pl.BlockSpec((None, 256, HEAD_DIM), lambda j, i: (j, 0, 0))],
        out_specs=pl.BlockSpec((None, tm, HEAD_DIM), lambda j, i: (j, i, 0)),
        out_shape=jax.ShapeDtypeStruct((2, m, HEAD_DIM), F32),
        compiler_params=pltpu.CompilerParams(vmem_limit_bytes=VMEM_LIMIT),
        name="cmp_prompt",
    )(flat, pe_flat, w1, w2)


def _masked_softmax_cols(s, valid):
    s = jnp.where(valid, s, NEG_INF)
    e = jnp.where(valid, jnp.exp(s - jnp.max(s, axis=0, keepdims=True)), 0.0)
    return e / jnp.maximum(jnp.sum(e, axis=0, keepdims=True), 1e-30)


def _select_blocks(score, nb, topn):
    nbp, n = score.shape
    blk = _iota((nbp, n), 0)
    rank = jnp.zeros((nbp, n), F32)
    for m in range(nb):
        r = score[m:m + 1, :]
        rank = rank + jnp.where((r > score) | ((r == score) & (blk > m)), 1.0, 0.0)
    return jnp.where((rank < topn) & (score >= 0.0), 1.0, 0.0)


def _nsa_body(qt_ref, kt_ref, q_ref, ks_ref, vs_ref, kwc_ref, kwp_ref, vwc_ref, vwp_ref, kc_ref, vc_ref,
              aux_ref, o_ref, sel_ref, m_ref, l_ref, acc_ref, oc_ref, ow_ref, *, tq, nb, topn):
    qi = qt_ref[pl.program_id(1)]
    ki = kt_ref[pl.program_id(1)]
    q0 = qi * tq
    tk = tq
    lane = _iota((tq, LANES), 1)

    def qstack(g):
        q2 = q_ref[:, g * LANES:(g + 1) * LANES].astype(F32)
        q2r = pltpu.roll(q2, 64, 1)
        ing = (lane >= 64 * g) & (lane < 64 * g + 64)
        qa, qb = (q2, q2r) if g == 0 else (q2r, q2)
        return jnp.concatenate([jnp.where(ing, qa, 0.0), jnp.where(ing, qb, 0.0)],
                               axis=0).astype(BF16)

    def biased(raw, g, kd, valid):
        sa = jnp.where(valid, raw[:tq] + ALIBI_SLOPES[2 * g] * kd, NEG_INF)
        sb = jnp.where(valid, raw[tq:] + ALIBI_SLOPES[2 * g + 1] * kd, NEG_INF)
        return jnp.concatenate([sa, sb], axis=0)

    def lanes(x, n):
        return jnp.concatenate([x] * (n // LANES), axis=1)

    @pl.when(ki == 0)
    def _():
        m_ref[...] = jnp.full(m_ref.shape, NEG_INF, F32)
        l_ref[...] = jnp.zeros(l_ref.shape, F32)
        acc_ref[...] = jnp.zeros(acc_ref.shape, F32)
        dqk = _iota((tq, tk), 0) - _iota((tq, tk), 1)
        kd_cur = _iota((1, tk), 1).astype(F32)
        for g in range(G_NSA):
            qs = qstack(g)
            st = _nt(kc_ref[...], qs)
            blk = _iota((NBLK_PAD, 2 * tq), 0)
            col = _iota((NBLK_PAD, 2 * tq), 1)
            qpos = q0 + jnp.where(col >= tq, col - tq, col)
            slope = jnp.where(col < tq, ALIBI_SLOPES[2 * g], ALIBI_SLOPES[2 * g + 1])
            cend = blk * NSA_BLOCK + (NSA_BLOCK - 1)
            st = st + slope * (cend - q0).astype(F32)
            pt = _masked_softmax_cols(st, (cend <= qpos) & (blk < nb))
            oc_ref[g] = _mm(pt.T.astype(BF16), vc_ref[...])
            imp = pt[:, :tq] + pt[:, tq:]
            nbp = -(-nb // 8) * 8
            blk2 = _iota((nbp, tq), 0)
            cur = (q0 + _iota((nbp, tq), 1)) // NSA_BLOCK
            forced = (blk2 == 0) | (blk2 == cur) | (blk2 == cur - 1)
            score = jnp.where(blk2 > cur, -1.0, jnp.where(forced, FORCE_SCORE, imp[:nbp]))
            selt = _select_blocks(score, nb, topn)
            if nbp < NBLK_PAD:
                selt = jnp.concatenate([selt, jnp.zeros((NBLK_PAD - nbp, tq), F32)], axis=0)
            sel_ref[g] = selt.T.astype(BF16)
            s1 = biased(_nt(qs, kwc_ref[...]), g, kd_cur, dqk >= 0)
            s2 = biased(_nt(qs, kwp_ref[...]), g, kd_cur - tq, (dqk + tq <= NSA_WINDOW) & (qi > 0))
            mw = jnp.maximum(jnp.max(s1, axis=1, keepdims=True), jnp.max(s2, axis=1, keepdims=True))
            mw = jnp.broadcast_to(mw, (2 * tq, LANES))
            e1 = jnp.exp(s1 - lanes(mw, tk))
            e2 = jnp.exp(s2 - lanes(mw, tk))
            lw = jnp.sum(e1, axis=1, keepdims=True) + jnp.sum(e2, axis=1, keepdims=True)
            ow_ref[g] = (_mm(e1.astype(BF16), vwc_ref[...]) + _mm(e2.astype(BF16), vwp_ref[...])) \
                / jnp.maximum(lw, 1e-30)

    def sel_step(diag):
        expand = (_iota((NBLK_PAD, tk), 0) ==
                  ki * (tk // NSA_BLOCK) + _iota((NBLK_PAD, tk), 1) // NSA_BLOCK).astype(BF16)
        kd = (_iota((1, tk), 1) + (ki * tk - q0)).astype(F32)
        raws = [(_nt(qstack(g), ks_ref[...]), _mm(sel_ref[g], expand)) for g in range(G_NSA)]
        probs = []
        for g in range(G_NSA):
            raw, picked = raws[g]
            valid = picked > 0.5
            if diag:
                valid = valid & (_iota((tq, tk), 0) >= _iota((tq, tk), 1))
            s = biased(raw, g, kd, valid)
            m_prev = m_ref[g]
            m_new = jnp.maximum(m_prev, jnp.max(s, axis=1, keepdims=True))
            alpha = jnp.exp(m_prev - m_new)
            p = jnp.exp(s - lanes(m_new, tk))
            l_ref[g] = alpha * l_ref[g] + jnp.sum(p, axis=1, keepdims=True)
            m_ref[g] = m_new
            probs.append((p.astype(BF16), alpha))
        for g in range(G_NSA):
            p, alpha = probs[g]
            acc_ref[g] = alpha * acc_ref[g] + _mm(p, vs_ref[...])

    @pl.when(ki < qi)
    def _():
        sel_step(False)

    @pl.when(ki == qi)
    def _():
        sel_step(True)
        for g in range(G_NSA):
            o_s = acc_ref[g] / jnp.maximum(l_ref[g], 1e-30)
            o_c = oc_ref[g]
            o_w = ow_ref[g]
            outs = []
            for hh in range(2):
                h = 2 * g + hh
                rs = slice(hh * tq, (hh + 1) * tq)
                gl = LANE_GATE0 + 3 * h
                outs.append(aux_ref[:, gl:gl + 1] * o_c[rs] + aux_ref[:, gl + 1:gl + 2] * o_s[rs]
                            + aux_ref[:, gl + 2:gl + 3] * o_w[rs])
            oa, ob = outs
            if g == 0:
                ob = pltpu.roll(ob, 64, 1)
            else:
                oa = pltpu.roll(oa, 64, 1)
            o_ref[:, g * LANES:(g + 1) * LANES] = jnp.where(lane < 64, oa, ob)


def _nsa_prompt(pb3, kcmp, vcmp, aux3, *, tq, nb):
    b, t, _ = pb3.shape
    qt, kt = _causal_pairs(t // tq)
    ct = C_NKV // LANES
    cur = lambda c: (lambda bi, s, qt, kt: (bi, qt[s], c))
    prev = lambda c: (lambda bi, s, qt, kt: (bi, jnp.maximum(qt[s] - 1, 0), c))
    kvt = lambda c: (lambda bi, s, qt, kt: (bi, kt[s], c))
    cmp_spec = pl.BlockSpec((None, NBLK_PAD, LANES), lambda bi, s, qt, kt: (bi, 0, 0))
    tile = lambda f: pl.BlockSpec((None, tq, LANES), f)
    grid_spec = pltpu.PrefetchScalarGridSpec(
        num_scalar_prefetch=2, grid=(b, qt.shape[0]),
        in_specs=[pl.BlockSpec((None, tq, 256), lambda bi, s, qt, kt: (bi, qt[s], C_NQ // 256)),
                  tile(kvt(ct + 2)), tile(kvt(ct + 3)),
                  tile(cur(ct + 4)), tile(prev(ct + 4)), tile(cur(ct + 5)), tile(prev(ct + 5)),
                  cmp_spec, cmp_spec,
                  pl.BlockSpec((None, tq, LANES), lambda bi, s, qt, kt: (bi, qt[s], 0))],
        out_specs=pl.BlockSpec((None, tq, 256), lambda bi, s, qt, kt: (bi, qt[s], 0)),
        scratch_shapes=[pltpu.VMEM((G_NSA, tq, NBLK_PAD), BF16),
                        pltpu.VMEM((G_NSA, 2 * tq, LANES), F32), pltpu.VMEM((G_NSA, 2 * tq, LANES), F32),
                        pltpu.VMEM((G_NSA, 2 * tq, LANES), F32),
                        pltpu.VMEM((G_NSA, 2 * tq, LANES), F32),
                        pltpu.VMEM((G_NSA, 2 * tq, LANES), F32)])
    return pl.pallas_call(
        functools.partial(_nsa_body, tq=tq, nb=nb, topn=min(NSA_TOPN, nb)),
        grid_spec=grid_spec,
        out_shape=jax.ShapeDtypeStruct((b, t, 256), F32),
        compiler_params=pltpu.CompilerParams(
            dimension_semantics=("parallel", "arbitrary"), vmem_limit_bytes=VMEM_LIMIT),
        name="nsa_prompt",
    )(qt, kt, pb3, pb3, pb3, pb3, pb3, pb3, pb3, kcmp, vcmp, aux3)


def _mem_attend(mq_ref, mk_ref, mv_ref, rows):
    lane = _iota((rows, LANES), 1)
    pairs = []
    for pr in range(H_MEM // 2):
        cs = slice(pr * LANES, (pr + 1) * LANES)
        q2 = mq_ref[:, cs]
        halves = []
        for hh in range(2):
            qm = jnp.where((lane >= 64 * hh) & (lane < 64 * hh + 64), q2, jnp.zeros_like(q2))
            s = _nt(qm, mk_ref[:, cs])
            e = jnp.exp(s - jnp.max(s, axis=1, keepdims=True))
            p = e / jnp.sum(e, axis=1, keepdims=True)
            halves.append(_mm(p.astype(BF16), mv_ref[:, cs]))
        pairs.append(jnp.where(lane < 64, halves[0], halves[1]))
    return pairs


def _mix_out(x, parts, z_of, wout_ref, fg_ref, final):
    y = x
    for c, part in enumerate(parts):
        z = z_of(c)
        gated = (part * (z * _sigmoid(z))).astype(BF16)
        y = y + _mm(gated, wout_ref[c * LANES:(c + 1) * LANES, :])
    if final:
        y = y * lax.rsqrt(jnp.mean(y * y, axis=-1, keepdims=True) + RMS_EPS) * fg_ref[...]
    return y


def _pool_window_lane(shape):
    grp = _iota(shape, 1) // (D_GROUP // len(POOL_WINDOWS))
    wl = jnp.where(grp == 0, POOL_WINDOWS[0], jnp.where(grp == 1, POOL_WINDOWS[1],
                   jnp.where(grp == 2, POOL_WINDOWS[2], POOL_WINDOWS[3])))
    return grp, wl


HALO = 32


def _post_body(oa_ref, ob_ref, pz_ref, halo_ref, mq_ref, mk_ref, mv_ref, x_ref, wout_ref, pw_ref,
               ps_ref, fg_ref, y_ref, s0, s1, s2, s3, *, tm, final):
    i = pl.program_id(1)
    u = pz_ref[:, 0:256]
    s0[0:HALO, :] = jnp.where(i > 0, halo_ref[...], 0.0)
    s0[HALO:HALO + tm, :] = u
    n = tm + HALO
    s1[8:n, :] = s0[8:n, :] + s0[7:n - 1, :]
    s2[16:n, :] = s1[16:n, :] + s1[14:n - 2, :]
    s3[24:n, :] = s2[24:n, :] + s2[20:n - 4, :]
    a16 = s3[HALO:n, :] + s3[HALO - 8:n - 8, :]
    grp, wl = _pool_window_lane((tm, 256))
    tsum = jnp.where(grp == 0, s1[HALO:n, :], jnp.where(grp == 1, s2[HALO:n, :],
                     jnp.where(grp == 2, s3[HALO:n, :], a16)))
    pos = i * tm + _iota((tm, 256), 0)
    cnt = jnp.minimum(pos + 1, wl).astype(F32)
    o_pool = _mm((tsum / cnt - u).astype(BF16), pw_ref[...]) * ps_ref[...]
    o_mem = _mem_attend(mq_ref, mk_ref, mv_ref, tm)
    parts = [oa_ref[:, 0:LANES], oa_ref[:, LANES:256], ob_ref[:, 0:LANES], ob_ref[:, LANES:256],
             o_pool[:, 0:LANES], o_pool[:, LANES:256], o_mem[0], o_mem[1]]
    z_of = lambda c: pz_ref[:, 256 + c * LANES:256 + (c + 1) * LANES]
    y_ref[...] = _mix_out(x_ref[...], parts, z_of, wout_ref, fg_ref, final)


def _post_prompt(oa, ob, pz3, pb3, memb, x3, wout, pw_bd, ps, fg, *, tm, final):
    b, t, d = x3.shape
    nt = t // tm
    row = lambda bi, i: (bi, i, 0)
    const = lambda bi, i: (0, 0)
    return pl.pallas_call(
        functools.partial(_post_body, tm=tm, final=final),
        grid=(b, nt),
        in_specs=[pl.BlockSpec((None, tm, 256), row), pl.BlockSpec((None, tm, 256), row),
                  pl.BlockSpec((None, tm, 1280), row),
                  pl.BlockSpec((None, HALO, 256),
                               lambda bi, i: (bi, jnp.maximum(i * (tm // HALO) - 1, 0), 0)),
                  pl.BlockSpec((None, tm, 256), lambda bi, i: (bi, i, C_MQ // 256)),
                  pl.BlockSpec((None, 256, 256), lambda bi, i: (bi, 0, 0)),
                  pl.BlockSpec((None, 256, 256), lambda bi, i: (bi, 0, 1)),
                  pl.BlockSpec((None, tm, d), row),
                  pl.BlockSpec((d, d), const), pl.BlockSpec((256, 256), const),
                  pl.BlockSpec((1, 256), const), pl.BlockSpec((1, d), const)],
        out_specs=pl.BlockSpec((None, tm, d), row),
        out_shape=jax.ShapeDtypeStruct((b, t, d), F32),
        scratch_shapes=[pltpu.VMEM((tm + HALO, 256), F32)] * 4,
        compiler_params=pltpu.CompilerParams(dimension_semantics=("parallel", "arbitrary"),
                                             vmem_limit_bytes=VMEM_LIMIT),
        name="post_prompt",
    )(oa, ob, pz3, pz3, pb3, memb, memb, x3, wout, pw_bd, ps, fg)


CMP_PITCH = 72
DQ = 4


def _cmp_sample_body(pt_ref, nsa_hbm, pe_ref, w1_ref, w2a_ref, w2b_ref, o_ref, xbuf, acc0, acc1, sem,
                     *, pages, layer):
    j = pl.program_id(0)
    i = pl.program_id(1)
    ni = pl.num_programs(1)
    s = j * ni + i
    m = pages * G_NSA

    def chunk_copy(jj, ii, slot, p, g):
        page = pt_ref[ii * pages + p]
        return pltpu.make_async_copy(
            nsa_hbm.at[layer, page, pl.ds(jj * LANES + g * HEAD_DIM, HEAD_DIM), :],
            xbuf.at[slot, pl.ds((p * G_NSA + g) * CMP_PITCH, HEAD_DIM), :], sem.at[slot])

    def for_chunks(jj, ii, slot, fn):
        def body(p, carry):
            for g in range(G_NSA):
                fn(chunk_copy(jj, ii, slot, p, g))
            return carry
        lax.fori_loop(0, pages, body, 0)

    @pl.when(s == 0)
    def _():
        for_chunks(0, 0, 0, lambda c: c.start())

    @pl.when(s + 1 < 2 * ni)
    def _():
        wrap = i + 1 == ni
        for_chunks(jnp.where(wrap, j + 1, j), jnp.where(wrap, 0, i + 1), (s + 1) % 2,
                   lambda c: c.start())

    slot = s % 2
    for_chunks(j, i, slot, lambda c: c.wait())

    lo = _iota((m, LANES), 1) < HEAD_DIM
    acc0[...] = jnp.zeros(acc0.shape, F32)
    acc1[...] = jnp.zeros(acc1.shape, F32)
    for dq in range(HEAD_DIM // DQ):
        xs = [xbuf[slot, pl.ds(DQ * dq + k, m, stride=CMP_PITCH), :] + pe_ref[DQ * dq + k:DQ * dq + k + 1, :]
              for k in range(DQ)]
        xr = [pltpu.roll(x, HEAD_DIM, 1) for x in xs]
        a0 = jnp.concatenate([jnp.where(lo, xs[0], xr[1]), jnp.where(lo, xs[2], xr[3])], axis=1)
        a1 = jnp.concatenate([jnp.where(lo, xr[0], xs[1]), jnp.where(lo, xr[2], xs[3])], axis=1)
        w = w1_ref[dq]
        acc0[...] += _mm(a0.astype(BF16), w)
        acc1[...] += _mm(a1.astype(BF16), w)
    h0 = acc0[...]
    h1 = acc1[...]
    h0 = (h0 * _sigmoid(h0)).astype(BF16)
    h1 = (h1 * _sigmoid(h1)).astype(BF16)
    o_ref[...] = _mm(h0, w2a_ref[...]) + _mm(h1, w2b_ref[...])


def _compress_sample(page_flat, nsa_t, pe_t, w1q, w2a, w2b, *, pages, layer):
    npages = page_flat.shape[0]
    nsteps = npages // pages
    m = pages * G_NSA
    by_j = lambda blk: pl.BlockSpec((None,) + blk, lambda j, i, pt: (j,) + (0,) * len(blk))
    grid_spec = pltpu.PrefetchScalarGridSpec(
        num_scalar_prefetch=1, grid=(2, nsteps),
        in_specs=[pl.BlockSpec(memory_space=pl.ANY), by_j((HEAD_DIM, LANES)),
                  by_j((HEAD_DIM // DQ, 256, 256)), by_j((256, LANES)), by_j((256, LANES))],
        out_specs=pl.BlockSpec((None, m, LANES), lambda j, i, pt: (j, i, 0)),
        scratch_shapes=[pltpu.VMEM((2, m * CMP_PITCH, LANES), F32), pltpu.VMEM((m, 256), F32),
                        pltpu.VMEM((m, 256), F32), pltpu.SemaphoreType.DMA((2,))])
    return pl.pallas_call(
        functools.partial(_cmp_sample_body, pages=pages, layer=layer),
        grid_spec=grid_spec,
        out_shape=jax.ShapeDtypeStruct((2, npages * G_NSA, LANES), F32),
        compiler_params=pltpu.CompilerParams(dimension_semantics=("arbitrary", "arbitrary"),
                                             vmem_limit_bytes=VMEM_LIMIT),
        name="cmp_sample",
    )(page_flat, nsa_t, pe_t, w1q, w2a, w2b)


HROWS = 16


def _rows_from_lanes(vec_row, pick):
    return jnp.sum(jnp.where(pick, vec_row, 0.0), axis=1, keepdims=True)


def _bf16r(x):
    return x.astype(BF16).astype(F32)


def _att_sample_body(pt_ref, fox_hbm, lf_hbm, nsa_hbm, pb_ref, sf_ref, aux_ref, win_ref, mem_ref,
                     kc_ref, vc_ref, exp_ref, o_ref, wout_ref, fbuf, lbuf, sbuf, sem,
                     *, npg, past, layer):
    b = pl.program_id(0)
    nb = pl.num_programs(0)

    def copies(seq, slot):
        out = []
        for i in range(npg):
            page = pt_ref[seq * npg + i]
            lanes = pl.ds(i * PAGE_SIZE, PAGE_SIZE)
            out.append(pltpu.make_async_copy(
                fox_hbm.at[layer, page], fbuf.at[slot, :, lanes], sem.at[0, slot]))
            out.append(pltpu.make_async_copy(
                lf_hbm.at[layer, page], lbuf.at[slot, pl.ds(i * 8, 8), :], sem.at[1, slot]))
            out.append(pltpu.make_async_copy(
                nsa_hbm.at[layer, page, pl.ds(256, 256), :], sbuf.at[slot, :, lanes], sem.at[2, slot]))
        return out

    @pl.when(b == 0)
    def _():
        for c in copies(0, 0):
            c.start()

    @pl.when(b + 1 < nb)
    def _():
        for c in copies(b + 1, (b + 1) % 2):
            c.start()

    slot = b % 2
    for c in copies(b, slot):
        c.wait()

    row8 = _iota((HROWS, LANES), 0)
    lane8 = _iota((HROWS, LANES), 1)
    aux = aux_ref[...]
    pbr = pb_ref[...].astype(F32)
    sfr = sf_ref[...]

    def head_rows(q256):
        r = _iota((HROWS, 256), 0)
        l = _iota((HROWS, 256), 1)
        return jnp.where(l // HEAD_DIM == r, q256, 0.0).astype(BF16)

    def pick_heads(o8):
        r = _iota((HROWS, 256), 0)
        l = _iota((HROWS, 256), 1)
        return jnp.sum(jnp.where(l // HEAD_DIM == r, o8, 0.0), axis=0, keepdims=True)

    kt = fbuf[slot, 0:256, :].astype(BF16)
    vt = fbuf[slot, 256:512, :].astype(BF16)
    qf = head_rows(pbr[:, C_FQ:C_FK])
    s = _mm(qf, kt)
    lf2 = lbuf[slot]
    r2 = _iota((npg * 8, npg * 8), 0)
    c2 = _iota((npg * 8, npg * 8), 1)
    upper = (_iota((LANES, LANES), 0) > _iota((LANES, LANES), 1)).astype(BF16)
    within = _exact_mm_rhs(lf2, upper)
    tot = jnp.sum(lf2, axis=1, keepdims=True)
    later_pages = ((c2 % 8 == r2 % 8) & (c2 // 8 > r2 // 8)).astype(BF16)
    later = _exact_mm(later_pages, jnp.broadcast_to(tot, (npg * 8, LANES)))
    rr = _iota((npg * 8, LANES), 0)
    ll = _iota((npg * 8, LANES), 1)
    lf_new = _rows_from_lanes(aux, ll == rr % 8)
    bias = within + later + lf_new
    zpad = jnp.zeros((HROWS - 8, LANES), F32)
    s = jnp.concatenate(
        [s[:, i * PAGE_SIZE:(i + 1) * PAGE_SIZE] + jnp.concatenate([bias[i * 8:(i + 1) * 8, :], zpad], axis=0)
         for i in range(npg)], axis=1)
    knew = _bf16r(sfr[:, 0:256])
    vnew = _bf16r(sfr[:, 256:512])
    s_new = jnp.sum(qf.astype(F32) * knew, axis=1, keepdims=True)
    m = jnp.maximum(jnp.max(s, axis=1, keepdims=True), s_new)
    e = jnp.exp(s - m)
    e_new = jnp.exp(s_new - m)
    den = jnp.sum(e, axis=1, keepdims=True) + e_new
    o8 = (_nt(e.astype(BF16), vt) + _bf16r(e_new) * vnew) / den
    o_ref[:, 0:256] = pick_heads(o8)

    nq = pbr[:, C_NQ:C_NKV]
    t0 = jnp.broadcast_to(nq[:, 0:LANES], (HROWS, LANES))
    t1 = jnp.broadcast_to(nq[:, LANES:256], (HROWS, LANES))
    t0r, t1r = pltpu.roll(t0, 64, 1), pltpu.roll(t1, 64, 1)
    lo = lane8 < 64
    hrows = [jnp.where(lo, t0, 0.0), jnp.where(lo, t0r, 0.0),
             jnp.where(lo, 0.0, t1r), jnp.where(lo, 0.0, t1)]
    qn = jnp.zeros((HROWS, LANES), F32)
    for h in range(H_NSA):
        qn = jnp.where(row8 == h, hrows[h], qn)
    qn_b = qn.astype(BF16)
    slope8 = jnp.zeros((HROWS, 1), F32)
    r81 = _iota((HROWS, 1), 0)
    for h in range(H_NSA):
        slope8 = jnp.where(r81 == h, ALIBI_SLOPES[h], slope8)
    nkv_new = sfr[:, 512:1280]
    nblk = past // NSA_BLOCK

    def new_score(k128):
        return jnp.sum(qn_b.astype(F32) * _bf16r(k128), axis=1, keepdims=True)

    sc = _nt(qn_b, kc_ref[...].astype(BF16))
    cend = _iota((HROWS, nblk), 1) * NSA_BLOCK + (NSA_BLOCK - 1)
    sc = sc - slope8 * (past - cend).astype(F32)
    ec = jnp.exp(sc - jnp.max(sc, axis=1, keepdims=True))
    pc = ec / jnp.maximum(jnp.sum(ec, axis=1, keepdims=True), 1e-30)
    o_c = _mm(pc.astype(BF16), vc_ref[...].astype(BF16))
    imp = jnp.concatenate([pc[2 * g:2 * g + 1] + pc[2 * g + 1:2 * g + 2] for g in range(G_NSA)]
                          + [jnp.zeros((HROWS - G_NSA, nblk), F32)], axis=0)
    if nblk < LANES:
        imp = jnp.concatenate([imp, jnp.zeros((HROWS, LANES - nblk), F32)], axis=1)
    forced = (lane8 == 0) | (lane8 == nblk) | (lane8 == nblk - 1)
    score = jnp.where(lane8 > nblk, -1.0, jnp.where(forced, FORCE_SCORE, imp))
    rank = jnp.zeros((HROWS, LANES), F32)
    for mblk in range(nblk + 1):
        r = score[:, mblk:mblk + 1]
        rank = rank + jnp.where((r > score) | ((r == score) & (lane8 > mblk)), 1.0, 0.0)
    sel_g = jnp.where((rank < min(NSA_TOPN, nblk + 1)) & (score >= 0.0), 1.0, 0.0)
    sel_h = jnp.zeros((HROWS, LANES), F32)
    for h in range(H_NSA):
        sel_h = jnp.where(row8 == h, sel_g[h // 2:h // 2 + 1, :], sel_h)
    valid = _mm(sel_h.astype(BF16), exp_ref[...]) > 0.5
    kst = sbuf[slot, 0:LANES, :].astype(BF16)
    vst = sbuf[slot, LANES:256, :].astype(BF16)
    ss = _mm(qn_b, kst) - slope8 * (past - _iota((HROWS, past), 1)).astype(F32)
    ss = jnp.where(valid, ss, NEG_INF)
    ss_new = new_score(nkv_new[:, 256:384])
    ms = jnp.maximum(jnp.max(ss, axis=1, keepdims=True), ss_new)
    es = jnp.where(valid, jnp.exp(ss - ms), 0.0)
    es_new = jnp.exp(ss_new - ms)
    o_s = (_nt(es.astype(BF16), vst) + _bf16r(es_new) * _bf16r(nkv_new[:, 384:512])) \
        / (jnp.sum(es, axis=1, keepdims=True) + es_new)
    wl_ = win_ref.shape[1]
    win = win_ref[...]
    sw = _mm(qn_b, win[0:LANES].astype(BF16)) - slope8 * (wl_ - _iota((HROWS, wl_), 1)).astype(F32)
    sw_new = new_score(nkv_new[:, 512:640])
    mw = jnp.maximum(jnp.max(sw, axis=1, keepdims=True), sw_new)
    ew = jnp.exp(sw - mw)
    ew_new = jnp.exp(sw_new - mw)
    o_w = (_nt(ew.astype(BF16), win[LANES:256].astype(BF16))
           + _bf16r(ew_new) * _bf16r(nkv_new[:, 640:768])) / (jnp.sum(ew, axis=1, keepdims=True) + ew_new)
    rw = _iota((256, 256), 0)
    lw_ = _iota((256, 256), 1)
    new_col = jnp.sum(jnp.where(rw == lw_, nkv_new[:, 512:768], 0.0), axis=1, keepdims=True)
    wout_ref[...] = jnp.where(_iota((256, wl_), 1) == wl_ - 1, new_col, pltpu.roll(win, wl_ - 1, 1))
    gates = [_rows_from_lanes(aux, lane8 == LANE_GATE0 + 3 * row8 + c) for c in range(3)]
    o_n = gates[0] * o_c + gates[1] * o_s + gates[2] * o_w
    o_nr = pltpu.roll(o_n, 64, 1)
    lo1 = _iota((1, LANES), 1) < 64
    o_ref[:, 256:384] = jnp.where(lo1, o_n[0:1], o_nr[1:2])
    o_ref[:, 384:512] = jnp.where(lo1, o_nr[2:3], o_n[3:4])

    qm = head_rows(pbr[:, C_MQ:C_PU])
    sm = _mm(qm, mem_ref[0:256, :].astype(BF16))
    em = jnp.exp(sm - jnp.max(sm, axis=1, keepdims=True))
    pm = em / jnp.sum(em, axis=1, keepdims=True)
    o_ref[:, 512:768] = pick_heads(_nt(pm.astype(BF16), mem_ref[256:512, :].astype(BF16)))


def _exact_mm_rhs(x, m01):
    p1, p2, p3 = _split3(x)
    return _mm(p1, m01) + _mm(p2, m01) + _mm(p3, m01)


def _attend_sample(page_flat, fox_t, lf_t, nsa_t, pb3, sf3, aux3, win_t, mem_t, kcmp, vcmp, expand,
                   *, npg, layer):
    db = pb3.shape[0]
    past = npg * PAGE_SIZE
    per_seq = lambda blk: pl.BlockSpec((None,) + blk, lambda b, pt: (b, 0, 0))
    per_seq_layer = lambda blk: pl.BlockSpec((None, None) + blk, lambda b, pt: (layer, b, 0, 0))
    grid_spec = pltpu.PrefetchScalarGridSpec(
        num_scalar_prefetch=1, grid=(db,),
        in_specs=[pl.BlockSpec(memory_space=pl.ANY), pl.BlockSpec(memory_space=pl.ANY),
                  pl.BlockSpec(memory_space=pl.ANY),
                  per_seq((1, N_PB)), per_seq((1, 1280)), per_seq((1, LANES)),
                  per_seq_layer(win_t.shape[2:]), per_seq_layer(mem_t.shape[2:]),
                  per_seq(kcmp.shape[1:]), per_seq(vcmp.shape[1:]),
                  pl.BlockSpec(expand.shape, lambda b, pt: (0, 0))],
        out_specs=[per_seq((1, 768)), per_seq(win_t.shape[2:])],
        scratch_shapes=[pltpu.VMEM((2, 512, past), F32), pltpu.VMEM((2, npg * 8, LANES), F32),
                        pltpu.VMEM((2, 256, past), F32), pltpu.SemaphoreType.DMA((3, 2))])
    return pl.pallas_call(
        functools.partial(_att_sample_body, npg=npg, past=past, layer=layer),
        grid_spec=grid_spec,
        out_shape=[jax.ShapeDtypeStruct((db, 1, 768), F32),
                   jax.ShapeDtypeStruct((db,) + win_t.shape[2:], F32)],
        compiler_params=pltpu.CompilerParams(dimension_semantics=("arbitrary",),
                                             vmem_limit_bytes=VMEM_LIMIT),
        name="att_sample",
    )(page_flat, fox_t, lf_t, nsa_t, pb3, sf3, aux3, win_t, mem_t, kcmp, vcmp, expand)


def _post_sample_body(oatt_ref, pz_ref, sp_ref, x_ref, wout_ref, pw_ref, ps_ref, fg_ref, y_ref,
                      spo_ref, *, rows, past, final):
    u = pz_ref[:, 0:256]
    grp, wl = _pool_window_lane((rows, 256))
    tsum = u
    for j in range(1, POOL_BUF + 1):
        tsum = tsum + jnp.where(wl > j, sp_ref[POOL_BUF - j], 0.0)
    for j in range(POOL_BUF - 1):
        spo_ref[j] = sp_ref[j + 1]
    spo_ref[POOL_BUF - 1] = u
    cnt = jnp.minimum(past + 1, wl).astype(F32)
    o_pool = _mm((tsum / cnt - u).astype(BF16), pw_ref[...]) * ps_ref[...]
    parts = [oatt_ref[:, c * LANES:(c + 1) * LANES] for c in range(4)]
    parts += [o_pool[:, 0:LANES], o_pool[:, LANES:256]]
    parts += [oatt_ref[:, 512:640], oatt_ref[:, 640:768]]
    z_of = lambda c: pz_ref[:, 256 + c * LANES:256 + (c + 1) * LANES]
    y_ref[...] = _mix_out(x_ref[...], parts, z_of, wout_ref, fg_ref, final)


def _post_sample(oatt, pz, pool_t, x2d, wout, pw_bd, ps, fg, *, past, final, layer):
    rows, d = x2d.shape
    full = lambda a: pl.BlockSpec(a.shape, lambda i: (0,) * a.ndim)
    return pl.pallas_call(
        functools.partial(_post_sample_body, rows=rows, past=past, final=final),
        grid=(1,),
        in_specs=[full(oatt), full(pz),
                  pl.BlockSpec((None, POOL_BUF, rows, 256), lambda i: (layer, 0, 0, 0)),
                  full(x2d), full(wout), full(pw_bd), full(ps), full(fg)],
        out_specs=[pl.BlockSpec((rows, d), lambda i: (0, 0)),
                   pl.BlockSpec((POOL_BUF, rows, 256), lambda i: (0, 0, 0))],
        out_shape=[jax.ShapeDtypeStruct((rows, d), F32),
                   jax.ShapeDtypeStruct((POOL_BUF, rows, 256), F32)],
        compiler_params=pltpu.CompilerParams(vmem_limit_bytes=VMEM_LIMIT),
        name="post_sample",
    )(oatt, pz, pool_t, x2d, wout, pw_bd, ps, fg)


def _prep_layer_weights(norm_g, w_in, b_fgt, nsa_pe, nsa_w1, nsa_w2, pool_w, pool_scale,
                        w_mem_kv, w_out):
    d = w_in.shape[0]
    o = 0
    segs = {}
    for name, width in (("fqkv", 768), ("ff", H_FOX), ("nq", 256), ("nkv", 768),
                        ("ng", 3 * H_NSA), ("pu", 256), ("mq", 256), ("z", d)):
        segs[name] = w_in[:, o:o + width]
        o += width
    pad = jnp.zeros((d, LANES - H_FOX - 3 * H_NSA), w_in.dtype)
    wp = jnp.concatenate([segs["fqkv"], segs["nq"], segs["nkv"], segs["mq"], segs["pu"],
                          segs["z"], segs["ff"], segs["ng"], pad], axis=1).astype(BF16)
    bvec = jnp.zeros((1, LANES), F32).at[0, 0:H_FOX].set(b_fgt.astype(F32))
    gw = D_GROUP // len(POOL_WINDOWS)
    pw_bd = jnp.zeros((D_GROUP, D_GROUP), F32)
    for g in range(len(POOL_WINDOWS)):
        pw_bd = pw_bd.at[g * gw:(g + 1) * gw, g * gw:(g + 1) * gw].set(pool_w[g])
    return dict(g=norm_g.reshape(1, d), wp=wp, bvec=bvec,
                pe_flat=nsa_pe.reshape(2, 1, NSA_BLOCK * HEAD_DIM),
                w1=nsa_w1.astype(BF16), w2=nsa_w2.astype(BF16),
                w1q=jnp.transpose(nsa_w1.reshape(2, NSA_BLOCK, HEAD_DIM, -1), (0, 2, 1, 3)).reshape(
                    2, HEAD_DIM // DQ, DQ * NSA_BLOCK, -1).astype(BF16),
                pe_t=jnp.concatenate([jnp.transpose(nsa_pe, (0, 2, 1))] * 2, axis=2).astype(F32),
                w2a=jnp.pad(nsa_w2, ((0, 0), (0, 0), (0, HEAD_DIM))).astype(BF16),
                w2b=jnp.pad(nsa_w2, ((0, 0), (0, 0), (HEAD_DIM, 0))).astype(BF16),
                pw_bd=pw_bd.astype(BF16), ps=pool_scale.reshape(1, D_GROUP).astype(F32),
                wmkv=w_mem_kv.astype(BF16), wout=w_out.astype(BF16))


def _layer_prompt(x, mem, lw, fg, final, tile):
    b, t, d = x.shape
    n = b * t
    nb = t // NSA_BLOCK
    pb, pz, aux, sfc, ct, lft, fkvt, nkvt, nwt = _project(
        x.reshape(n, d), lw["g"], lw["wp"], lw["bvec"], tm=tile, tiles_per_seq=t // tile, prompt=True)
    pb3 = pb.reshape(b, t, N_PB)
    o_a = _fox_prompt(pb3, ct, tq=tile)
    kvc = sfc.reshape(b, nb, NSA_BLOCK, 2, G_NSA, HEAD_DIM)
    flat = jnp.transpose(kvc, (3, 0, 1, 4, 2, 5)).reshape(2, b * nb * G_NSA, NSA_BLOCK * HEAD_DIM)
    cmp = _compress_prompt(flat, lw["pe_flat"], lw["w1"], lw["w2"]).reshape(2, b, nb, LANES)
    cmp = jnp.pad(cmp, ((0, 0), (0, 0), (0, NBLK_PAD - nb), (0, 0))).astype(BF16)
    o_b = _nsa_prompt(pb3, cmp[0], cmp[1], aux.reshape(b, t, LANES), tq=tile, nb=nb)
    n_mem = mem.shape[1]
    mem_t, memb = _mem_project(mem, lw["wmkv"])
    pz3 = pz.reshape(b, t, 1280)
    y = _post_prompt(o_a, o_b, pz3, pb3, memb, x, lw["wout"], lw["pw_bd"], lw["ps"], fg,
                     tm=tile, final=final)
    w_keep = min(NSA_WINDOW, t)
    back = (0, 4, 1, 2, 3)
    state = (jnp.transpose(fkvt.reshape(b, 2, H_FOX, HEAD_DIM, t), back),
             jnp.transpose(lft[:, 0:H_FOX, :], (0, 2, 1)),
             jnp.transpose(nkvt.reshape(b, 4, G_NSA, HEAD_DIM, t), back),
             jnp.transpose(nwt[:, :, t - w_keep:].reshape(b, 2, G_NSA, HEAD_DIM, w_keep), back),
             pz3[:, t - POOL_BUF:, 0:256],
             jnp.transpose(mem_t.reshape(b, 2, H_MEM, HEAD_DIM, n_mem), back))
    return y, state


def _native_views(cache_fox_kv, cache_fox_lf, cache_nsa_kv, state_nsa_win, state_pool, cache_mem_kv):
    dp, n_phys = cache_fox_kv.shape[:2]
    db = state_nsa_win.shape[1]
    to_t = (0, 1, 3, 4, 5, 2)
    fox_t = jnp.transpose(cache_fox_kv, to_t).reshape(dp, n_phys, 512, PAGE_SIZE)
    lf_t = jnp.pad(jnp.transpose(cache_fox_lf.astype(F32), (0, 1, 3, 2)),
                   ((0, 0), (0, 0), (0, 8 - H_FOX), (0, 0)))
    nsa_t = jnp.transpose(cache_nsa_kv, to_t).reshape(dp, n_phys, 512, PAGE_SIZE)
    win_t = jnp.transpose(state_nsa_win, to_t).reshape(dp, db, 256, state_nsa_win.shape[2])
    mem_t = jnp.transpose(cache_mem_kv, to_t).reshape(dp, db, 512, cache_mem_kv.shape[2])
    pool_t = jnp.transpose(state_pool, (0, 2, 1, 3))
    return dict(fox_t=fox_t, lf_t=lf_t, nsa_t=nsa_t, win_t=win_t, mem_t=mem_t, pool_t=pool_t)


def _layer_sample(x, nv, page_table, lw, fg, final, layer):
    db, ds, d = x.shape
    npg = page_table.shape[1]
    past = npg * PAGE_SIZE
    nblk = past // NSA_BLOCK
    pb, pz, aux, sf, sft, auxt = _project(x.reshape(db, d), lw["g"], lw["wp"], lw["bvec"],
                                          tm=db, tiles_per_seq=1, prompt=False)
    page_flat = page_table.reshape(db * npg).astype(I32)
    cmp = _compress_sample(page_flat, nv["nsa_t"], lw["pe_t"], lw["w1q"], lw["w2a"], lw["w2b"],
                           pages=min(db * npg, 128), layer=layer)
    cmp = jnp.transpose(cmp.reshape(2, db, npg, G_NSA, PAGE_SIZE // NSA_BLOCK, HEAD_DIM),
                        (0, 1, 2, 4, 3, 5)).reshape(2, db, nblk, LANES)
    kcmp, vcmp = cmp[0], cmp[1]
    expand = (np.arange(LANES)[:, None] == (np.arange(past)[None, :] // NSA_BLOCK)).astype(np.float32)
    oatt, win_new = _attend_sample(
        page_flat, nv["fox_t"], nv["lf_t"], nv["nsa_t"], pb.reshape(db, 1, N_PB),
        sf.reshape(db, 1, 1280), aux.reshape(db, 1, LANES), nv["win_t"], nv["mem_t"], kcmp, vcmp,
        jnp.asarray(expand, BF16), npg=npg, layer=layer)
    y, pool_new = _post_sample(oatt.reshape(db, 768), pz, nv["pool_t"], x.reshape(db, d), lw["wout"],
                               lw["pw_bd"], lw["ps"], fg, past=past, final=final, layer=layer)
    w_buf = win_new.shape[-1]
    seq_first = (3, 0, 1, 2)
    state = (jnp.transpose(sft[0:512].reshape(2, H_FOX, HEAD_DIM, db), seq_first)[:, None],
             jnp.transpose(auxt[0:H_FOX])[:, None, :],
             jnp.transpose(sft[512:1024].reshape(4, G_NSA, HEAD_DIM, db), seq_first)[:, None],
             jnp.transpose(win_new.reshape(db, 2, G_NSA, HEAD_DIM, w_buf), (0, 4, 1, 2, 3)),
             jnp.transpose(pool_new, (1, 0, 2)))
    return y.reshape(db, ds, d), state


def kernel(x_prompt, x_sample, cache_fox_kv, cache_fox_lf, cache_nsa_kv, state_nsa_win, state_pool,
           cache_mem_kv, page_table, mem_prompt, norm_g, w_in, b_fgt, nsa_pe, nsa_w1, nsa_w2, pool_w,
           pool_scale, w_mem_kv, w_out, final_g):
    depth = norm_g.shape[0]
    t = x_prompt.shape[1]
    assert x_sample.shape[1] == 1 and t % NSA_WINDOW == 0 and t // NSA_BLOCK <= NBLK_PAD
    hp, hs = x_prompt, x_sample
    fg = final_g.reshape(1, -1).astype(F32)
    nv = _native_views(cache_fox_kv, cache_fox_lf, cache_nsa_kv, state_nsa_win, state_pool,
                       cache_mem_kv)
    sp = [[] for _ in range(6)]
    ss = [[] for _ in range(5)]
    for l in range(depth):
        lw = _prep_layer_weights(norm_g[l], w_in[l], b_fgt[l], nsa_pe[l], nsa_w1[l], nsa_w2[l],
                                 pool_w[l], pool_scale[l], w_mem_kv[l], w_out[l])
        final = l == depth - 1
        hp, st_p = _layer_prompt(hp, mem_prompt, lw, fg, final, NSA_WINDOW)
        hs, st_s = _layer_sample(hs, nv, page_table, lw, fg, final, l)
        for acc, s in zip(sp, st_p):
            acc.append(s)
        for acc, s in zip(ss, st_s):
            acc.append(s)
    return (hp, hs) + tuple(jnp.stack(a) for a in sp) + tuple(jnp.stack(a) for a in ss)
```

```python
import functools

import numpy as np
import jax
import jax.numpy as jnp
from jax import lax
from jax.experimental import pallas as pl
from jax.experimental.pallas import tpu as pltpu

F32 = jnp.float32
BF16 = jnp.bfloat16
I32 = jnp.int32

HEAD_DIM = 64
H_FOX = 4
H_NSA = 4
G_NSA = 2
H_MEM = 4
D_GROUP = 256
POOL_WINDOWS = (2, 4, 8, 16)
POOL_BUF = 15
NSA_BLOCK = 64
NSA_TOPN = 16
NSA_WINDOW = 512
PAGE_SIZE = 128
RMS_EPS = 1e-6
NEG_INF = -1e30
FORCE_SCORE = 1e4
SCALE = HEAD_DIM ** -0.5
ALIBI_SLOPES = tuple(2.0 ** (-8.0 * (h + 1) / H_NSA) for h in range(H_NSA))

LANES = 128
NBLK_PAD = 128
VMEM_LIMIT = 56 * 1024 * 1024

C_FQ, C_FK, C_FV, C_NQ, C_NKV, C_MQ, C_PU, C_Z, C_SM, C_END = (
    0, 256, 512, 768, 1024, 1792, 2048, 2304, 3328, 3456)
N_PB = C_PU
LANE_GATE0 = 4


def _nt(a, b):
    return lax.dot_general(a, b, (((1,), (1,)), ((), ())), preferred_element_type=F32)


def _mm(a, b):
    return jnp.dot(a, b, preferred_element_type=F32)


def _split3(x):
    p1 = x.astype(BF16)
    r1 = x - p1.astype(F32)
    p2 = r1.astype(BF16)
    p3 = (r1 - p2.astype(F32)).astype(BF16)
    return p1, p2, p3


def _exact_mm(m01, x):
    p1, p2, p3 = _split3(x)
    return _mm(m01, p1) + _mm(m01, p2) + _mm(m01, p3)


def _sigmoid(x):
    return 1.0 / (1.0 + jnp.exp(-x))


def _log_sigmoid(x):
    return jnp.minimum(x, 0.0) - jnp.log1p(jnp.exp(-jnp.abs(x)))


def _iota(shape, dim):
    return lax.broadcasted_iota(I32, shape, dim)


def _proj_body(x_ref, g_ref, w_ref, b_ref, pb_ref, pz_ref, aux_ref, *rest, tm, tiles_per_seq, prompt):
    x = x_ref[...]
    ms = jnp.mean(x * x, axis=-1, keepdims=True)
    xn = (x * lax.rsqrt(ms + RMS_EPS) * g_ref[...]).astype(BF16)

    def seg(a, b):
        return _mm(xn, w_ref[:, a:b])

    pb_ref[:, C_FQ:C_FK] = (seg(C_FQ, C_FK) * SCALE).astype(BF16)
    fkv = seg(C_FK, C_NQ)
    pb_ref[:, C_FK:C_NQ] = fkv.astype(BF16)
    pb_ref[:, C_NQ:C_NKV] = (seg(C_NQ, C_NKV) * SCALE).astype(BF16)
    nkv = seg(C_NKV, C_MQ)
    pb_ref[:, C_NKV:C_MQ] = nkv.astype(BF16)
    pb_ref[:, C_MQ:C_PU] = (seg(C_MQ, C_PU) * SCALE).astype(BF16)
    pz_ref[...] = seg(C_PU, C_SM)
    small = seg(C_SM, C_END)

    lane = _iota((tm, LANES), 1)
    lf = _log_sigmoid(small + b_ref[...])
    aux = jnp.where(lane < H_FOX, lf, jnp.where(lane < LANE_GATE0 + 3 * H_NSA, _sigmoid(small), 0.0))
    aux_ref[...] = aux
    if prompt:
        kvc_ref, ct_ref, lft_ref, fkvt_ref, nkvt_ref, nwt_ref, carry_ref = rest
        kvc_ref[0] = nkv[:, 0:LANES]
        kvc_ref[1] = nkv[:, LANES:256]
        fkvt_ref[...] = fkv.T
        nkvt_ref[...] = nkv[:, 0:512].T
        nwt_ref[...] = nkv[:, 512:768].T
        lfm = jnp.where(lane < H_FOX, lf, 0.0)
        lft_ref[...] = lfm.T[0:8, :]
        tri = (_iota((LANES, LANES), 0) >= _iota((LANES, LANES), 1)).astype(BF16)
        first = (pl.program_id(0) % tiles_per_seq) == 0
        carry = jnp.where(first, 0.0, carry_ref[0:1, :])
        for r in range(tm // LANES):
            blk = lfm[r * LANES:(r + 1) * LANES]
            cblk = _exact_mm(tri, blk) + carry
            carry = cblk[LANES - 1:LANES, :]
            ct_ref[:, r * LANES:(r + 1) * LANES] = cblk.T[0:8, :]
        carry_ref[0:1, :] = carry
    else:
        sf_ref, sft_ref, auxt_ref = rest
        sf_ref[:, 0:512] = fkv
        sf_ref[:, 512:1280] = nkv
        sft_ref[0:512, :] = fkv.T
        sft_ref[512:1280, :] = nkv.T
        auxt_ref[...] = aux.T


def _project(x2d, g, wp, bvec, *, tm, tiles_per_seq, prompt):
    n, d = x2d.shape
    grid = (n // tm,)
    row = lambda i: (i, 0)
    const = lambda i: (0, 0)
    out_shape = [jax.ShapeDtypeStruct((n, N_PB), BF16), jax.ShapeDtypeStruct((n, 1280), F32),
                 jax.ShapeDtypeStruct((n, LANES), F32)]
    out_specs = [pl.BlockSpec((tm, N_PB), row), pl.BlockSpec((tm, 1280), row),
                 pl.BlockSpec((tm, LANES), row)]
    scratch = []
    if prompt:
        nseq = n // (tm * tiles_per_seq)
        t = tm * tiles_per_seq
        seq_t = lambda rows: pl.BlockSpec(
            (None, rows, tm), lambda i: (i // tiles_per_seq, 0, i % tiles_per_seq))
        out_shape += [jax.ShapeDtypeStruct((2, n, LANES), F32),
                      jax.ShapeDtypeStruct((nseq, 8, t), F32), jax.ShapeDtypeStruct((nseq, 8, t), F32),
                      jax.ShapeDtypeStruct((nseq, 512, t), F32), jax.ShapeDtypeStruct((nseq, 512, t), F32),
                      jax.ShapeDtypeStruct((nseq, 256, t), F32)]
        out_specs += [pl.BlockSpec((2, tm, LANES), lambda i: (0, i, 0)),
                      seq_t(8), seq_t(8), seq_t(512), seq_t(512), seq_t(256)]
        scratch = [pltpu.VMEM((8, LANES), F32)]
    else:
        assert grid == (1,)
        out_shape += [jax.ShapeDtypeStruct((n, 1280), F32), jax.ShapeDtypeStruct((1280, n), F32),
                      jax.ShapeDtypeStruct((LANES, n), F32)]
        out_specs += [pl.BlockSpec((tm, 1280), row), pl.BlockSpec((1280, tm), const),
                      pl.BlockSpec((LANES, tm), const)]
    return pl.pallas_call(
        functools.partial(_proj_body, tm=tm, tiles_per_seq=tiles_per_seq, prompt=prompt),
        grid=grid,
        in_specs=[pl.BlockSpec((tm, d), row), pl.BlockSpec((1, d), const),
                  pl.BlockSpec((d, C_END), const), pl.BlockSpec((1, LANES), const)],
        out_specs=out_specs, out_shape=out_shape, scratch_shapes=scratch,
        compiler_params=pltpu.CompilerParams(dimension_semantics=("arbitrary",),
                                             vmem_limit_bytes=VMEM_LIMIT),
        name="proj",
    )(x2d, g, wp, bvec)


def _memkv_body(m_ref, w_ref, ot_ref, ob_ref):
    r = _mm(m_ref[...].astype(BF16), w_ref[...])
    ot_ref[...] = r.T
    ob_ref[...] = r.astype(BF16)


def _mem_project(mem3, w):
    b, n_mem, d = mem3.shape
    e = w.shape[1]
    return pl.pallas_call(
        _memkv_body, grid=(b,),
        in_specs=[pl.BlockSpec((None, n_mem, d), lambda i: (i, 0, 0)),
                  pl.BlockSpec((d, e), lambda i: (0, 0))],
        out_specs=[pl.BlockSpec((None, e, n_mem), lambda i: (i, 0, 0)),
                   pl.BlockSpec((None, n_mem, e), lambda i: (i, 0, 0))],
        out_shape=[jax.ShapeDtypeStruct((b, e, n_mem), F32), jax.ShapeDtypeStruct((b, n_mem, e), BF16)],
        name="memkv",
    )(mem3, w)


def _fox_body(qt_ref, kt_ref, q_ref, k_ref, v_ref, ct_ref, o_ref, m_ref, l_ref, acc_ref, *, tq, tk):
    qi = qt_ref[pl.program_id(1)]
    ki = kt_ref[pl.program_id(1)]

    @pl.when(ki == 0)
    def _():
        m_ref[...] = jnp.full(m_ref.shape, NEG_INF, F32)
        l_ref[...] = jnp.zeros(l_ref.shape, F32)
        acc_ref[...] = jnp.zeros(acc_ref.shape, F32)

    lane = _iota((tq, LANES), 1)

    def step(masked):
        if masked:
            causal = _iota((tq, tk), 0) >= _iota((tq, tk), 1)
        scores = []
        for h in range(H_FOX):
            pr, hh = divmod(h, 2)
            cs = slice(pr * LANES, (pr + 1) * LANES)
            q2 = q_ref[:, cs]
            qm = jnp.where((lane >= 64 * hh) & (lane < 64 * hh + 64), q2, jnp.zeros_like(q2))
            scores.append(_nt(qm, k_ref[:, cs]))
        probs = []
        for h in range(H_FOX):
            s = scores[h] - ct_ref[h:h + 1, :]
            if masked:
                s = jnp.where(causal, s, NEG_INF)
            m_prev = m_ref[h]
            m_new = jnp.maximum(m_prev, jnp.max(s, axis=1, keepdims=True))
            alpha = jnp.exp(m_prev - m_new)
            p = jnp.exp(s - jnp.concatenate([m_new] * (tk // LANES), axis=1))
            l_ref[h] = alpha * l_ref[h] + jnp.sum(p, axis=1, keepdims=True)
            m_ref[h] = m_new
            probs.append((p.astype(BF16), alpha))
        for h in range(H_FOX):
            cs = slice((h // 2) * LANES, (h // 2 + 1) * LANES)
            p, alpha = probs[h]
            acc_ref[h] = alpha * acc_ref[h] + _mm(p, v_ref[:, cs])

    @pl.when(ki < qi)
    def _():
        step(False)

    @pl.when(ki == qi)
    def _():
        step(True)
        for pr in range(H_FOX // 2):
            o0 = acc_ref[2 * pr] / l_ref[2 * pr]
            o1 = acc_ref[2 * pr + 1] / l_ref[2 * pr + 1]
            o_ref[:, pr * LANES:(pr + 1) * LANES] = jnp.where(lane < 64, o0, o1)


def _causal_pairs(nq):
    pairs = [(qi, ki) for qi in range(nq) for ki in range(qi + 1)]
    return (jnp.asarray([p[0] for p in pairs], I32), jnp.asarray([p[1] for p in pairs], I32))


def _fox_prompt(pb3, ct, *, tq):
    b, t, _ = pb3.shape
    qt, kt = _causal_pairs(t // tq)
    grid_spec = pltpu.PrefetchScalarGridSpec(
        num_scalar_prefetch=2, grid=(b, qt.shape[0]),
        in_specs=[pl.BlockSpec((None, tq, 256), lambda bi, s, qt, kt: (bi, qt[s], C_FQ // 256)),
                  pl.BlockSpec((None, tq, 256), lambda bi, s, qt, kt: (bi, kt[s], C_FK // 256)),
                  pl.BlockSpec((None, tq, 256), lambda bi, s, qt, kt: (bi, kt[s], C_FV // 256)),
                  pl.BlockSpec((None, 8, tq), lambda bi, s, qt, kt: (bi, 0, kt[s]))],
        out_specs=pl.BlockSpec((None, tq, 256), lambda bi, s, qt, kt: (bi, qt[s], 0)),
        scratch_shapes=[pltpu.VMEM((H_FOX, tq, LANES), F32), pltpu.VMEM((H_FOX, tq, LANES), F32),
                        pltpu.VMEM((H_FOX, tq, LANES), F32)])
    return pl.pallas_call(
        functools.partial(_fox_body, tq=tq, tk=tq),
        grid_spec=grid_spec,
        out_shape=jax.ShapeDtypeStruct((b, t, 256), F32),
        compiler_params=pltpu.CompilerParams(
            dimension_semantics=("parallel", "arbitrary"), vmem_limit_bytes=VMEM_LIMIT),
        name="fox_prompt",
    )(qt, kt, pb3, pb3, pb3, ct)


def _cmp_body(x_ref, pe_ref, w1_ref, w2_ref, o_ref, acc_ref, *, m):
    lo = _iota((m, LANES), 1) < HEAD_DIM
    acc_ref[...] = jnp.zeros(acc_ref.shape, F32)
    for pos in range(NSA_BLOCK):
        x = x_ref[pl.ds(pos, m, stride=NSA_BLOCK), :] + pe_ref[pos:pos + 1, :]
        a = jnp.concatenate([jnp.where(lo, x, 0.0), jnp.where(lo, 0.0, x)], axis=0)
        acc_ref[...] += _mm(a.astype(BF16), w1_ref[pos])
    h = acc_ref[...]
    o_ref[...] = _mm((h * _sigmoid(h)).astype(BF16), w2_ref[...])


def _compress_prompt(kvc, pe2, w1dup, w2):
    _, n, _ = kvc.shape
    m = n // NSA_BLOCK
    by_j = lambda blk: pl.BlockSpec((None,) + blk, lambda j: (j,) + (0,) * len(blk))
    return pl.pallas_call(
        functools.partial(_cmp_body, m=m), grid=(2,),
        in_specs=[by_j((n, LANES)), by_j((NSA_BLOCK, LANES)), by_j((NSA_BLOCK, LANES, 256)),
                  by_j((256, HEAD_DIM))],
        out_specs=by_j((G_NSA * m, HEAD_DIM)),
        out_shape=jax.ShapeDtypeStruct((2, G_NSA * m, HEAD_DIM), F32),
        scratch_shapes=[pltpu.VMEM((G_NSA * m, 256), F32)],
        compiler_params=pltpu.CompilerParams(vmem_limit_bytes=VMEM_LIMIT),
        name="cmp_prompt",
    )(kvc, pe2, w1dup, w2)


def _masked_softmax_cols(s, valid):
    s = jnp.where(valid, s, NEG_INF)
    e = jnp.where(valid, jnp.exp(s - jnp.max(s, axis=0, keepdims=True)), 0.0)
    return e / jnp.maximum(jnp.sum(e, axis=0, keepdims=True), 1e-30)


def _select_blocks(score, nb, topn):
    nbp, n = score.shape
    blk = _iota((nbp, n), 0)
    rank = jnp.zeros((nbp, n), F32)
    for m in range(nb):
        r = score[m:m + 1, :]
        rank = rank + jnp.where((r > score) | ((r == score) & (blk > m)), 1.0, 0.0)
    return jnp.where((rank < topn) & (score >= 0.0), 1.0, 0.0)


def _nsa_body(qt_ref, kt_ref, q_ref, ks_ref, vs_ref, kwc_ref, kwp_ref, vwc_ref, vwp_ref, kc_ref, vc_ref,
              aux_ref, o_ref, sel_ref, m_ref, l_ref, acc_ref, oc_ref, ow_ref, *, tq, nb, topn):
    qi = qt_ref[pl.program_id(1)]
    ki = kt_ref[pl.program_id(1)]
    q0 = qi * tq
    tk = tq
    lane = _iota((tq, LANES), 1)

    def qstack(g):
        q2 = q_ref[:, g * LANES:(g + 1) * LANES].astype(F32)
        q2r = pltpu.roll(q2, 64, 1)
        ing = (lane >= 64 * g) & (lane < 64 * g + 64)
        qa, qb = (q2, q2r) if g == 0 else (q2r, q2)
        return jnp.concatenate([jnp.where(ing, qa, 0.0), jnp.where(ing, qb, 0.0)],
                               axis=0).astype(BF16)

    def biased(raw, g, kd, valid):
        sa = jnp.where(valid, raw[:tq] + ALIBI_SLOPES[2 * g] * kd, NEG_INF)
        sb = jnp.where(valid, raw[tq:] + ALIBI_SLOPES[2 * g + 1] * kd, NEG_INF)
        return jnp.concatenate([sa, sb], axis=0)

    def lanes(x, n):
        return jnp.concatenate([x] * (n // LANES), axis=1)

    @pl.when(ki == 0)
    def _():
        m_ref[...] = jnp.full(m_ref.shape, NEG_INF, F32)
        l_ref[...] = jnp.zeros(l_ref.shape, F32)
        acc_ref[...] = jnp.zeros(acc_ref.shape, F32)
        dqk = _iota((tq, tk), 0) - _iota((tq, tk), 1)
        kd_cur = _iota((1, tk), 1).astype(F32)
        for g in range(G_NSA):
            qs = qstack(g)
            st = _nt(kc_ref[...], qs)
            blk = _iota((NBLK_PAD, 2 * tq), 0)
            col = _iota((NBLK_PAD, 2 * tq), 1)
            qpos = q0 + jnp.where(col >= tq, col - tq, col)
            slope = jnp.where(col < tq, ALIBI_SLOPES[2 * g], ALIBI_SLOPES[2 * g + 1])
            cend = blk * NSA_BLOCK + (NSA_BLOCK - 1)
            st = st + slope * (cend - q0).astype(F32)
            pt = _masked_softmax_cols(st, (cend <= qpos) & (blk < nb))
            oc_ref[g] = _mm(pt.T.astype(BF16), vc_ref[...])
            imp = pt[:, :tq] + pt[:, tq:]
            nbp = -(-nb // 8) * 8
            blk2 = _iota((nbp, tq), 0)
            cur = (q0 + _iota((nbp, tq), 1)) // NSA_BLOCK
            forced = (blk2 == 0) | (blk2 == cur) | (blk2 == cur - 1)
            score = jnp.where(blk2 > cur, -1.0, jnp.where(forced, FORCE_SCORE, imp[:nbp]))
            selt = _select_blocks(score, nb, topn)
            if nbp < NBLK_PAD:
                selt = jnp.concatenate([selt, jnp.zeros((NBLK_PAD - nbp, tq), F32)], axis=0)
            sel_ref[g] = selt.T.astype(BF16)
            s1 = biased(_nt(qs, kwc_ref[...]), g, kd_cur, dqk >= 0)
            s2 = biased(_nt(qs, kwp_ref[...]), g, kd_cur - tq, (dqk + tq <= NSA_WINDOW) & (qi > 0))
            mw = jnp.maximum(jnp.max(s1, axis=1, keepdims=True), jnp.max(s2, axis=1, keepdims=True))
            mw = jnp.broadcast_to(mw, (2 * tq, LANES))
            e1 = jnp.exp(s1 - lanes(mw, tk))
            e2 = jnp.exp(s2 - lanes(mw, tk))
            lw = jnp.sum(e1, axis=1, keepdims=True) + jnp.sum(e2, axis=1, keepdims=True)
            ow_ref[g] = (_mm(e1.astype(BF16), vwc_ref[...]) + _mm(e2.astype(BF16), vwp_ref[...])) \
                / jnp.maximum(lw, 1e-30)

    def sel_step(diag):
        expand = (_iota((NBLK_PAD, tk), 0) ==
                  ki * (tk // NSA_BLOCK) + _iota((NBLK_PAD, tk), 1) // NSA_BLOCK).astype(BF16)
        kd = (_iota((1, tk), 1) + (ki * tk - q0)).astype(F32)
        raws = [(_nt(qstack(g), ks_ref[...]), _mm(sel_ref[g], expand)) for g in range(G_NSA)]
        probs = []
        for g in range(G_NSA):
            raw, picked = raws[g]
            valid = picked > 0.5
            if diag:
                valid = valid & (_iota((tq, tk), 0) >= _iota((tq, tk), 1))
            s = biased(raw, g, kd, valid)
            m_prev = m_ref[g]
            m_new = jnp.maximum(m_prev, jnp.max(s, axis=1, keepdims=True))
            alpha = jnp.exp(m_prev - m_new)
            p = jnp.exp(s - lanes(m_new, tk))
            l_ref[g] = alpha * l_ref[g] + jnp.sum(p, axis=1, keepdims=True)
            m_ref[g] = m_new
            probs.append((p.astype(BF16), alpha))
        for g in range(G_NSA):
            p, alpha = probs[g]
            acc_ref[g] = alpha * acc_ref[g] + _mm(p, vs_ref[...])

    @pl.when(ki < qi)
    def _():
        sel_step(False)

    @pl.when(ki == qi)
    def _():
        sel_step(True)
        for g in range(G_NSA):
            o_s = acc_ref[g] / jnp.maximum(l_ref[g], 1e-30)
            o_c = oc_ref[g]
            o_w = ow_ref[g]
            outs = []
            for hh in range(2):
                h = 2 * g + hh
                rs = slice(hh * tq, (hh + 1) * tq)
                gl = LANE_GATE0 + 3 * h
                outs.append(aux_ref[:, gl:gl + 1] * o_c[rs] + aux_ref[:, gl + 1:gl + 2] * o_s[rs]
                            + aux_ref[:, gl + 2:gl + 3] * o_w[rs])
            oa, ob = outs
            if g == 0:
                ob = pltpu.roll(ob, 64, 1)
            else:
                oa = pltpu.roll(oa, 64, 1)
            o_ref[:, g * LANES:(g + 1) * LANES] = jnp.where(lane < 64, oa, ob)


def _nsa_prompt(pb3, kcmp, vcmp, aux3, *, tq, nb):
    b, t, _ = pb3.shape
    qt, kt = _causal_pairs(t // tq)
    ct = C_NKV // LANES
    cur = lambda c: (lambda bi, s, qt, kt: (bi, qt[s], c))
    prev = lambda c: (lambda bi, s, qt, kt: (bi, jnp.maximum(qt[s] - 1, 0), c))
    kvt = lambda c: (lambda bi, s, qt, kt: (bi, kt[s], c))
    cmp_spec = pl.BlockSpec((None, NBLK_PAD, LANES), lambda bi, s, qt, kt: (bi, 0, 0))
    tile = lambda f: pl.BlockSpec((None, tq, LANES), f)
    grid_spec = pltpu.PrefetchScalarGridSpec(
        num_scalar_prefetch=2, grid=(b, qt.shape[0]),
        in_specs=[pl.BlockSpec((None, tq, 256), lambda bi, s, qt, kt: (bi, qt[s], C_NQ // 256)),
                  tile(kvt(ct + 2)), tile(kvt(ct + 3)),
                  tile(cur(ct + 4)), tile(prev(ct + 4)), tile(cur(ct + 5)), tile(prev(ct + 5)),
                  cmp_spec, cmp_spec,
                  pl.BlockSpec((None, tq, LANES), lambda bi, s, qt, kt: (bi, qt[s], 0))],
        out_specs=pl.BlockSpec((None, tq, 256), lambda bi, s, qt, kt: (bi, qt[s], 0)),
        scratch_shapes=[pltpu.VMEM((G_NSA, tq, NBLK_PAD), BF16),
                        pltpu.VMEM((G_NSA, 2 * tq, LANES), F32), pltpu.VMEM((G_NSA, 2 * tq, LANES), F32),
                        pltpu.VMEM((G_NSA, 2 * tq, LANES), F32),
                        pltpu.VMEM((G_NSA, 2 * tq, LANES), F32),
                        pltpu.VMEM((G_NSA, 2 * tq, LANES), F32)])
    return pl.pallas_call(
        functools.partial(_nsa_body, tq=tq, nb=nb, topn=min(NSA_TOPN, nb)),
        grid_spec=grid_spec,
        out_shape=jax.ShapeDtypeStruct((b, t, 256), F32),
        compiler_params=pltpu.CompilerParams(
            dimension_semantics=("parallel", "arbitrary"), vmem_limit_bytes=VMEM_LIMIT),
        name="nsa_prompt",
    )(qt, kt, pb3, pb3, pb3, pb3, pb3, pb3, pb3, kcmp, vcmp, aux3)


def _mem_attend(mq_ref, mk_ref, mv_ref, rows):
    lane = _iota((rows, LANES), 1)
    n_mem = mk_ref.shape[0]
    scores = []
    for h in range(H_MEM):
        pr, hh = divmod(h, 2)
        cs = slice(pr * LANES, (pr + 1) * LANES)
        q2 = mq_ref[:, cs]
        qm = jnp.where((lane >= 64 * hh) & (lane < 64 * hh + 64), q2, jnp.zeros_like(q2))
        scores.append(_nt(qm, mk_ref[:, cs]))
    weights = []
    for h in range(H_MEM):
        s = scores[h]
        mx = jnp.broadcast_to(jnp.max(s, axis=1, keepdims=True), (rows, LANES))
        e = jnp.exp(s - jnp.concatenate([mx] * (n_mem // LANES), axis=1))
        weights.append((e.astype(BF16), jnp.sum(e, axis=1, keepdims=True)))
    outs = []
    for h in range(H_MEM):
        e, den = weights[h]
        cs = slice((h // 2) * LANES, (h // 2 + 1) * LANES)
        outs.append(_mm(e, mv_ref[:, cs]) / den)
    return [jnp.where(lane < 64, outs[2 * pr], outs[2 * pr + 1]) for pr in range(H_MEM // 2)]


def _mix_out(x, parts, z_of, wout_ref, fg_ref, final):
    y = x
    for c in range(0, len(parts), 2):
        gated = []
        for cc in (c, c + 1):
            z = z_of(cc)
            gated.append((parts[cc] * (z * _sigmoid(z))).astype(BF16))
        y = y + _mm(jnp.concatenate(gated, axis=1), wout_ref[c * LANES:(c + 2) * LANES, :])
    if final:
        y = y * lax.rsqrt(jnp.mean(y * y, axis=-1, keepdims=True) + RMS_EPS) * fg_ref[...]
    return y


def _pool_window_lane(shape):
    grp = _iota(shape, 1) // (D_GROUP // len(POOL_WINDOWS))
    wl = jnp.where(grp == 0, POOL_WINDOWS[0], jnp.where(grp == 1, POOL_WINDOWS[1],
                   jnp.where(grp == 2, POOL_WINDOWS[2], POOL_WINDOWS[3])))
    return grp, wl


HALO = 32


def _post_body(oa_ref, ob_ref, pz_ref, halo_ref, mq_ref, mk_ref, mv_ref, x_ref, wout_ref, pw_ref,
               ps_ref, fg_ref, y_ref, s0, s1, s2, s3, *, tm, final):
    i = pl.program_id(1)
    u = pz_ref[:, 0:256]
    s0[0:HALO, :] = jnp.where(i > 0, halo_ref[...], 0.0)
    s0[HALO:HALO + tm, :] = u
    n = tm + HALO
    s1[8:n, :] = s0[8:n, :] + s0[7:n - 1, :]
    s2[16:n, :] = s1[16:n, :] + s1[14:n - 2, :]
    s3[24:n, :] = s2[24:n, :] + s2[20:n - 4, :]
    a16 = s3[HALO:n, :] + s3[HALO - 8:n - 8, :]
    grp, wl = _pool_window_lane((tm, 256))
    tsum = jnp.where(grp == 0, s1[HALO:n, :], jnp.where(grp == 1, s2[HALO:n, :],
                     jnp.where(grp == 2, s3[HALO:n, :], a16)))
    pos = i * tm + _iota((tm, 256), 0)
    cnt = jnp.minimum(pos + 1, wl).astype(F32)
    o_pool = _mm((tsum / cnt - u).astype(BF16), pw_ref[...]) * ps_ref[...]
    o_mem = _mem_attend(mq_ref, mk_ref, mv_ref, tm)
    parts = [oa_ref[:, 0:LANES], oa_ref[:, LANES:256], ob_ref[:, 0:LANES], ob_ref[:, LANES:256],
             o_pool[:, 0:LANES], o_pool[:, LANES:256], o_mem[0], o_mem[1]]
    z_of = lambda c: pz_ref[:, 256 + c * LANES:256 + (c + 1) * LANES]
    y_ref[...] = _mix_out(x_ref[...], parts, z_of, wout_ref, fg_ref, final)


def _post_prompt(oa, ob, pz3, pb3, memb, x3, wout, pw_bd, ps, fg, *, tm, final):
    b, t, d = x3.shape
    nt = t // tm
    row = lambda bi, i: (bi, i, 0)
    const = lambda bi, i: (0, 0)
    return pl.pallas_call(
        functools.partial(_post_body, tm=tm, final=final),
        grid=(b, nt),
        in_specs=[pl.BlockSpec((None, tm, 256), row), pl.BlockSpec((None, tm, 256), row),
                  pl.BlockSpec((None, tm, 1280), row),
                  pl.BlockSpec((None, HALO, 256),
                               lambda bi, i: (bi, jnp.maximum(i * (tm // HALO) - 1, 0), 0)),
                  pl.BlockSpec((None, tm, 256), lambda bi, i: (bi, i, C_MQ // 256)),
                  pl.BlockSpec((None, 256, 256), lambda bi, i: (bi, 0, 0)),
                  pl.BlockSpec((None, 256, 256), lambda bi, i: (bi, 0, 1)),
                  pl.BlockSpec((None, tm, d), row),
                  pl.BlockSpec((d, d), const), pl.BlockSpec((256, 256), const),
                  pl.BlockSpec((1, 256), const), pl.BlockSpec((1, d), const)],
        out_specs=pl.BlockSpec((None, tm, d), row),
        out_shape=jax.ShapeDtypeStruct((b, t, d), F32),
        scratch_shapes=[pltpu.VMEM((tm + HALO, 256), F32)] * 4,
        compiler_params=pltpu.CompilerParams(dimension_semantics=("parallel", "arbitrary"),
                                             vmem_limit_bytes=VMEM_LIMIT),
        name="post_prompt",
    )(oa, ob, pz3, pz3, pb3, memb, memb, x3, wout, pw_bd, ps, fg)


CMP_PITCH = 72
DQ = 4


def _cmp_sample_body(pt_ref, nsa_hbm, pe_ref, w1_ref, w2a_ref, w2b_ref, o_ref, xbuf, acc0, acc1, sem,
                     *, pages, layer):
    j = pl.program_id(0)
    i = pl.program_id(1)
    ni = pl.num_programs(1)
    s = j * ni + i
    m = pages * G_NSA

    def chunk_copy(jj, ii, slot, p, g):
        page = pt_ref[ii * pages + p]
        return pltpu.make_async_copy(
            nsa_hbm.at[layer, page, pl.ds(jj * LANES + g * HEAD_DIM, HEAD_DIM), :],
            xbuf.at[slot, pl.ds((p * G_NSA + g) * CMP_PITCH, HEAD_DIM), :], sem.at[slot])

    def for_chunks(jj, ii, slot, fn):
        def body(p, carry):
            for g in range(G_NSA):
                fn(chunk_copy(jj, ii, slot, p, g))
            return carry
        lax.fori_loop(0, pages, body, 0)

    @pl.when(s == 0)
    def _():
        for_chunks(0, 0, 0, lambda c: c.start())

    @pl.when(s + 1 < 2 * ni)
    def _():
        wrap = i + 1 == ni
        for_chunks(jnp.where(wrap, j + 1, j), jnp.where(wrap, 0, i + 1), (s + 1) % 2,
                   lambda c: c.start())

    slot = s % 2
    for_chunks(j, i, slot, lambda c: c.wait())

    lo = _iota((m, LANES), 1) < HEAD_DIM
    acc0[...] = jnp.zeros(acc0.shape, F32)
    acc1[...] = jnp.zeros(acc1.shape, F32)
    for dq in range(HEAD_DIM // DQ):
        xs = [xbuf[slot, pl.ds(DQ * dq + k, m, stride=CMP_PITCH), :] + pe_ref[DQ * dq + k:DQ * dq + k + 1, :]
              for k in range(DQ)]
        xr = [pltpu.roll(x, HEAD_DIM, 1) for x in xs]
        a0 = jnp.concatenate([jnp.where(lo, xs[0], xr[1]), jnp.where(lo, xs[2], xr[3])], axis=1)
        a1 = jnp.concatenate([jnp.where(lo, xr[0], xs[1]), jnp.where(lo, xr[2], xs[3])], axis=1)
        w = w1_ref[dq]
        acc0[...] += _mm(a0.astype(BF16), w)
        acc1[...] += _mm(a1.astype(BF16), w)
    h0 = acc0[...]
    h1 = acc1[...]
    h0 = (h0 * _sigmoid(h0)).astype(BF16)
    h1 = (h1 * _sigmoid(h1)).astype(BF16)
    o_ref[...] = _mm(h0, w2a_ref[...]) + _mm(h1, w2b_ref[...])


def _compress_sample(page_flat, nsa_t, pe_t, w1q, w2a, w2b, *, pages, layer):
    npages = page_flat.shape[0]
    nsteps = npages // pages
    m = pages * G_NSA
    by_j = lambda blk: pl.BlockSpec((None,) + blk, lambda j, i, pt: (j,) + (0,) * len(blk))
    grid_spec = pltpu.PrefetchScalarGridSpec(
        num_scalar_prefetch=1, grid=(2, nsteps),
        in_specs=[pl.BlockSpec(memory_space=pl.ANY), by_j((HEAD_DIM, LANES)),
                  by_j((HEAD_DIM // DQ, 256, 256)), by_j((256, LANES)), by_j((256, LANES))],
        out_specs=pl.BlockSpec((None, m, LANES), lambda j, i, pt: (j, i, 0)),
        scratch_shapes=[pltpu.VMEM((2, m * CMP_PITCH, LANES), F32), pltpu.VMEM((m, 256), F32),
                        pltpu.VMEM((m, 256), F32), pltpu.SemaphoreType.DMA((2,))])
    return pl.pallas_call(
        functools.partial(_cmp_sample_body, pages=pages, layer=layer),
        grid_spec=grid_spec,
        out_shape=jax.ShapeDtypeStruct((2, npages * G_NSA, LANES), F32),
        compiler_params=pltpu.CompilerParams(dimension_semantics=("arbitrary", "arbitrary"),
                                             vmem_limit_bytes=VMEM_LIMIT),
        name="cmp_sample",
    )(page_flat, nsa_t, pe_t, w1q, w2a, w2b)


HROWS = 16


def _rows_from_lanes(vec_row, pick):
    return jnp.sum(jnp.where(pick, vec_row, 0.0), axis=1, keepdims=True)


def _bf16r(x):
    return x.astype(BF16).astype(F32)


def _att_sample_body(pt_ref, fox_hbm, lf_hbm, nsa_hbm, pb_ref, sf_ref, aux_ref, win_ref, mem_ref,
                     kc_ref, vc_ref, exp_ref, o_ref, wout_ref, fbuf, lbuf, sbuf, sem,
                     *, npg, past, layer):
    b = pl.program_id(0)
    nb = pl.num_programs(0)

    def copies(seq, slot):
        out = []
        for i in range(npg):
            page = pt_ref[seq * npg + i]
            out.append(pltpu.make_async_copy(
                fox_hbm.at[layer, page], fbuf.at[slot, i], sem.at[0, slot]))
            out.append(pltpu.make_async_copy(
                lf_hbm.at[layer, page], lbuf.at[slot, pl.ds(i * 8, 8), :], sem.at[1, slot]))
            out.append(pltpu.make_async_copy(
                nsa_hbm.at[layer, page, pl.ds(256, 256), :], sbuf.at[slot, i], sem.at[2, slot]))
        return out

    def slab(buf, r0, r1):
        return jnp.concatenate([buf[slot, i, r0:r1, :].astype(BF16) for i in range(npg)], axis=1)

    @pl.when(b == 0)
    def _():
        for c in copies(0, 0):
            c.start()

    @pl.when(b + 1 < nb)
    def _():
        for c in copies(b + 1, (b + 1) % 2):
            c.start()

    slot = b % 2
    for c in copies(b, slot):
        c.wait()

    row8 = _iota((HROWS, LANES), 0)
    lane8 = _iota((HROWS, LANES), 1)
    aux = aux_ref[...]
    pbr = pb_ref[...].astype(F32)
    sfr = sf_ref[...]

    def head_rows(q256):
        r = _iota((HROWS, 256), 0)
        l = _iota((HROWS, 256), 1)
        return jnp.where(l // HEAD_DIM == r, q256, 0.0).astype(BF16)

    def pick_heads(o8):
        r = _iota((HROWS, 256), 0)
        l = _iota((HROWS, 256), 1)
        return jnp.sum(jnp.where(l // HEAD_DIM == r, o8, 0.0), axis=0, keepdims=True)

    kt = slab(fbuf, 0, 256)
    vt = slab(fbuf, 256, 512)
    qf = head_rows(pbr[:, C_FQ:C_FK])
    s = _mm(qf, kt)
    lf2 = lbuf[slot]
    r2 = _iota((npg * 8, npg * 8), 0)
    c2 = _iota((npg * 8, npg * 8), 1)
    upper = (_iota((LANES, LANES), 0) > _iota((LANES, LANES), 1)).astype(BF16)
    within = _exact_mm_rhs(lf2, upper)
    tot = jnp.sum(lf2, axis=1, keepdims=True)
    later_pages = ((c2 % 8 == r2 % 8) & (c2 // 8 > r2 // 8)).astype(BF16)
    later = _exact_mm(later_pages, jnp.broadcast_to(tot, (npg * 8, LANES)))
    rr = _iota((npg * 8, LANES), 0)
    ll = _iota((npg * 8, LANES), 1)
    lf_new = _rows_from_lanes(aux, ll == rr % 8)
    bias = within + later + lf_new
    zpad = jnp.zeros((HROWS - 8, LANES), F32)
    s = jnp.concatenate(
        [s[:, i * PAGE_SIZE:(i + 1) * PAGE_SIZE] + jnp.concatenate([bias[i * 8:(i + 1) * 8, :], zpad], axis=0)
         for i in range(npg)], axis=1)
    knew = _bf16r(sfr[:, 0:256])
    vnew = _bf16r(sfr[:, 256:512])
    s_new = jnp.sum(qf.astype(F32) * knew, axis=1, keepdims=True)
    m = jnp.maximum(jnp.max(s, axis=1, keepdims=True), s_new)
    e = jnp.exp(s - m)
    e_new = jnp.exp(s_new - m)
    den = jnp.sum(e, axis=1, keepdims=True) + e_new
    o8 = (_nt(e.astype(BF16), vt) + _bf16r(e_new) * vnew) / den
    o_ref[:, 0:256] = pick_heads(o8)

    nq = pbr[:, C_NQ:C_NKV]
    t0 = jnp.broadcast_to(nq[:, 0:LANES], (HROWS, LANES))
    t1 = jnp.broadcast_to(nq[:, LANES:256], (HROWS, LANES))
    t0r, t1r = pltpu.roll(t0, 64, 1), pltpu.roll(t1, 64, 1)
    lo = lane8 < 64
    hrows = [jnp.where(lo, t0, 0.0), jnp.where(lo, t0r, 0.0),
             jnp.where(lo, 0.0, t1r), jnp.where(lo, 0.0, t1)]
    qn = jnp.zeros((HROWS, LANES), F32)
    for h in range(H_NSA):
        qn = jnp.where(row8 == h, hrows[h], qn)
    qn_b = qn.astype(BF16)
    slope8 = jnp.zeros((HROWS, 1), F32)
    r81 = _iota((HROWS, 1), 0)
    for h in range(H_NSA):
        slope8 = jnp.where(r81 == h, ALIBI_SLOPES[h], slope8)
    nkv_new = sfr[:, 512:1280]
    nblk = past // NSA_BLOCK

    def new_score(k128):
        return jnp.sum(qn_b.astype(F32) * _bf16r(k128), axis=1, keepdims=True)

    sc = _nt(qn_b, kc_ref[...].astype(BF16))
    cend = _iota((HROWS, nblk), 1) * NSA_BLOCK + (NSA_BLOCK - 1)
    sc = sc - slope8 * (past - cend).astype(F32)
    ec = jnp.exp(sc - jnp.max(sc, axis=1, keepdims=True))
    pc = ec / jnp.maximum(jnp.sum(ec, axis=1, keepdims=True), 1e-30)
    o_c = _mm(pc.astype(BF16), vc_ref[...].astype(BF16))
    imp = jnp.concatenate([pc[2 * g:2 * g + 1] + pc[2 * g + 1:2 * g + 2] for g in range(G_NSA)]
                          + [jnp.zeros((HROWS - G_NSA, nblk), F32)], axis=0)
    if nblk < LANES:
        imp = jnp.concatenate([imp, jnp.zeros((HROWS, LANES - nblk), F32)], axis=1)
    forced = (lane8 == 0) | (lane8 == nblk) | (lane8 == nblk - 1)
    score = jnp.where(lane8 > nblk, -1.0, jnp.where(forced, FORCE_SCORE, imp))
    rank = jnp.zeros((HROWS, LANES), F32)
    for mblk in range(nblk + 1):
        r = score[:, mblk:mblk + 1]
        rank = rank + jnp.where((r > score) | ((r == score) & (lane8 > mblk)), 1.0, 0.0)
    sel_g = jnp.where((rank < min(NSA_TOPN, nblk + 1)) & (score >= 0.0), 1.0, 0.0)
    sel_h = jnp.zeros((HROWS, LANES), F32)
    for h in range(H_NSA):
        sel_h = jnp.where(row8 == h, sel_g[h // 2:h // 2 + 1, :], sel_h)
    valid = _mm(sel_h.astype(BF16), exp_ref[...]) > 0.5
    kst = slab(sbuf, 0, LANES)
    vst = slab(sbuf, LANES, 256)
    ss = _mm(qn_b, kst) - slope8 * (past - _iota((HROWS, past), 1)).astype(F32)
    ss = jnp.where(valid, ss, NEG_INF)
    ss_new = new_score(nkv_new[:, 256:384])
    ms = jnp.maximum(jnp.max(ss, axis=1, keepdims=True), ss_new)
    es = jnp.where(valid, jnp.exp(ss - ms), 0.0)
    es_new = jnp.exp(ss_new - ms)
    o_s = (_nt(es.astype(BF16), vst) + _bf16r(es_new) * _bf16r(nkv_new[:, 384:512])) \
        / (jnp.sum(es, axis=1, keepdims=True) + es_new)
    wl_ = win_ref.shape[1]
    win = win_ref[...]
    sw = _mm(qn_b, win[0:LANES].astype(BF16)) - slope8 * (wl_ - _iota((HROWS, wl_), 1)).astype(F32)
    sw_new = new_score(nkv_new[:, 512:640])
    mw = jnp.maximum(jnp.max(sw, axis=1, keepdims=True), sw_new)
    ew = jnp.exp(sw - mw)
    ew_new = jnp.exp(sw_new - mw)
    o_w = (_nt(ew.astype(BF16), win[LANES:256].astype(BF16))
           + _bf16r(ew_new) * _bf16r(nkv_new[:, 640:768])) / (jnp.sum(ew, axis=1, keepdims=True) + ew_new)
    rw = _iota((256, 256), 0)
    lw_ = _iota((256, 256), 1)
    new_col = jnp.sum(jnp.where(rw == lw_, nkv_new[:, 512:768], 0.0), axis=1, keepdims=True)
    wout_ref[...] = jnp.where(_iota((256, wl_), 1) == wl_ - 1, new_col, pltpu.roll(win, wl_ - 1, 1))
    gates = [_rows_from_lanes(aux, lane8 == LANE_GATE0 + 3 * row8 + c) for c in range(3)]
    o_n = gates[0] * o_c + gates[1] * o_s + gates[2] * o_w
    o_nr = pltpu.roll(o_n, 64, 1)
    lo1 = _iota((1, LANES), 1) < 64
    o_ref[:, 256:384] = jnp.where(lo1, o_n[0:1], o_nr[1:2])
    o_ref[:, 384:512] = jnp.where(lo1, o_nr[2:3], o_n[3:4])

    qm = head_rows(pbr[:, C_MQ:C_PU])
    sm = _mm(qm, mem_ref[0:256, :].astype(BF16))
    em = jnp.exp(sm - jnp.max(sm, axis=1, keepdims=True))
    pm = em / jnp.sum(em, axis=1, keepdims=True)
    o_ref[:, 512:768] = pick_heads(_nt(pm.astype(BF16), mem_ref[256:512, :].astype(BF16)))


def _exact_mm_rhs(x, m01):
    p1, p2, p3 = _split3(x)
    return _mm(p1, m01) + _mm(p2, m01) + _mm(p3, m01)


def _attend_sample(page_flat, fox_t, lf_t, nsa_t, pb3, sf3, aux3, win_t, mem_t, kcmp, vcmp, expand,
                   *, npg, layer):
    db = pb3.shape[0]
    past = npg * PAGE_SIZE
    per_seq = lambda blk: pl.BlockSpec((None,) + blk, lambda b, pt: (b, 0, 0))
    per_seq_layer = lambda blk: pl.BlockSpec((None, None) + blk, lambda b, pt: (layer, b, 0, 0))
    grid_spec = pltpu.PrefetchScalarGridSpec(
        num_scalar_prefetch=1, grid=(db,),
        in_specs=[pl.BlockSpec(memory_space=pl.ANY), pl.BlockSpec(memory_space=pl.ANY),
                  pl.BlockSpec(memory_space=pl.ANY),
                  per_seq((1, N_PB)), per_seq((1, 1280)), per_seq((1, LANES)),
                  per_seq_layer(win_t.shape[2:]), per_seq_layer(mem_t.shape[2:]),
                  per_seq(kcmp.shape[1:]), per_seq(vcmp.shape[1:]),
                  pl.BlockSpec(expand.shape, lambda b, pt: (0, 0))],
        out_specs=[per_seq((1, 768)), per_seq(win_t.shape[2:])],
        scratch_shapes=[pltpu.VMEM((2, npg, 512, PAGE_SIZE), F32), pltpu.VMEM((2, npg * 8, LANES), F32),
                        pltpu.VMEM((2, npg, 256, PAGE_SIZE), F32), pltpu.SemaphoreType.DMA((3, 2))])
    return pl.pallas_call(
        functools.partial(_att_sample_body, npg=npg, past=past, layer=layer),
        grid_spec=grid_spec,
        out_shape=[jax.ShapeDtypeStruct((db, 1, 768), F32),
                   jax.ShapeDtypeStruct((db,) + win_t.shape[2:], F32)],
        compiler_params=pltpu.CompilerParams(dimension_semantics=("arbitrary",),
                                             vmem_limit_bytes=VMEM_LIMIT),
        name="att_sample",
    )(page_flat, fox_t, lf_t, nsa_t, pb3, sf3, aux3, win_t, mem_t, kcmp, vcmp, expand)


def _post_sample_body(oatt_ref, pz_ref, sp_ref, x_ref, wout_ref, pw_ref, ps_ref, fg_ref, y_ref,
                      spo_ref, *, rows, past, final):
    u = pz_ref[:, 0:256]
    grp, wl = _pool_window_lane((rows, 256))
    tsum = u
    for j in range(1, POOL_BUF + 1):
        tsum = tsum + jnp.where(wl > j, sp_ref[POOL_BUF - j], 0.0)
    for j in range(POOL_BUF - 1):
        spo_ref[j] = sp_ref[j + 1]
    spo_ref[POOL_BUF - 1] = u
    cnt = jnp.minimum(past + 1, wl).astype(F32)
    o_pool = _mm((tsum / cnt - u).astype(BF16), pw_ref[...]) * ps_ref[...]
    parts = [oatt_ref[:, c * LANES:(c + 1) * LANES] for c in range(4)]
    parts += [o_pool[:, 0:LANES], o_pool[:, LANES:256]]
    parts += [oatt_ref[:, 512:640], oatt_ref[:, 640:768]]
    z_of = lambda c: pz_ref[:, 256 + c * LANES:256 + (c + 1) * LANES]
    y_ref[...] = _mix_out(x_ref[...], parts, z_of, wout_ref, fg_ref, final)


def _post_sample(oatt, pz, pool_t, x2d, wout, pw_bd, ps, fg, *, past, final, layer):
    rows, d = x2d.shape
    full = lambda a: pl.BlockSpec(a.shape, lambda i: (0,) * a.ndim)
    return pl.pallas_call(
        functools.partial(_post_sample_body, rows=rows, past=past, final=final),
        grid=(1,),
        in_specs=[full(oatt), full(pz),
                  pl.BlockSpec((None, POOL_BUF, rows, 256), lambda i: (layer, 0, 0, 0)),
                  full(x2d), full(wout), full(pw_bd), full(ps), full(fg)],
        out_specs=[pl.BlockSpec((rows, d), lambda i: (0, 0)),
                   pl.BlockSpec((POOL_BUF, rows, 256), lambda i: (0, 0, 0))],
        out_shape=[jax.ShapeDtypeStruct((rows, d), F32),
                   jax.ShapeDtypeStruct((POOL_BUF, rows, 256), F32)],
        compiler_params=pltpu.CompilerParams(vmem_limit_bytes=VMEM_LIMIT),
        name="post_sample",
    )(oatt, pz, pool_t, x2d, wout, pw_bd, ps, fg)


def _prep_layer_weights(norm_g, w_in, b_fgt, nsa_pe, nsa_w1, nsa_w2, pool_w, pool_scale,
                        w_mem_kv, w_out):
    d = w_in.shape[0]
    o = 0
    segs = {}
    for name, width in (("fqkv", 768), ("ff", H_FOX), ("nq", 256), ("nkv", 768),
                        ("ng", 3 * H_NSA), ("pu", 256), ("mq", 256), ("z", d)):
        segs[name] = w_in[:, o:o + width]
        o += width
    pad = jnp.zeros((d, LANES - H_FOX - 3 * H_NSA), w_in.dtype)
    wp = jnp.concatenate([segs["fqkv"], segs["nq"], segs["nkv"], segs["mq"], segs["pu"],
                          segs["z"], segs["ff"], segs["ng"], pad], axis=1).astype(BF16)
    bvec = jnp.zeros((1, LANES), F32).at[0, 0:H_FOX].set(b_fgt.astype(F32))
    gw = D_GROUP // len(POOL_WINDOWS)
    pw_bd = jnp.zeros((D_GROUP, D_GROUP), F32)
    for g in range(len(POOL_WINDOWS)):
        pw_bd = pw_bd.at[g * gw:(g + 1) * gw, g * gw:(g + 1) * gw].set(pool_w[g])
    return dict(g=norm_g.reshape(1, d), wp=wp, bvec=bvec,
                pe2=jnp.concatenate([nsa_pe, nsa_pe], axis=2).astype(F32),
                w1dup=jnp.concatenate([nsa_w1.reshape(2, NSA_BLOCK, HEAD_DIM, -1)] * 2, axis=2).astype(BF16),
                w2=nsa_w2.astype(BF16),
                w1q=jnp.transpose(nsa_w1.reshape(2, NSA_BLOCK, HEAD_DIM, -1), (0, 2, 1, 3)).reshape(
                    2, HEAD_DIM // DQ, DQ * NSA_BLOCK, -1).astype(BF16),
                pe_t=jnp.concatenate([jnp.transpose(nsa_pe, (0, 2, 1))] * 2, axis=2).astype(F32),
                w2a=jnp.pad(nsa_w2, ((0, 0), (0, 0), (0, HEAD_DIM))).astype(BF16),
                w2b=jnp.pad(nsa_w2, ((0, 0), (0, 0), (HEAD_DIM, 0))).astype(BF16),
                pw_bd=pw_bd.astype(BF16), ps=pool_scale.reshape(1, D_GROUP).astype(F32),
                wmkv=w_mem_kv.astype(BF16), wout=w_out.astype(BF16))


def _layer_prompt(x, mem, lw, fg, final, tile):
    b, t, d = x.shape
    n = b * t
    nb = t // NSA_BLOCK
    pb, pz, aux, kvc, ct, lft, fkvt, nkvt, nwt = _project(
        x.reshape(n, d), lw["g"], lw["wp"], lw["bvec"], tm=tile, tiles_per_seq=t // tile, prompt=True)
    pb3 = pb.reshape(b, t, N_PB)
    o_a = _fox_prompt(pb3, ct, tq=tile)
    cmp = _compress_prompt(kvc, lw["pe2"], lw["w1dup"], lw["w2"]).reshape(2, G_NSA, b, nb, HEAD_DIM)
    cmp = jnp.transpose(cmp, (0, 2, 3, 1, 4)).reshape(2, b, nb, LANES)
    cmp = jnp.pad(cmp, ((0, 0), (0, 0), (0, NBLK_PAD - nb), (0, 0))).astype(BF16)
    o_b = _nsa_prompt(pb3, cmp[0], cmp[1], aux.reshape(b, t, LANES), tq=tile, nb=nb)
    n_mem = mem.shape[1]
    mem_t, memb = _mem_project(mem, lw["wmkv"])
    pz3 = pz.reshape(b, t, 1280)
    y = _post_prompt(o_a, o_b, pz3, pb3, memb, x, lw["wout"], lw["pw_bd"], lw["ps"], fg,
                     tm=tile, final=final)
    w_keep = min(NSA_WINDOW, t)
    back = (0, 4, 1, 2, 3)
    state = (jnp.transpose(fkvt.reshape(b, 2, H_FOX, HEAD_DIM, t), back),
             jnp.transpose(lft[:, 0:H_FOX, :], (0, 2, 1)),
             jnp.transpose(nkvt.reshape(b, 4, G_NSA, HEAD_DIM, t), back),
             jnp.transpose(nwt[:, :, t - w_keep:].reshape(b, 2, G_NSA, HEAD_DIM, w_keep), back),
             pz3[:, t - POOL_BUF:, 0:256],
             jnp.transpose(mem_t.reshape(b, 2, H_MEM, HEAD_DIM, n_mem), back))
    return y, state


def _native_views(cache_fox_kv, cache_fox_lf, cache_nsa_kv, state_nsa_win, state_pool, cache_mem_kv):
    dp, n_phys = cache_fox_kv.shape[:2]
    db = state_nsa_win.shape[1]
    to_t = (0, 1, 3, 4, 5, 2)
    fox_t = jnp.transpose(cache_fox_kv, to_t).reshape(dp, n_phys, 512, PAGE_SIZE)
    lf_t = jnp.pad(jnp.transpose(cache_fox_lf.astype(F32), (0, 1, 3, 2)),
                   ((0, 0), (0, 0), (0, 8 - H_FOX), (0, 0)))
    nsa_t = jnp.transpose(cache_nsa_kv, to_t).reshape(dp, n_phys, 512, PAGE_SIZE)
    win_t = jnp.transpose(state_nsa_win, to_t).reshape(dp, db, 256, state_nsa_win.shape[2])
    mem_t = jnp.transpose(cache_mem_kv, to_t).reshape(dp, db, 512, cache_mem_kv.shape[2])
    pool_t = jnp.transpose(state_pool, (0, 2, 1, 3))
    return dict(fox_t=fox_t, lf_t=lf_t, nsa_t=nsa_t, win_t=win_t, mem_t=mem_t, pool_t=pool_t)


def _layer_sample(x, nv, page_table, lw, fg, final, layer):
    db, ds, d = x.shape
    npg = page_table.shape[1]
    past = npg * PAGE_SIZE
    nblk = past // NSA_BLOCK
    pb, pz, aux, sf, sft, auxt = _project(x.reshape(db, d), lw["g"], lw["wp"], lw["bvec"],
                                          tm=db, tiles_per_seq=1, prompt=False)
    page_flat = page_table.reshape(db * npg).astype(I32)
    cmp = _compress_sample(page_flat, nv["nsa_t"], lw["pe_t"], lw["w1q"], lw["w2a"], lw["w2b"],
                           pages=min(db * npg, 128), layer=layer)
    cmp = jnp.transpose(cmp.reshape(2, db, npg, G_NSA, PAGE_SIZE // NSA_BLOCK, HEAD_DIM),
                        (0, 1, 2, 4, 3, 5)).reshape(2, db, nblk, LANES)
    kcmp, vcmp = cmp[0], cmp[1]
    expand = (np.arange(LANES)[:, None] == (np.arange(past)[None, :] // NSA_BLOCK)).astype(np.float32)
    oatt, win_new = _attend_sample(
        page_flat, nv["fox_t"], nv["lf_t"], nv["nsa_t"], pb.reshape(db, 1, N_PB),
        sf.reshape(db, 1, 1280), aux.reshape(db, 1, LANES), nv["win_t"], nv["mem_t"], kcmp, vcmp,
        jnp.asarray(expand, BF16), npg=npg, layer=layer)
    y, pool_new = _post_sample(oatt.reshape(db, 768), pz, nv["pool_t"], x.reshape(db, d), lw["wout"],
                               lw["pw_bd"], lw["ps"], fg, past=past, final=final, layer=layer)
    w_buf = win_new.shape[-1]
    seq_first = (3, 0, 1, 2)
    state = (jnp.transpose(sft[0:512].reshape(2, H_FOX, HEAD_DIM, db), seq_first)[:, None],
             jnp.transpose(auxt[0:H_FOX])[:, None, :],
             jnp.transpose(sft[512:1024].reshape(4, G_NSA, HEAD_DIM, db), seq_first)[:, None],
             jnp.transpose(win_new.reshape(db, 2, G_NSA, HEAD_DIM, w_buf), (0, 4, 1, 2, 3)),
             jnp.transpose(pool_new, (1, 0, 2)))
    return y.reshape(db, ds, d), state


def kernel(x_prompt, x_sample, cache_fox_kv, cache_fox_lf, cache_nsa_kv, state_nsa_win, state_pool,
           cache_mem_kv, page_table, mem_prompt, norm_g, w_in, b_fgt, nsa_pe, nsa_w1, nsa_w2, pool_w,
           pool_scale, w_mem_kv, w_out, final_g):
    depth = norm_g.shape[0]
    t = x_prompt.shape[1]
    assert x_sample.shape[1] == 1 and t % NSA_WINDOW == 0 and t // NSA_BLOCK <= NBLK_PAD
    hp, hs = x_prompt, x_sample
    fg = final_g.reshape(1, -1).astype(F32)
    nv = _native_views(cache_fox_kv, cache_fox_lf, cache_nsa_kv, state_nsa_win, state_pool,
                       cache_mem_kv)
    sp = [[] for _ in range(6)]
    ss = [[] for _ in range(5)]
    for l in range(depth):
        lw = _prep_layer_weights(norm_g[l], w_in[l], b_fgt[l], nsa_pe[l], nsa_w1[l], nsa_w2[l],
                                 pool_w[l], pool_scale[l], w_mem_kv[l], w_out[l])
        final = l == depth - 1
        hp, st_p = _layer_prompt(hp, mem_prompt, lw, fg, final, NSA_WINDOW)
        hs, st_s = _layer_sample(hs, nv, page_table, lw, fg, final, l)
        for acc, s in zip(sp, st_p):
            acc.append(s)
        for acc, s in zip(ss, st_s):
            acc.append(s)
    return (hp, hs) + tuple(jnp.stack(a) for a in sp) + tuple(jnp.stack(a) for a in ss)
```

```python
import functools

import numpy as np
import jax
import jax.numpy as jnp
from jax import lax
from jax.experimental import pallas as pl
from jax.experimental.pallas import tpu as pltpu

F32 = jnp.float32
BF16 = jnp.bfloat16
I32 = jnp.int32

HEAD_DIM = 64
H_FOX = 4
H_NSA = 4
G_NSA = 2
H_MEM = 4
D_GROUP = 256
POOL_WINDOWS = (2, 4, 8, 16)
POOL_BUF = 15
NSA_BLOCK = 64
NSA_TOPN = 16
NSA_WINDOW = 512
PAGE_SIZE = 128
RMS_EPS = 1e-6
NEG_INF = -1e30
FORCE_SCORE = 1e4
SCALE = HEAD_DIM ** -0.5
ALIBI_SLOPES = tuple(2.0 ** (-8.0 * (h + 1) / H_NSA) for h in range(H_NSA))

LANES = 128
NBLK_PAD = 128
VMEM_LIMIT = 56 * 1024 * 1024

C_FQ, C_FK, C_FV, C_NQ, C_NKV, C_MQ, C_PU, C_Z, C_SM, C_END = (
    0, 256, 512, 768, 1024, 1792, 2048, 2304, 3328, 3456)
N_PB = C_PU
LANE_GATE0 = 4


def _nt(a, b):
    return lax.dot_general(a, b, (((1,), (1,)), ((), ())), preferred_element_type=F32)


def _mm(a, b):
    return jnp.dot(a, b, preferred_element_type=F32)


def _split3(x):
    p1 = x.astype(BF16)
    r1 = x - p1.astype(F32)
    p2 = r1.astype(BF16)
    p3 = (r1 - p2.astype(F32)).astype(BF16)
    return p1, p2, p3


def _exact_mm(m01, x):
    p1, p2, p3 = _split3(x)
    return _mm(m01, p1) + _mm(m01, p2) + _mm(m01, p3)


def _sigmoid(x):
    return 1.0 / (1.0 + jnp.exp(-x))


def _log_sigmoid(x):
    return jnp.minimum(x, 0.0) - jnp.log1p(jnp.exp(-jnp.abs(x)))


def _iota(shape, dim):
    return lax.broadcasted_iota(I32, shape, dim)


def _div_pow2(x, n):
    assert n & (n - 1) == 0
    return jnp.right_shift(x, n.bit_length() - 1)


def _mod_pow2(x, n):
    assert n & (n - 1) == 0
    return jnp.bitwise_and(x, n - 1)


def _store_layer(ref, value, first_layer):
    if first_layer:
        ref[0] = value
        for k in range(1, ref.shape[0]):
            ref[k] = jnp.zeros(value.shape, value.dtype)
    else:
        ref[...] = value


def _proj_body(x_ref, g_ref, w_ref, b_ref, *rest, tm, tiles_per_seq, prompt, n_alias):
    pb_ref, pz_ref, aux_ref, *rest = rest[n_alias:]
    x = x_ref[...]
    ms = jnp.mean(x * x, axis=-1, keepdims=True)
    xn = (x * lax.rsqrt(ms + RMS_EPS) * g_ref[...]).astype(BF16)

    def seg(a, b):
        return _mm(xn, w_ref[:, a:b])

    pb_ref[:, C_FQ:C_FK] = (seg(C_FQ, C_FK) * SCALE).astype(BF16)
    fkv = seg(C_FK, C_NQ)
    pb_ref[:, C_FK:C_NQ] = fkv.astype(BF16)
    pb_ref[:, C_NQ:C_NKV] = (seg(C_NQ, C_NKV) * SCALE).astype(BF16)
    nkv = seg(C_NKV, C_MQ)
    pb_ref[:, C_NKV:C_MQ] = nkv.astype(BF16)
    pb_ref[:, C_MQ:C_PU] = (seg(C_MQ, C_PU) * SCALE).astype(BF16)
    pz_ref[...] = seg(C_PU, C_SM)
    small = seg(C_SM, C_END)

    lane = _iota((tm, LANES), 1)
    lf = _log_sigmoid(small + b_ref[...])
    aux = jnp.where(lane < H_FOX, lf, jnp.where(lane < LANE_GATE0 + 3 * H_NSA, _sigmoid(small), 0.0))
    aux_ref[...] = aux
    if prompt:
        kvc_ref, ct_ref, lft_ref, fkvt_ref, nkvt_ref, nwt_ref, carry_ref = rest
        kvc_ref[0] = nkv[:, 0:LANES]
        kvc_ref[1] = nkv[:, LANES:256]
        _store_layer(fkvt_ref, fkv.T, n_alias == 0)
        _store_layer(nkvt_ref, nkv[:, 0:512].T, n_alias == 0)
        nwt_ref[...] = nkv[:, 512:768].T
        lfm = jnp.where(lane < H_FOX, lf, 0.0)
        lft_ref[...] = lfm.T[0:8, :]
        tri = (_iota((LANES, LANES), 0) >= _iota((LANES, LANES), 1)).astype(BF16)
        first = (pl.program_id(0) % tiles_per_seq) == 0
        carry = jnp.where(first, 0.0, carry_ref[0:1, :])
        for r in range(tm // LANES):
            blk = lfm[r * LANES:(r + 1) * LANES]
            cblk = _exact_mm(tri, blk) + carry
            carry = cblk[LANES - 1:LANES, :]
            ct_ref[:, r * LANES:(r + 1) * LANES] = cblk.T[0:8, :]
        carry_ref[0:1, :] = carry
    else:
        sf_ref, sft_ref, auxt_ref = rest
        sf_ref[:, 0:512] = fkv
        sf_ref[:, 512:1280] = nkv
        sft_ref[0:512, :] = fkv.T
        sft_ref[512:1280, :] = nkv.T
        auxt_ref[...] = aux.T


def _stacked_spec(depth, layer, blk, index):
    if layer == 0:
        return pl.BlockSpec((depth,) + blk, lambda *a: (0,) + index(*a))
    return pl.BlockSpec((None,) + blk, lambda *a: (layer,) + index(*a))


def _project(x2d, g, wp, bvec, *, tm, tiles_per_seq, prompt, layer=0, depth=1, stacked=()):
    n, d = x2d.shape
    grid = (n // tm,)
    row = lambda i: (i, 0)
    const = lambda i: (0, 0)
    out_shape = [jax.ShapeDtypeStruct((n, N_PB), BF16), jax.ShapeDtypeStruct((n, 1280), F32),
                 jax.ShapeDtypeStruct((n, LANES), F32)]
    out_specs = [pl.BlockSpec((tm, N_PB), row), pl.BlockSpec((tm, 1280), row),
                 pl.BlockSpec((tm, LANES), row)]
    scratch = []
    aliases = {}
    if prompt:
        assert len(stacked) == (2 if layer else 0)
        nseq = n // (tm * tiles_per_seq)
        t = tm * tiles_per_seq
        seq_idx = lambda i: (i // tiles_per_seq, 0, i % tiles_per_seq)
        seq_t = lambda rows: pl.BlockSpec((None, rows, tm), seq_idx)
        lay_t = lambda rows: _stacked_spec(depth, layer, (None, rows, tm), seq_idx)
        out_shape += [jax.ShapeDtypeStruct((2, n, LANES), F32),
                      jax.ShapeDtypeStruct((nseq, 8, t), F32), jax.ShapeDtypeStruct((nseq, 8, t), F32),
                      jax.ShapeDtypeStruct((depth, nseq, 512, t), F32),
                      jax.ShapeDtypeStruct((depth, nseq, 512, t), F32),
                      jax.ShapeDtypeStruct((nseq, 256, t), F32)]
        out_specs += [pl.BlockSpec((2, tm, LANES), lambda i: (0, i, 0)),
                      seq_t(8), seq_t(8), lay_t(512), lay_t(512), seq_t(256)]
        scratch = [pltpu.VMEM((8, LANES), F32)]
        aliases = {4 + k: 6 + k for k in range(len(stacked))}
    else:
        assert grid == (1,) and not stacked
        out_shape += [jax.ShapeDtypeStruct((n, 1280), F32), jax.ShapeDtypeStruct((1280, n), F32),
                      jax.ShapeDtypeStruct((LANES, n), F32)]
        out_specs += [pl.BlockSpec((tm, 1280), row), pl.BlockSpec((1280, tm), const),
                      pl.BlockSpec((LANES, tm), const)]
    return pl.pallas_call(
        functools.partial(_proj_body, tm=tm, tiles_per_seq=tiles_per_seq, prompt=prompt,
                          n_alias=len(stacked)),
        grid=grid,
        in_specs=[pl.BlockSpec((tm, d), row), pl.BlockSpec((1, d), const),
                  pl.BlockSpec((d, C_END), const), pl.BlockSpec((1, LANES), const)]
        + [pl.BlockSpec(memory_space=pl.ANY)] * len(stacked),
        out_specs=out_specs, out_shape=out_shape, scratch_shapes=scratch,
        input_output_aliases=aliases,
        compiler_params=pltpu.CompilerParams(dimension_semantics=("arbitrary",),
                                             vmem_limit_bytes=VMEM_LIMIT),
        name="proj",
    )(x2d, g, wp, bvec, *stacked)


def _memkv_body(m_ref, w_ref, ot_ref, ob_ref):
    r = _mm(m_ref[...].astype(BF16), w_ref[...])
    ot_ref[...] = r.T
    ob_ref[...] = r.astype(BF16)


def _mem_project(mem3, w):
    b, n_mem, d = mem3.shape
    e = w.shape[1]
    return pl.pallas_call(
        _memkv_body, grid=(b,),
        in_specs=[pl.BlockSpec((None, n_mem, d), lambda i: (i, 0, 0)),
                  pl.BlockSpec((d, e), lambda i: (0, 0))],
        out_specs=[pl.BlockSpec((None, e, n_mem), lambda i: (i, 0, 0)),
                   pl.BlockSpec((None, n_mem, e), lambda i: (i, 0, 0))],
        out_shape=[jax.ShapeDtypeStruct((b, e, n_mem), F32), jax.ShapeDtypeStruct((b, n_mem, e), BF16)],
        name="memkv",
    )(mem3, w)


def _fox_body(qt_ref, kt_ref, q_ref, k_ref, v_ref, ct_ref, o_ref, m_ref, l_ref, acc_ref, *, tq, tk):
    qi = qt_ref[pl.program_id(1)]
    ki = kt_ref[pl.program_id(1)]

    @pl.when(ki == 0)
    def _():
        m_ref[...] = jnp.full(m_ref.shape, NEG_INF, F32)
        l_ref[...] = jnp.zeros(l_ref.shape, F32)
        acc_ref[...] = jnp.zeros(acc_ref.shape, F32)

    lane = _iota((tq, LANES), 1)

    def step(masked):
        if masked:
            causal = _iota((tq, tk), 0) >= _iota((tq, tk), 1)
        scores = []
        for h in range(H_FOX):
            pr, hh = divmod(h, 2)
            cs = slice(pr * LANES, (pr + 1) * LANES)
            q2 = q_ref[:, cs]
            qm = jnp.where((lane >= 64 * hh) & (lane < 64 * hh + 64), q2, jnp.zeros_like(q2))
            scores.append(_nt(qm, k_ref[:, cs]))
        probs = []
        for h in range(H_FOX):
            s = scores[h] - ct_ref[h:h + 1, :]
            if masked:
                s = jnp.where(causal, s, NEG_INF)
            m_prev = m_ref[h]
            m_new = jnp.maximum(m_prev, jnp.max(s, axis=1, keepdims=True))
            alpha = jnp.exp(m_prev - m_new)
            p = jnp.exp(s - jnp.concatenate([m_new] * (tk // LANES), axis=1))
            l_ref[h] = alpha * l_ref[h] + jnp.sum(p, axis=1, keepdims=True)
            m_ref[h] = m_new
            probs.append((p.astype(BF16), alpha))
        for h in range(H_FOX):
            cs = slice((h // 2) * LANES, (h // 2 + 1) * LANES)
            p, alpha = probs[h]
            acc_ref[h] = alpha * acc_ref[h] + _mm(p, v_ref[:, cs])

    @pl.when(ki < qi)
    def _():
        step(False)

    @pl.when(ki == qi)
    def _():
        step(True)
        for pr in range(H_FOX // 2):
            o0 = acc_ref[2 * pr] / l_ref[2 * pr]
            o1 = acc_ref[2 * pr + 1] / l_ref[2 * pr + 1]
            o_ref[:, pr * LANES:(pr + 1) * LANES] = jnp.where(lane < 64, o0, o1)


def _causal_pairs(nq):
    pairs = [(qi, ki) for qi in range(nq) for ki in range(qi + 1)]
    return (jnp.asarray([p[0] for p in pairs], I32), jnp.asarray([p[1] for p in pairs], I32))


def _fox_prompt(pb3, ct, *, tq):
    b, t, _ = pb3.shape
    qt, kt = _causal_pairs(t // tq)
    grid_spec = pltpu.PrefetchScalarGridSpec(
        num_scalar_prefetch=2, grid=(b, qt.shape[0]),
        in_specs=[pl.BlockSpec((None, tq, 256), lambda bi, s, qt, kt: (bi, qt[s], C_FQ // 256)),
                  pl.BlockSpec((None, tq, 256), lambda bi, s, qt, kt: (bi, kt[s], C_FK // 256)),
                  pl.BlockSpec((None, tq, 256), lambda bi, s, qt, kt: (bi, kt[s], C_FV // 256)),
                  pl.BlockSpec((None, 8, tq), lambda bi, s, qt, kt: (bi, 0, kt[s]))],
        out_specs=pl.BlockSpec((None, tq, 256), lambda bi, s, qt, kt: (bi, qt[s], 0)),
        scratch_shapes=[pltpu.VMEM((H_FOX, tq, LANES), F32), pltpu.VMEM((H_FOX, tq, LANES), F32),
                        pltpu.VMEM((H_FOX, tq, LANES), F32)])
    return pl.pallas_call(
        functools.partial(_fox_body, tq=tq, tk=tq),
        grid_spec=grid_spec,
        out_shape=jax.ShapeDtypeStruct((b, t, 256), F32),
        compiler_params=pltpu.CompilerParams(
            dimension_semantics=("parallel", "arbitrary"), vmem_limit_bytes=VMEM_LIMIT),
        name="fox_prompt",
    )(qt, kt, pb3, pb3, pb3, ct)


def _cmp_body(x_ref, pe_ref, w1_ref, w2_ref, o_ref, acc_ref, *, m):
    lo = _iota((m, LANES), 1) < HEAD_DIM
    acc_ref[...] = jnp.zeros(acc_ref.shape, F32)
    for pos in range(NSA_BLOCK):
        x = x_ref[pl.ds(pos, m, stride=NSA_BLOCK), :] + pe_ref[pos:pos + 1, :]
        a = jnp.concatenate([jnp.where(lo, x, 0.0), jnp.where(lo, 0.0, x)], axis=0)
        acc_ref[...] += _mm(a.astype(BF16), w1_ref[pos])
    h = acc_ref[...]
    o_ref[...] = _mm((h * _sigmoid(h)).astype(BF16), w2_ref[...])


def _compress_prompt(kvc, pe2, w1dup, w2):
    _, n, _ = kvc.shape
    m = n // NSA_BLOCK
    by_j = lambda blk: pl.BlockSpec((None,) + blk, lambda j: (j,) + (0,) * len(blk))
    return pl.pallas_call(
        functools.partial(_cmp_body, m=m), grid=(2,),
        in_specs=[by_j((n, LANES)), by_j((NSA_BLOCK, LANES)), by_j((NSA_BLOCK, LANES, 256)),
                  by_j((256, HEAD_DIM))],
        out_specs=by_j((G_NSA * m, HEAD_DIM)),
        out_shape=jax.ShapeDtypeStruct((2, G_NSA * m, HEAD_DIM), F32),
        scratch_shapes=[pltpu.VMEM((G_NSA * m, 256), F32)],
        compiler_params=pltpu.CompilerParams(vmem_limit_bytes=VMEM_LIMIT),
        name="cmp_prompt",
    )(kvc, pe2, w1dup, w2)


def _masked_softmax_cols(s, valid):
    s = jnp.where(valid, s, NEG_INF)
    e = jnp.where(valid, jnp.exp(s - jnp.max(s, axis=0, keepdims=True)), 0.0)
    return e / jnp.maximum(jnp.sum(e, axis=0, keepdims=True), 1e-30)


def _select_blocks(score, score_ref, n_live, topn):
    nbp, n = score.shape
    blk = _iota((nbp, n), 0)
    score_ref[...] = score

    def body(m, rank):
        r = score_ref[pl.ds(m, 1), :]
        return rank + jnp.where((r > score) | ((r == score) & (blk > m)), 1.0, 0.0)

    rank = lax.fori_loop(0, n_live, body, jnp.zeros((nbp, n), F32))
    return jnp.where((rank < topn) & (score >= 0.0), 1.0, 0.0)


def _nsa_body(qt_ref, kt_ref, q_ref, ks_ref, vs_ref, kwc_ref, kwp_ref, vwc_ref, vwp_ref, kc_ref, vc_ref,
              aux_ref, o_ref, sel_ref, m_ref, l_ref, acc_ref, oc_ref, ow_ref, score_ref, *, tq, nb, topn):
    qi = qt_ref[pl.program_id(1)]
    ki = kt_ref[pl.program_id(1)]
    q0 = qi * tq
    tk = tq
    lane = _iota((tq, LANES), 1)

    def qstack(g):
        q2 = q_ref[:, g * LANES:(g + 1) * LANES].astype(F32)
        q2r = pltpu.roll(q2, 64, 1)
        ing = (lane >= 64 * g) & (lane < 64 * g + 64)
        qa, qb = (q2, q2r) if g == 0 else (q2r, q2)
        return jnp.concatenate([jnp.where(ing, qa, 0.0), jnp.where(ing, qb, 0.0)],
                               axis=0).astype(BF16)

    def biased(raw, g, kd, valid):
        sa = jnp.where(valid, raw[:tq] + ALIBI_SLOPES[2 * g] * kd, NEG_INF)
        sb = jnp.where(valid, raw[tq:] + ALIBI_SLOPES[2 * g + 1] * kd, NEG_INF)
        return jnp.concatenate([sa, sb], axis=0)

    def lanes(x, n):
        return jnp.concatenate([x] * (n // LANES), axis=1)

    @pl.when(ki == 0)
    def _():
        m_ref[...] = jnp.full(m_ref.shape, NEG_INF, F32)
        l_ref[...] = jnp.zeros(l_ref.shape, F32)
        acc_ref[...] = jnp.zeros(acc_ref.shape, F32)
        dqk = _iota((tq, tk), 0) - _iota((tq, tk), 1)
        kd_cur = _iota((1, tk), 1).astype(F32)
        for g in range(G_NSA):
            qs = qstack(g)
            st = _nt(kc_ref[...], qs)
            blk = _iota((NBLK_PAD, 2 * tq), 0)
            col = _iota((NBLK_PAD, 2 * tq), 1)
            qpos = q0 + jnp.where(col >= tq, col - tq, col)
            slope = jnp.where(col < tq, ALIBI_SLOPES[2 * g], ALIBI_SLOPES[2 * g + 1])
            cend = blk * NSA_BLOCK + (NSA_BLOCK - 1)
            st = st + slope * (cend - q0).astype(F32)
            pt = _masked_softmax_cols(st, (cend <= qpos) & (blk < nb))
            oc_ref[g] = _mm(pt.T.astype(BF16), vc_ref[...])
            imp = pt[:, :tq] + pt[:, tq:]
            nbp = -(-nb // 8) * 8
            blk2 = _iota((nbp, tq), 0)
            cur = _div_pow2(q0 + _iota((nbp, tq), 1), NSA_BLOCK)
            forced = (blk2 == 0) | (blk2 == cur) | (blk2 == cur - 1)
            score = jnp.where(blk2 > cur, -1.0, jnp.where(forced, FORCE_SCORE, imp[:nbp]))
            n_live = jnp.minimum(_div_pow2(q0 + tq, NSA_BLOCK), nb)
            selt = _select_blocks(score, score_ref, n_live, topn)
            if nbp < NBLK_PAD:
                selt = jnp.concatenate([selt, jnp.zeros((NBLK_PAD - nbp, tq), F32)], axis=0)
            sel_ref[g] = selt.T.astype(BF16)
            s1 = biased(_nt(qs, kwc_ref[...]), g, kd_cur, dqk >= 0)
            s2 = biased(_nt(qs, kwp_ref[...]), g, kd_cur - tq, (dqk + tq <= NSA_WINDOW) & (qi > 0))
            mw = jnp.maximum(jnp.max(s1, axis=1, keepdims=True), jnp.max(s2, axis=1, keepdims=True))
            mw = jnp.broadcast_to(mw, (2 * tq, LANES))
            e1 = jnp.exp(s1 - lanes(mw, tk))
            e2 = jnp.exp(s2 - lanes(mw, tk))
            lw = jnp.sum(e1, axis=1, keepdims=True) + jnp.sum(e2, axis=1, keepdims=True)
            ow_ref[g] = (_mm(e1.astype(BF16), vwc_ref[...]) + _mm(e2.astype(BF16), vwp_ref[...])) \
                / jnp.maximum(lw, 1e-30)

    def sel_step(diag):
        expand = (_iota((NBLK_PAD, tk), 0) ==
                  ki * (tk // NSA_BLOCK) + _div_pow2(_iota((NBLK_PAD, tk), 1), NSA_BLOCK)).astype(BF16)
        kd = (_iota((1, tk), 1) + (ki * tk - q0)).astype(F32)
        raws = [(_nt(qstack(g), ks_ref[...]), _mm(sel_ref[g], expand)) for g in range(G_NSA)]
        probs = []
        for g in range(G_NSA):
            raw, picked = raws[g]
            valid = picked > 0.5
            if diag:
                valid = valid & (_iota((tq, tk), 0) >= _iota((tq, tk), 1))
            s = biased(raw, g, kd, valid)
            m_prev = m_ref[g]
            m_new = jnp.maximum(m_prev, jnp.max(s, axis=1, keepdims=True))
            alpha = jnp.exp(m_prev - m_new)
            p = jnp.exp(s - lanes(m_new, tk))
            l_ref[g] = alpha * l_ref[g] + jnp.sum(p, axis=1, keepdims=True)
            m_ref[g] = m_new
            probs.append((p.astype(BF16), alpha))
        for g in range(G_NSA):
            p, alpha = probs[g]
            acc_ref[g] = alpha * acc_ref[g] + _mm(p, vs_ref[...])

    @pl.when(ki < qi)
    def _():
        sel_step(False)

    @pl.when(ki == qi)
    def _():
        sel_step(True)
        for g in range(G_NSA):
            o_s = acc_ref[g] / jnp.maximum(l_ref[g], 1e-30)
            o_c = oc_ref[g]
            o_w = ow_ref[g]
            outs = []
            for hh in range(2):
                h = 2 * g + hh
                rs = slice(hh * tq, (hh + 1) * tq)
                gl = LANE_GATE0 + 3 * h
                outs.append(aux_ref[:, gl:gl + 1] * o_c[rs] + aux_ref[:, gl + 1:gl + 2] * o_s[rs]
                            + aux_ref[:, gl + 2:gl + 3] * o_w[rs])
            oa, ob = outs
            if g == 0:
                ob = pltpu.roll(ob, 64, 1)
            else:
                oa = pltpu.roll(oa, 64, 1)
            o_ref[:, g * LANES:(g + 1) * LANES] = jnp.where(lane < 64, oa, ob)


def _nsa_prompt(pb3, kcmp, vcmp, aux3, *, tq, nb):
    b, t, _ = pb3.shape
    qt, kt = _causal_pairs(t // tq)
    ct = C_NKV // LANES
    cur = lambda c: (lambda bi, s, qt, kt: (bi, qt[s], c))
    prev = lambda c: (lambda bi, s, qt, kt: (bi, jnp.maximum(qt[s] - 1, 0), c))
    kvt = lambda c: (lambda bi, s, qt, kt: (bi, kt[s], c))
    cmp_spec = pl.BlockSpec((None, NBLK_PAD, LANES), lambda bi, s, qt, kt: (bi, 0, 0))
    tile = lambda f: pl.BlockSpec((None, tq, LANES), f)
    grid_spec = pltpu.PrefetchScalarGridSpec(
        num_scalar_prefetch=2, grid=(b, qt.shape[0]),
        in_specs=[pl.BlockSpec((None, tq, 256), lambda bi, s, qt, kt: (bi, qt[s], C_NQ // 256)),
                  tile(kvt(ct + 2)), tile(kvt(ct + 3)),
                  tile(cur(ct + 4)), tile(prev(ct + 4)), tile(cur(ct + 5)), tile(prev(ct + 5)),
                  cmp_spec, cmp_spec,
                  pl.BlockSpec((None, tq, LANES), lambda bi, s, qt, kt: (bi, qt[s], 0))],
        out_specs=pl.BlockSpec((None, tq, 256), lambda bi, s, qt, kt: (bi, qt[s], 0)),
        scratch_shapes=[pltpu.VMEM((G_NSA, tq, NBLK_PAD), BF16),
                        pltpu.VMEM((G_NSA, 2 * tq, LANES), F32), pltpu.VMEM((G_NSA, 2 * tq, LANES), F32),
                        pltpu.VMEM((G_NSA, 2 * tq, LANES), F32),
                        pltpu.VMEM((G_NSA, 2 * tq, LANES), F32),
                        pltpu.VMEM((G_NSA, 2 * tq, LANES), F32),
                        pltpu.VMEM((-(-nb // 8) * 8, tq), F32)])
    return pl.pallas_call(
        functools.partial(_nsa_body, tq=tq, nb=nb, topn=min(NSA_TOPN, nb)),
        grid_spec=grid_spec,
        out_shape=jax.ShapeDtypeStruct((b, t, 256), F32),
        compiler_params=pltpu.CompilerParams(
            dimension_semantics=("parallel", "arbitrary"), vmem_limit_bytes=VMEM_LIMIT),
        name="nsa_prompt",
    )(qt, kt, pb3, pb3, pb3, pb3, pb3, pb3, pb3, kcmp, vcmp, aux3)


def _mem_attend(mq_ref, mk_ref, mv_ref, rows):
    lane = _iota((rows, LANES), 1)
    n_mem = mk_ref.shape[0]
    scores = []
    for h in range(H_MEM):
        pr, hh = divmod(h, 2)
        cs = slice(pr * LANES, (pr + 1) * LANES)
        q2 = mq_ref[:, cs]
        qm = jnp.where((lane >= 64 * hh) & (lane < 64 * hh + 64), q2, jnp.zeros_like(q2))
        scores.append(_nt(qm, mk_ref[:, cs]))
    weights = []
    for h in range(H_MEM):
        s = scores[h]
        mx = jnp.broadcast_to(jnp.max(s, axis=1, keepdims=True), (rows, LANES))
        e = jnp.exp(s - jnp.concatenate([mx] * (n_mem // LANES), axis=1))
        weights.append((e.astype(BF16), jnp.sum(e, axis=1, keepdims=True)))
    outs = []
    for h in range(H_MEM):
        e, den = weights[h]
        cs = slice((h // 2) * LANES, (h // 2 + 1) * LANES)
        outs.append(_mm(e, mv_ref[:, cs]) / den)
    return [jnp.where(lane < 64, outs[2 * pr], outs[2 * pr + 1]) for pr in range(H_MEM // 2)]


def _mix_out(x, parts, z_of, wout_ref, fg_ref, final):
    y = x
    for c in range(0, len(parts), 2):
        gated = []
        for cc in (c, c + 1):
            z = z_of(cc)
            gated.append((parts[cc] * (z * _sigmoid(z))).astype(BF16))
        y = y + _mm(jnp.concatenate(gated, axis=1), wout_ref[c * LANES:(c + 2) * LANES, :])
    if final:
        y = y * lax.rsqrt(jnp.mean(y * y, axis=-1, keepdims=True) + RMS_EPS) * fg_ref[...]
    return y


def _pool_window_lane(shape):
    grp = _div_pow2(_iota(shape, 1), D_GROUP // len(POOL_WINDOWS))
    wl = jnp.where(grp == 0, POOL_WINDOWS[0], jnp.where(grp == 1, POOL_WINDOWS[1],
                   jnp.where(grp == 2, POOL_WINDOWS[2], POOL_WINDOWS[3])))
    return grp, wl


HALO = 32


def _post_body(oa_ref, ob_ref, pz_ref, halo_ref, mq_ref, mk_ref, mv_ref, x_ref, wout_ref, pw_ref,
               ps_ref, fg_ref, y_ref, s0, s1, s2, s3, *, tm, final):
    i = pl.program_id(1)
    u = pz_ref[:, 0:256]
    s0[0:HALO, :] = jnp.where(i > 0, halo_ref[...], 0.0)
    s0[HALO:HALO + tm, :] = u
    n = tm + HALO
    s1[8:n, :] = s0[8:n, :] + s0[7:n - 1, :]
    s2[16:n, :] = s1[16:n, :] + s1[14:n - 2, :]
    s3[24:n, :] = s2[24:n, :] + s2[20:n - 4, :]
    a16 = s3[HALO:n, :] + s3[HALO - 8:n - 8, :]
    grp, wl = _pool_window_lane((tm, 256))
    tsum = jnp.where(grp == 0, s1[HALO:n, :], jnp.where(grp == 1, s2[HALO:n, :],
                     jnp.where(grp == 2, s3[HALO:n, :], a16)))
    pos = i * tm + _iota((tm, 256), 0)
    cnt = jnp.minimum(pos + 1, wl).astype(F32)
    o_pool = _mm((tsum / cnt - u).astype(BF16), pw_ref[...]) * ps_ref[...]
    o_mem = _mem_attend(mq_ref, mk_ref, mv_ref, tm)
    parts = [oa_ref[:, 0:LANES], oa_ref[:, LANES:256], ob_ref[:, 0:LANES], ob_ref[:, LANES:256],
             o_pool[:, 0:LANES], o_pool[:, LANES:256], o_mem[0], o_mem[1]]
    z_of = lambda c: pz_ref[:, 256 + c * LANES:256 + (c + 1) * LANES]
    y_ref[...] = _mix_out(x_ref[...], parts, z_of, wout_ref, fg_ref, final)


def _post_prompt(oa, ob, pz3, pb3, memb, x3, wout, pw_bd, ps, fg, *, tm, final):
    b, t, d = x3.shape
    nt = t // tm
    row = lambda bi, i: (bi, i, 0)
    const = lambda bi, i: (0, 0)
    return pl.pallas_call(
        functools.partial(_post_body, tm=tm, final=final),
        grid=(b, nt),
        in_specs=[pl.BlockSpec((None, tm, 256), row), pl.BlockSpec((None, tm, 256), row),
                  pl.BlockSpec((None, tm, 1280), row),
                  pl.BlockSpec((None, HALO, 256),
                               lambda bi, i: (bi, jnp.maximum(i * (tm // HALO) - 1, 0), 0)),
                  pl.BlockSpec((None, tm, 256), lambda bi, i: (bi, i, C_MQ // 256)),
                  pl.BlockSpec((None, 256, 256), lambda bi, i: (bi, 0, 0)),
                  pl.BlockSpec((None, 256, 256), lambda bi, i: (bi, 0, 1)),
                  pl.BlockSpec((None, tm, d), row),
                  pl.BlockSpec((d, d), const), pl.BlockSpec((256, 256), const),
                  pl.BlockSpec((1, 256), const), pl.BlockSpec((1, d), const)],
        out_specs=pl.BlockSpec((None, tm, d), row),
        out_shape=jax.ShapeDtypeStruct((b, t, d), F32),
        scratch_shapes=[pltpu.VMEM((tm + HALO, 256), F32)] * 4,
        compiler_params=pltpu.CompilerParams(dimension_semantics=("parallel", "arbitrary"),
                                             vmem_limit_bytes=VMEM_LIMIT),
        name="post_prompt",
    )(oa, ob, pz3, pz3, pb3, memb, memb, x3, wout, pw_bd, ps, fg)


CMP_PITCH = 72
DQ = 4


def _cmp_sample_body(pt_ref, nsa_hbm, pe_ref, w1_ref, w2a_ref, w2b_ref, o_ref, xbuf, acc0, acc1, sem,
                     *, pages, layer):
    j = pl.program_id(0)
    i = pl.program_id(1)
    ni = pl.num_programs(1)
    s = j * ni + i
    m = pages * G_NSA

    def chunk_copy(jj, ii, slot, p, g):
        page = pt_ref[ii * pages + p]
        return pltpu.make_async_copy(
            nsa_hbm.at[layer, page, pl.ds(jj * LANES + g * HEAD_DIM, HEAD_DIM), :],
            xbuf.at[slot, pl.ds((p * G_NSA + g) * CMP_PITCH, HEAD_DIM), :], sem.at[slot])

    def for_chunks(jj, ii, slot, fn):
        def body(p, carry):
            for g in range(G_NSA):
                fn(chunk_copy(jj, ii, slot, p, g))
            return carry
        lax.fori_loop(0, pages, body, 0)

    @pl.when(s == 0)
    def _():
        for_chunks(0, 0, 0, lambda c: c.start())

    @pl.when(s + 1 < 2 * ni)
    def _():
        wrap = i + 1 == ni
        for_chunks(jnp.where(wrap, j + 1, j), jnp.where(wrap, 0, i + 1), (s + 1) % 2,
                   lambda c: c.start())

    slot = s % 2
    for_chunks(j, i, slot, lambda c: c.wait())

    lo = _iota((m, LANES), 1) < HEAD_DIM
    acc0[...] = jnp.zeros(acc0.shape, F32)
    acc1[...] = jnp.zeros(acc1.shape, F32)
    for dq in range(HEAD_DIM // DQ):
        xs = [xbuf[slot, pl.ds(DQ * dq + k, m, stride=CMP_PITCH), :] + pe_ref[DQ * dq + k:DQ * dq + k + 1, :]
              for k in range(DQ)]
        xr = [pltpu.roll(x, HEAD_DIM, 1) for x in xs]
        a0 = jnp.concatenate([jnp.where(lo, xs[0], xr[1]), jnp.where(lo, xs[2], xr[3])], axis=1)
        a1 = jnp.concatenate([jnp.where(lo, xr[0], xs[1]), jnp.where(lo, xr[2], xs[3])], axis=1)
        w = w1_ref[dq]
        acc0[...] += _mm(a0.astype(BF16), w)
        acc1[...] += _mm(a1.astype(BF16), w)
    h0 = acc0[...]
    h1 = acc1[...]
    h0 = (h0 * _sigmoid(h0)).astype(BF16)
    h1 = (h1 * _sigmoid(h1)).astype(BF16)
    o_ref[...] = _mm(h0, w2a_ref[...]) + _mm(h1, w2b_ref[...])


def _compress_sample(page_flat, nsa_t, pe_t, w1q, w2a, w2b, *, pages, layer):
    npages = page_flat.shape[0]
    nsteps = npages // pages
    m = pages * G_NSA
    by_j = lambda blk: pl.BlockSpec((None,) + blk, lambda j, i, pt: (j,) + (0,) * len(blk))
    grid_spec = pltpu.PrefetchScalarGridSpec(
        num_scalar_prefetch=1, grid=(2, nsteps),
        in_specs=[pl.BlockSpec(memory_space=pl.ANY), by_j((HEAD_DIM, LANES)),
                  by_j((HEAD_DIM // DQ, 256, 256)), by_j((256, LANES)), by_j((256, LANES))],
        out_specs=pl.BlockSpec((None, m, LANES), lambda j, i, pt: (j, i, 0)),
        scratch_shapes=[pltpu.VMEM((2, m * CMP_PITCH, LANES), F32), pltpu.VMEM((m, 256), F32),
                        pltpu.VMEM((m, 256), F32), pltpu.SemaphoreType.DMA((2,))])
    return pl.pallas_call(
        functools.partial(_cmp_sample_body, pages=pages, layer=layer),
        grid_spec=grid_spec,
        out_shape=jax.ShapeDtypeStruct((2, npages * G_NSA, LANES), F32),
        compiler_params=pltpu.CompilerParams(dimension_semantics=("arbitrary", "arbitrary"),
                                             vmem_limit_bytes=VMEM_LIMIT),
        name="cmp_sample",
    )(page_flat, nsa_t, pe_t, w1q, w2a, w2b)


HROWS = 16


def _rows_from_lanes(vec_row, pick):
    return jnp.sum(jnp.where(pick, vec_row, 0.0), axis=1, keepdims=True)


def _bf16r(x):
    return x.astype(BF16).astype(F32)


def _att_sample_body(pt_ref, fox_hbm, lf_hbm, nsa_hbm, pb_ref, sf_ref, aux_ref, win_ref, mem_ref,
                     kc_ref, vc_ref, exp_ref, *rest, npg, past, layer):
    o_ref, wout_ref, fbuf, lbuf, sbuf, sem = rest[-6:]
    b = pl.program_id(0)
    nb = pl.num_programs(0)

    def copies(seq, slot):
        out = []
        for i in range(npg):
            page = pt_ref[seq * npg + i]
            out.append(pltpu.make_async_copy(
                fox_hbm.at[layer, page], fbuf.at[slot, i], sem.at[0, slot]))
            out.append(pltpu.make_async_copy(
                lf_hbm.at[layer, page], lbuf.at[slot, pl.ds(i * 8, 8), :], sem.at[1, slot]))
            out.append(pltpu.make_async_copy(
                nsa_hbm.at[layer, page, pl.ds(256, 256), :], sbuf.at[slot, i], sem.at[2, slot]))
        return out

    def slab(buf, r0, r1):
        return jnp.concatenate([buf[slot, i, r0:r1, :].astype(BF16) for i in range(npg)], axis=1)

    @pl.when(b == 0)
    def _():
        for c in copies(0, 0):
            c.start()

    @pl.when(b + 1 < nb)
    def _():
        for c in copies(b + 1, (b + 1) % 2):
            c.start()

    slot = b % 2
    for c in copies(b, slot):
        c.wait()

    row8 = _iota((HROWS, LANES), 0)
    lane8 = _iota((HROWS, LANES), 1)
    aux = aux_ref[...]
    pbr = pb_ref[...].astype(F32)
    sfr = sf_ref[...]

    def head_rows(q256):
        r = _iota((HROWS, 256), 0)
        l = _iota((HROWS, 256), 1)
        return jnp.where(_div_pow2(l, HEAD_DIM) == r, q256, 0.0).astype(BF16)

    def pick_heads(o8):
        r = _iota((HROWS, 256), 0)
        l = _iota((HROWS, 256), 1)
        return jnp.sum(jnp.where(_div_pow2(l, HEAD_DIM) == r, o8, 0.0), axis=0, keepdims=True)

    kt = slab(fbuf, 0, 256)
    vt = slab(fbuf, 256, 512)
    qf = head_rows(pbr[:, C_FQ:C_FK])
    s = _mm(qf, kt)
    lf2 = lbuf[slot]
    r2 = _iota((npg * 8, npg * 8), 0)
    c2 = _iota((npg * 8, npg * 8), 1)
    upper = (_iota((LANES, LANES), 0) > _iota((LANES, LANES), 1)).astype(BF16)
    within = _exact_mm_rhs(lf2, upper)
    tot = jnp.sum(lf2, axis=1, keepdims=True)
    later_pages = ((_mod_pow2(c2, 8) == _mod_pow2(r2, 8))
                   & (_div_pow2(c2, 8) > _div_pow2(r2, 8))).astype(BF16)
    later = _exact_mm(later_pages, jnp.broadcast_to(tot, (npg * 8, LANES)))
    rr = _iota((npg * 8, LANES), 0)
    ll = _iota((npg * 8, LANES), 1)
    lf_new = _rows_from_lanes(aux, ll == _mod_pow2(rr, 8))
    bias = within + later + lf_new
    zpad = jnp.zeros((HROWS - 8, LANES), F32)
    s = jnp.concatenate(
        [s[:, i * PAGE_SIZE:(i + 1) * PAGE_SIZE] + jnp.concatenate([bias[i * 8:(i + 1) * 8, :], zpad], axis=0)
         for i in range(npg)], axis=1)
    knew = _bf16r(sfr[:, 0:256])
    vnew = _bf16r(sfr[:, 256:512])
    s_new = jnp.sum(qf.astype(F32) * knew, axis=1, keepdims=True)
    m = jnp.maximum(jnp.max(s, axis=1, keepdims=True), s_new)
    e = jnp.exp(s - m)
    e_new = jnp.exp(s_new - m)
    den = jnp.sum(e, axis=1, keepdims=True) + e_new
    o8 = (_nt(e.astype(BF16), vt) + _bf16r(e_new) * vnew) / den
    o_ref[:, 0:256] = pick_heads(o8)

    nq = pbr[:, C_NQ:C_NKV]
    t0 = jnp.broadcast_to(nq[:, 0:LANES], (HROWS, LANES))
    t1 = jnp.broadcast_to(nq[:, LANES:256], (HROWS, LANES))
    t0r, t1r = pltpu.roll(t0, 64, 1), pltpu.roll(t1, 64, 1)
    lo = lane8 < 64
    hrows = [jnp.where(lo, t0, 0.0), jnp.where(lo, t0r, 0.0),
             jnp.where(lo, 0.0, t1r), jnp.where(lo, 0.0, t1)]
    qn = jnp.zeros((HROWS, LANES), F32)
    for h in range(H_NSA):
        qn = jnp.where(row8 == h, hrows[h], qn)
    qn_b = qn.astype(BF16)
    slope8 = jnp.zeros((HROWS, 1), F32)
    r81 = _iota((HROWS, 1), 0)
    for h in range(H_NSA):
        slope8 = jnp.where(r81 == h, ALIBI_SLOPES[h], slope8)
    nkv_new = sfr[:, 512:1280]
    nblk = past // NSA_BLOCK

    def new_score(k128):
        return jnp.sum(qn_b.astype(F32) * _bf16r(k128), axis=1, keepdims=True)

    sc = _nt(qn_b, kc_ref[...].astype(BF16))
    cend = _iota((HROWS, nblk), 1) * NSA_BLOCK + (NSA_BLOCK - 1)
    sc = sc - slope8 * (past - cend).astype(F32)
    ec = jnp.exp(sc - jnp.max(sc, axis=1, keepdims=True))
    pc = ec / jnp.maximum(jnp.sum(ec, axis=1, keepdims=True), 1e-30)
    o_c = _mm(pc.astype(BF16), vc_ref[...].astype(BF16))
    imp = jnp.concatenate([pc[2 * g:2 * g + 1] + pc[2 * g + 1:2 * g + 2] for g in range(G_NSA)]
                          + [jnp.zeros((HROWS - G_NSA, nblk), F32)], axis=0)
    if nblk < LANES:
        imp = jnp.concatenate([imp, jnp.zeros((HROWS, LANES - nblk), F32)], axis=1)
    forced = (lane8 == 0) | (lane8 == nblk) | (lane8 == nblk - 1)
    score = jnp.where(lane8 > nblk, -1.0, jnp.where(forced, FORCE_SCORE, imp))
    rank = jnp.zeros((HROWS, LANES), F32)
    for mblk in range(nblk + 1):
        r = score[:, mblk:mblk + 1]
        rank = rank + jnp.where((r > score) | ((r == score) & (lane8 > mblk)), 1.0, 0.0)
    sel_g = jnp.where((rank < min(NSA_TOPN, nblk + 1)) & (score >= 0.0), 1.0, 0.0)
    sel_h = jnp.zeros((HROWS, LANES), F32)
    for h in range(H_NSA):
        sel_h = jnp.where(row8 == h, sel_g[h // 2:h // 2 + 1, :], sel_h)
    valid = _mm(sel_h.astype(BF16), exp_ref[...]) > 0.5
    kst = slab(sbuf, 0, LANES)
    vst = slab(sbuf, LANES, 256)
    ss = _mm(qn_b, kst) - slope8 * (past - _iota((HROWS, past), 1)).astype(F32)
    ss = jnp.where(valid, ss, NEG_INF)
    ss_new = new_score(nkv_new[:, 256:384])
    ms = jnp.maximum(jnp.max(ss, axis=1, keepdims=True), ss_new)
    es = jnp.where(valid, jnp.exp(ss - ms), 0.0)
    es_new = jnp.exp(ss_new - ms)
    o_s = (_nt(es.astype(BF16), vst) + _bf16r(es_new) * _bf16r(nkv_new[:, 384:512])) \
        / (jnp.sum(es, axis=1, keepdims=True) + es_new)
    wl_ = win_ref.shape[1]
    win = win_ref[...]
    sw = _mm(qn_b, win[0:LANES].astype(BF16)) - slope8 * (wl_ - _iota((HROWS, wl_), 1)).astype(F32)
    sw_new = new_score(nkv_new[:, 512:640])
    mw = jnp.maximum(jnp.max(sw, axis=1, keepdims=True), sw_new)
    ew = jnp.exp(sw - mw)
    ew_new = jnp.exp(sw_new - mw)
    o_w = (_nt(ew.astype(BF16), win[LANES:256].astype(BF16))
           + _bf16r(ew_new) * _bf16r(nkv_new[:, 640:768])) / (jnp.sum(ew, axis=1, keepdims=True) + ew_new)
    rw = _iota((256, 256), 0)
    lw_ = _iota((256, 256), 1)
    new_col = jnp.sum(jnp.where(rw == lw_, nkv_new[:, 512:768], 0.0), axis=1, keepdims=True)
    _store_layer(wout_ref, jnp.where(_iota((256, wl_), 1) == wl_ - 1, new_col, pltpu.roll(win, wl_ - 1, 1)),
                 layer == 0)
    gates = [_rows_from_lanes(aux, lane8 == LANE_GATE0 + 3 * row8 + c) for c in range(3)]
    o_n = gates[0] * o_c + gates[1] * o_s + gates[2] * o_w
    o_nr = pltpu.roll(o_n, 64, 1)
    lo1 = _iota((1, LANES), 1) < 64
    o_ref[:, 256:384] = jnp.where(lo1, o_n[0:1], o_nr[1:2])
    o_ref[:, 384:512] = jnp.where(lo1, o_nr[2:3], o_n[3:4])

    qm = head_rows(pbr[:, C_MQ:C_PU])
    sm = _mm(qm, mem_ref[0:256, :].astype(BF16))
    em = jnp.exp(sm - jnp.max(sm, axis=1, keepdims=True))
    pm = em / jnp.sum(em, axis=1, keepdims=True)
    o_ref[:, 512:768] = pick_heads(_nt(pm.astype(BF16), mem_ref[256:512, :].astype(BF16)))


def _exact_mm_rhs(x, m01):
    p1, p2, p3 = _split3(x)
    return _mm(p1, m01) + _mm(p2, m01) + _mm(p3, m01)


def _attend_sample(page_flat, fox_t, lf_t, nsa_t, pb3, sf3, aux3, win_t, mem_t, kcmp, vcmp, expand,
                   *, npg, layer, stacked=()):
    depth = win_t.shape[0]
    db = pb3.shape[0]
    past = npg * PAGE_SIZE
    assert len(stacked) == (1 if layer else 0)
    per_seq = lambda blk: pl.BlockSpec((None,) + blk, lambda b, pt: (b, 0, 0))
    per_seq_layer = lambda blk: pl.BlockSpec((None, None) + blk, lambda b, pt: (layer, b, 0, 0))
    grid_spec = pltpu.PrefetchScalarGridSpec(
        num_scalar_prefetch=1, grid=(db,),
        in_specs=[pl.BlockSpec(memory_space=pl.ANY), pl.BlockSpec(memory_space=pl.ANY),
                  pl.BlockSpec(memory_space=pl.ANY),
                  per_seq((1, N_PB)), per_seq((1, 1280)), per_seq((1, LANES)),
                  per_seq_layer(win_t.shape[2:]), per_seq_layer(mem_t.shape[2:]),
                  per_seq(kcmp.shape[1:]), per_seq(vcmp.shape[1:]),
                  pl.BlockSpec(expand.shape, lambda b, pt: (0, 0))]
        + [pl.BlockSpec(memory_space=pl.ANY)] * len(stacked),
        out_specs=[per_seq((1, 768)),
                   _stacked_spec(depth, layer, (None,) + win_t.shape[2:], lambda b, pt: (b, 0, 0))],
        scratch_shapes=[pltpu.VMEM((2, npg, 512, PAGE_SIZE), F32), pltpu.VMEM((2, npg * 8, LANES), F32),
                        pltpu.VMEM((2, npg, 256, PAGE_SIZE), F32), pltpu.SemaphoreType.DMA((3, 2))])
    return pl.pallas_call(
        functools.partial(_att_sample_body, npg=npg, past=past, layer=layer),
        grid_spec=grid_spec,
        out_shape=[jax.ShapeDtypeStruct((db, 1, 768), F32), jax.ShapeDtypeStruct(win_t.shape, F32)],
        input_output_aliases={12 + k: 1 + k for k in range(len(stacked))},
        compiler_params=pltpu.CompilerParams(dimension_semantics=("arbitrary",),
                                             vmem_limit_bytes=VMEM_LIMIT),
        name="att_sample",
    )(page_flat, fox_t, lf_t, nsa_t, pb3, sf3, aux3, win_t, mem_t, kcmp, vcmp, expand, *stacked)


def _post_sample_body(oatt_ref, pz_ref, sp_ref, x_ref, wout_ref, pw_ref, ps_ref, fg_ref, y_ref,
                      spo_ref, *, rows, past, final):
    u = pz_ref[:, 0:256]
    grp, wl = _pool_window_lane((rows, 256))
    tsum = u
    for j in range(1, POOL_BUF + 1):
        tsum = tsum + jnp.where(wl > j, sp_ref[POOL_BUF - j], 0.0)
    for j in range(POOL_BUF - 1):
        spo_ref[j] = sp_ref[j + 1]
    spo_ref[POOL_BUF - 1] = u
    cnt = jnp.minimum(past + 1, wl).astype(F32)
    o_pool = _mm((tsum / cnt - u).astype(BF16), pw_ref[...]) * ps_ref[...]
    parts = [oatt_ref[:, c * LANES:(c + 1) * LANES] for c in range(4)]
    parts += [o_pool[:, 0:LANES], o_pool[:, LANES:256]]
    parts += [oatt_ref[:, 512:640], oatt_ref[:, 640:768]]
    z_of = lambda c: pz_ref[:, 256 + c * LANES:256 + (c + 1) * LANES]
    y_ref[...] = _mix_out(x_ref[...], parts, z_of, wout_ref, fg_ref, final)


def _post_sample(oatt, pz, pool_t, x2d, wout, pw_bd, ps, fg, *, past, final, layer):
    rows, d = x2d.shape
    full = lambda a: pl.BlockSpec(a.shape, lambda i: (0,) * a.ndim)
    return pl.pallas_call(
        functools.partial(_post_sample_body, rows=rows, past=past, final=final),
        grid=(1,),
        in_specs=[full(oatt), full(pz),
                  pl.BlockSpec((None, POOL_BUF, rows, 256), lambda i: (layer, 0, 0, 0)),
                  full(x2d), full(wout), full(pw_bd), full(ps), full(fg)],
        out_specs=[pl.BlockSpec((rows, d), lambda i: (0, 0)),
                   pl.BlockSpec((POOL_BUF, rows, 256), lambda i: (0, 0, 0))],
        out_shape=[jax.ShapeDtypeStruct((rows, d), F32),
                   jax.ShapeDtypeStruct((POOL_BUF, rows, 256), F32)],
        compiler_params=pltpu.CompilerParams(vmem_limit_bytes=VMEM_LIMIT),
        name="post_sample",
    )(oatt, pz, pool_t, x2d, wout, pw_bd, ps, fg)


def _prep_layer_weights(norm_g, w_in, b_fgt, nsa_pe, nsa_w1, nsa_w2, pool_w, pool_scale,
                        w_mem_kv, w_out):
    d = w_in.shape[0]
    o = 0
    segs = {}
    for name, width in (("fqkv", 768), ("ff", H_FOX), ("nq", 256), ("nkv", 768),
                        ("ng", 3 * H_NSA), ("pu", 256), ("mq", 256), ("z", d)):
        segs[name] = w_in[:, o:o + width]
        o += width
    pad = jnp.zeros((d, LANES - H_FOX - 3 * H_NSA), w_in.dtype)
    wp = jnp.concatenate([segs["fqkv"], segs["nq"], segs["nkv"], segs["mq"], segs["pu"],
                          segs["z"], segs["ff"], segs["ng"], pad], axis=1).astype(BF16)
    bvec = jnp.zeros((1, LANES), F32).at[0, 0:H_FOX].set(b_fgt.astype(F32))
    gw = D_GROUP // len(POOL_WINDOWS)
    pw_bd = jnp.zeros((D_GROUP, D_GROUP), F32)
    for g in range(len(POOL_WINDOWS)):
        pw_bd = pw_bd.at[g * gw:(g + 1) * gw, g * gw:(g + 1) * gw].set(pool_w[g])
    return dict(g=norm_g.reshape(1, d), wp=wp, bvec=bvec,
                pe2=jnp.concatenate([nsa_pe, nsa_pe], axis=2).astype(F32),
                w1dup=jnp.concatenate([nsa_w1.reshape(2, NSA_BLOCK, HEAD_DIM, -1)] * 2, axis=2).astype(BF16),
                w2=nsa_w2.astype(BF16),
                w1q=jnp.transpose(nsa_w1.reshape(2, NSA_BLOCK, HEAD_DIM, -1), (0, 2, 1, 3)).reshape(
                    2, HEAD_DIM // DQ, DQ * NSA_BLOCK, -1).astype(BF16),
                pe_t=jnp.concatenate([jnp.transpose(nsa_pe, (0, 2, 1))] * 2, axis=2).astype(F32),
                w2a=jnp.pad(nsa_w2, ((0, 0), (0, 0), (0, HEAD_DIM))).astype(BF16),
                w2b=jnp.pad(nsa_w2, ((0, 0), (0, 0), (HEAD_DIM, 0))).astype(BF16),
                pw_bd=pw_bd.astype(BF16), ps=pool_scale.reshape(1, D_GROUP).astype(F32),
                wmkv=w_mem_kv.astype(BF16), wout=w_out.astype(BF16))


def _layer_prompt(x, mem, lw, fg, tile, layer, depth, stacked):
    b, t, d = x.shape
    n = b * t
    nb = t // NSA_BLOCK
    final = layer == depth - 1
    pb, pz, aux, kvc, ct, lft, fkvt, nkvt, nwt = _project(
        x.reshape(n, d), lw["g"], lw["wp"], lw["bvec"], tm=tile, tiles_per_seq=t // tile, prompt=True,
        layer=layer, depth=depth, stacked=stacked)
    pb3 = pb.reshape(b, t, N_PB)
    o_a = _fox_prompt(pb3, ct, tq=tile)
    cmp = _compress_prompt(kvc, lw["pe2"], lw["w1dup"], lw["w2"]).reshape(2, G_NSA, b, nb, HEAD_DIM)
    cmp = jnp.transpose(cmp, (0, 2, 3, 1, 4)).reshape(2, b, nb, LANES)
    cmp = jnp.pad(cmp, ((0, 0), (0, 0), (0, NBLK_PAD - nb), (0, 0))).astype(BF16)
    o_b = _nsa_prompt(pb3, cmp[0], cmp[1], aux.reshape(b, t, LANES), tq=tile, nb=nb)
    n_mem = mem.shape[1]
    mem_t, memb = _mem_project(mem, lw["wmkv"])
    pz3 = pz.reshape(b, t, 1280)
    y = _post_prompt(o_a, o_b, pz3, pb3, memb, x, lw["wout"], lw["pw_bd"], lw["ps"], fg,
                     tm=tile, final=final)
    w_keep = min(NSA_WINDOW, t)
    state = (jnp.transpose(lft[:, 0:H_FOX, :], (0, 2, 1)),
             jnp.transpose(nwt[:, :, t - w_keep:].reshape(b, 2, G_NSA, HEAD_DIM, w_keep), POS_FIRST),
             pz3[:, t - POOL_BUF:, 0:256],
             jnp.transpose(mem_t.reshape(b, 2, H_MEM, HEAD_DIM, n_mem), POS_FIRST))
    return y, state, (fkvt, nkvt)


POS_FIRST = (0, 4, 1, 2, 3)


def _native_views(cache_fox_kv, cache_fox_lf, cache_nsa_kv, state_nsa_win, state_pool, cache_mem_kv):
    dp, n_phys = cache_fox_kv.shape[:2]
    db = state_nsa_win.shape[1]
    to_t = (0, 1, 3, 4, 5, 2)
    fox_t = jnp.transpose(cache_fox_kv, to_t).reshape(dp, n_phys, 512, PAGE_SIZE)
    lf_t = jnp.pad(jnp.transpose(cache_fox_lf.astype(F32), (0, 1, 3, 2)),
                   ((0, 0), (0, 0), (0, 8 - H_FOX), (0, 0)))
    nsa_t = jnp.transpose(cache_nsa_kv, to_t).reshape(dp, n_phys, 512, PAGE_SIZE)
    win_t = jnp.transpose(state_nsa_win, to_t).reshape(dp, db, 256, state_nsa_win.shape[2])
    mem_t = jnp.transpose(cache_mem_kv, to_t).reshape(dp, db, 512, cache_mem_kv.shape[2])
    pool_t = jnp.transpose(state_pool, (0, 2, 1, 3))
    return dict(fox_t=fox_t, lf_t=lf_t, nsa_t=nsa_t, win_t=win_t, mem_t=mem_t, pool_t=pool_t)


def _layer_sample(x, nv, page_table, lw, fg, layer, depth, stacked):
    db, ds, d = x.shape
    npg = page_table.shape[1]
    past = npg * PAGE_SIZE
    nblk = past // NSA_BLOCK
    final = layer == depth - 1
    pb, pz, aux, sf, sft, auxt = _project(x.reshape(db, d), lw["g"], lw["wp"], lw["bvec"],
                                          tm=db, tiles_per_seq=1, prompt=False)
    page_flat = page_table.reshape(db * npg).astype(I32)
    cmp = _compress_sample(page_flat, nv["nsa_t"], lw["pe_t"], lw["w1q"], lw["w2a"], lw["w2b"],
                           pages=min(db * npg, 128), layer=layer)
    cmp = jnp.transpose(cmp.reshape(2, db, npg, G_NSA, PAGE_SIZE // NSA_BLOCK, HEAD_DIM),
                        (0, 1, 2, 4, 3, 5)).reshape(2, db, nblk, LANES)
    kcmp, vcmp = cmp[0], cmp[1]
    expand = (np.arange(LANES)[:, None] == (np.arange(past)[None, :] // NSA_BLOCK)).astype(np.float32)
    oatt, win_new = _attend_sample(
        page_flat, nv["fox_t"], nv["lf_t"], nv["nsa_t"], pb.reshape(db, 1, N_PB),
        sf.reshape(db, 1, 1280), aux.reshape(db, 1, LANES), nv["win_t"], nv["mem_t"], kcmp, vcmp,
        jnp.asarray(expand, BF16), npg=npg, layer=layer, stacked=stacked)
    y, pool_new = _post_sample(oatt.reshape(db, 768), pz, nv["pool_t"], x.reshape(db, d), lw["wout"],
                               lw["pw_bd"], lw["ps"], fg, past=past, final=final, layer=layer)
    seq_first = (3, 0, 1, 2)
    state = (jnp.transpose(sft[0:512].reshape(2, H_FOX, HEAD_DIM, db), seq_first)[:, None],
             jnp.transpose(auxt[0:H_FOX])[:, None, :],
             jnp.transpose(sft[512:1024].reshape(4, G_NSA, HEAD_DIM, db), seq_first)[:, None],
             jnp.transpose(pool_new, (1, 0, 2)))
    return y.reshape(db, ds, d), state, (win_new,)


def _from_t(x_t, a, b):
    dp, bt, _, pos = x_t.shape
    return jnp.transpose(x_t.reshape(dp, bt, a, b, HEAD_DIM, pos), (0, 1, 5, 2, 3, 4))


def kernel(x_prompt, x_sample, cache_fox_kv, cache_fox_lf, cache_nsa_kv, state_nsa_win, state_pool,
           cache_mem_kv, page_table, mem_prompt, norm_g, w_in, b_fgt, nsa_pe, nsa_w1, nsa_w2, pool_w,
           pool_scale, w_mem_kv, w_out, final_g):
    depth = norm_g.shape[0]
    t = x_prompt.shape[1]
    assert x_sample.shape[1] == 1 and t % NSA_WINDOW == 0 and t // NSA_BLOCK <= NBLK_PAD
    hp, hs = x_prompt, x_sample
    fg = final_g.reshape(1, -1).astype(F32)
    nv = _native_views(cache_fox_kv, cache_fox_lf, cache_nsa_kv, state_nsa_win, state_pool,
                       cache_mem_kv)
    sp = [[] for _ in range(4)]
    ss = [[] for _ in range(4)]
    stk_p, stk_s = (), ()
    for l in range(depth):
        lw = _prep_layer_weights(norm_g[l], w_in[l], b_fgt[l], nsa_pe[l], nsa_w1[l], nsa_w2[l],
                                 pool_w[l], pool_scale[l], w_mem_kv[l], w_out[l])
        hp, st_p, stk_p = _layer_prompt(hp, mem_prompt, lw, fg, NSA_WINDOW, l, depth, stk_p)
        hs, st_s, stk_s = _layer_sample(hs, nv, page_table, lw, fg, l, depth, stk_s)
        for acc, s in zip(sp, st_p):
            acc.append(s)
        for acc, s in zip(ss, st_s):
            acc.append(s)
    fox_lf_p, nsa_win_p, pool_p, mem_kv_p = (jnp.stack(a) for a in sp)
    fox_kv_s, fox_lf_s, nsa_kv_s, pool_s = (jnp.stack(a) for a in ss)
    return (hp, hs, _from_t(stk_p[0], 2, H_FOX), fox_lf_p, _from_t(stk_p[1], 4, G_NSA), nsa_win_p, pool_p,
            mem_kv_p, fox_kv_s, fox_lf_s, nsa_kv_s, _from_t(stk_s[0], 2, G_NSA), pool_s)
```

```python
import functools

import numpy as np
import jax
import jax.numpy as jnp
from jax import lax
from jax.experimental import pallas as pl
from jax.experimental.pallas import tpu as pltpu

F32 = jnp.float32
BF16 = jnp.bfloat16
I32 = jnp.int32

HEAD_DIM = 64
H_FOX = 4
H_NSA = 4
G_NSA = 2
H_MEM = 4
D_GROUP = 256
POOL_WINDOWS = (2, 4, 8, 16)
POOL_BUF = 15
NSA_BLOCK = 64
NSA_TOPN = 16
NSA_WINDOW = 512
PAGE_SIZE = 128
RMS_EPS = 1e-6
NEG_INF = -1e30
FORCE_SCORE = 1e4
SCALE = HEAD_DIM ** -0.5
ALIBI_SLOPES = tuple(2.0 ** (-8.0 * (h + 1) / H_NSA) for h in range(H_NSA))

LANES = 128
NBLK_PAD = 128
VMEM_LIMIT = 56 * 1024 * 1024

C_FQ, C_FK, C_FV, C_NQ, C_NKV, C_MQ, C_PU, C_Z, C_SM, C_END = (
    0, 256, 512, 768, 1024, 1792, 2048, 2304, 3328, 3456)
N_PB = C_PU
LANE_GATE0 = 4


def _nt(a, b):
    return lax.dot_general(a, b, (((1,), (1,)), ((), ())), preferred_element_type=F32)


def _mm(a, b):
    return jnp.dot(a, b, preferred_element_type=F32)


def _split3(x):
    p1 = x.astype(BF16)
    r1 = x - p1.astype(F32)
    p2 = r1.astype(BF16)
    p3 = (r1 - p2.astype(F32)).astype(BF16)
    return p1, p2, p3


def _exact_mm(m01, x):
    p1, p2, p3 = _split3(x)
    return _mm(m01, p1) + _mm(m01, p2) + _mm(m01, p3)


def _sigmoid(x):
    return 1.0 / (1.0 + jnp.exp(-x))


def _log_sigmoid(x):
    return jnp.minimum(x, 0.0) - jnp.log1p(jnp.exp(-jnp.abs(x)))


def _iota(shape, dim):
    return lax.broadcasted_iota(I32, shape, dim)


def _div_pow2(x, n):
    assert n & (n - 1) == 0
    return jnp.right_shift(x, n.bit_length() - 1)


def _mod_pow2(x, n):
    assert n & (n - 1) == 0
    return jnp.bitwise_and(x, n - 1)


def _store_layer(ref, value, first_layer):
    if first_layer:
        ref[0] = value
        for k in range(1, ref.shape[0]):
            ref[k] = jnp.zeros(value.shape, value.dtype)
    else:
        ref[...] = value


def _proj_body(x_ref, g_ref, w_ref, b_ref, *rest, tm, tiles_per_seq, prompt, n_alias):
    pb_ref, pz_ref, aux_ref, *rest = rest[n_alias:]
    x = x_ref[...]
    ms = jnp.mean(x * x, axis=-1, keepdims=True)
    xn = (x * lax.rsqrt(ms + RMS_EPS) * g_ref[...]).astype(BF16)

    def seg(a, b):
        return _mm(xn, w_ref[:, a:b])

    pb_ref[:, C_FQ:C_FK] = (seg(C_FQ, C_FK) * SCALE).astype(BF16)
    fkv = seg(C_FK, C_NQ)
    pb_ref[:, C_FK:C_NQ] = fkv.astype(BF16)
    pb_ref[:, C_NQ:C_NKV] = (seg(C_NQ, C_NKV) * SCALE).astype(BF16)
    nkv = seg(C_NKV, C_MQ)
    pb_ref[:, C_NKV:C_MQ] = nkv.astype(BF16)
    pb_ref[:, C_MQ:C_PU] = (seg(C_MQ, C_PU) * SCALE).astype(BF16)
    pz_ref[...] = seg(C_PU, C_SM)
    small = seg(C_SM, C_END)

    lane = _iota((tm, LANES), 1)
    lf = _log_sigmoid(small + b_ref[...])
    aux = jnp.where(lane < H_FOX, lf, jnp.where(lane < LANE_GATE0 + 3 * H_NSA, _sigmoid(small), 0.0))
    aux_ref[...] = aux
    if prompt:
        kvc_ref, ct_ref, lft_ref, fkvt_ref, nkvt_ref, nwt_ref, carry_ref = rest
        kvc_ref[0] = nkv[:, 0:LANES]
        kvc_ref[1] = nkv[:, LANES:256]
        _store_layer(fkvt_ref, fkv.T, n_alias == 0)
        _store_layer(nkvt_ref, nkv[:, 0:512].T, n_alias == 0)
        nwt_ref[...] = nkv[:, 512:768].T
        lfm = jnp.where(lane < H_FOX, lf, 0.0)
        lft_ref[...] = lfm.T[0:8, :]
        tri = (_iota((LANES, LANES), 0) >= _iota((LANES, LANES), 1)).astype(BF16)
        first = (pl.program_id(0) % tiles_per_seq) == 0
        carry = jnp.where(first, 0.0, carry_ref[0:1, :])
        for r in range(tm // LANES):
            blk = lfm[r * LANES:(r + 1) * LANES]
            cblk = _exact_mm(tri, blk) + carry
            carry = cblk[LANES - 1:LANES, :]
            ct_ref[:, r * LANES:(r + 1) * LANES] = cblk.T[0:8, :]
        carry_ref[0:1, :] = carry
    else:
        sf_ref, sft_ref, auxt_ref = rest
        sf_ref[:, 0:512] = fkv
        sf_ref[:, 512:1280] = nkv
        sft_ref[0:512, :] = fkv.T
        sft_ref[512:1280, :] = nkv.T
        auxt_ref[...] = aux.T


def _stacked_spec(depth, layer, blk, index):
    if layer == 0:
        return pl.BlockSpec((depth,) + blk, lambda *a: (0,) + index(*a))
    return pl.BlockSpec((None,) + blk, lambda *a: (layer,) + index(*a))


def _project(x2d, g, wp, bvec, *, tm, tiles_per_seq, prompt, layer=0, depth=1, stacked=()):
    n, d = x2d.shape
    grid = (n // tm,)
    row = lambda i: (i, 0)
    const = lambda i: (0, 0)
    out_shape = [jax.ShapeDtypeStruct((n, N_PB), BF16), jax.ShapeDtypeStruct((n, 1280), F32),
                 jax.ShapeDtypeStruct((n, LANES), F32)]
    out_specs = [pl.BlockSpec((tm, N_PB), row), pl.BlockSpec((tm, 1280), row),
                 pl.BlockSpec((tm, LANES), row)]
    scratch = []
    aliases = {}
    if prompt:
        assert len(stacked) == (2 if layer else 0)
        nseq = n // (tm * tiles_per_seq)
        t = tm * tiles_per_seq
        seq_idx = lambda i: (i // tiles_per_seq, 0, i % tiles_per_seq)
        seq_t = lambda rows: pl.BlockSpec((None, rows, tm), seq_idx)
        lay_t = lambda rows: _stacked_spec(depth, layer, (None, rows, tm), seq_idx)
        out_shape += [jax.ShapeDtypeStruct((2, n, LANES), F32),
                      jax.ShapeDtypeStruct((nseq, 8, t), F32), jax.ShapeDtypeStruct((nseq, 8, t), F32),
                      jax.ShapeDtypeStruct((depth, nseq, 512, t), F32),
                      jax.ShapeDtypeStruct((depth, nseq, 512, t), F32),
                      jax.ShapeDtypeStruct((nseq, 256, t), F32)]
        out_specs += [pl.BlockSpec((2, tm, LANES), lambda i: (0, i, 0)),
                      seq_t(8), seq_t(8), lay_t(512), lay_t(512), seq_t(256)]
        scratch = [pltpu.VMEM((8, LANES), F32)]
        aliases = {4 + k: 6 + k for k in range(len(stacked))}
    else:
        assert grid == (1,) and not stacked
        out_shape += [jax.ShapeDtypeStruct((n, 1280), F32), jax.ShapeDtypeStruct((1280, n), F32),
                      jax.ShapeDtypeStruct((LANES, n), F32)]
        out_specs += [pl.BlockSpec((tm, 1280), row), pl.BlockSpec((1280, tm), const),
                      pl.BlockSpec((LANES, tm), const)]
    return pl.pallas_call(
        functools.partial(_proj_body, tm=tm, tiles_per_seq=tiles_per_seq, prompt=prompt,
                          n_alias=len(stacked)),
        grid=grid,
        in_specs=[pl.BlockSpec((tm, d), row), pl.BlockSpec((1, d), const),
                  pl.BlockSpec((d, C_END), const), pl.BlockSpec((1, LANES), const)]
        + [pl.BlockSpec(memory_space=pl.ANY)] * len(stacked),
        out_specs=out_specs, out_shape=out_shape, scratch_shapes=scratch,
        input_output_aliases=aliases,
        compiler_params=pltpu.CompilerParams(dimension_semantics=("arbitrary",),
                                             vmem_limit_bytes=VMEM_LIMIT),
        name="proj",
    )(x2d, g, wp, bvec, *stacked)


def _memkv_body(m_ref, w_ref, ot_ref, ob_ref):
    r = _mm(m_ref[...].astype(BF16), w_ref[...])
    ot_ref[...] = r.T
    ob_ref[...] = r.astype(BF16)


def _mem_project(mem3, w):
    b, n_mem, d = mem3.shape
    e = w.shape[1]
    return pl.pallas_call(
        _memkv_body, grid=(b,),
        in_specs=[pl.BlockSpec((None, n_mem, d), lambda i: (i, 0, 0)),
                  pl.BlockSpec((d, e), lambda i: (0, 0))],
        out_specs=[pl.BlockSpec((None, e, n_mem), lambda i: (i, 0, 0)),
                   pl.BlockSpec((None, n_mem, e), lambda i: (i, 0, 0))],
        out_shape=[jax.ShapeDtypeStruct((b, e, n_mem), F32), jax.ShapeDtypeStruct((b, n_mem, e), BF16)],
        name="memkv",
    )(mem3, w)


def _fox_body(qt_ref, kt_ref, q_ref, k_ref, v_ref, ct_ref, o_ref, m_ref, l_ref, acc_ref, *, tq, tk):
    qi = qt_ref[pl.program_id(1)]
    ki = kt_ref[pl.program_id(1)]

    @pl.when(ki == 0)
    def _():
        m_ref[...] = jnp.full(m_ref.shape, NEG_INF, F32)
        l_ref[...] = jnp.zeros(l_ref.shape, F32)
        acc_ref[...] = jnp.zeros(acc_ref.shape, F32)

    lane = _iota((tq, LANES), 1)

    def step(masked):
        if masked:
            causal = _iota((tq, tk), 0) >= _iota((tq, tk), 1)
        scores = []
        for h in range(H_FOX):
            pr, hh = divmod(h, 2)
            cs = slice(pr * LANES, (pr + 1) * LANES)
            q2 = q_ref[:, cs]
            qm = jnp.where((lane >= 64 * hh) & (lane < 64 * hh + 64), q2, jnp.zeros_like(q2))
            scores.append(_nt(qm, k_ref[:, cs]))
        probs = []
        for h in range(H_FOX):
            s = scores[h] - ct_ref[h:h + 1, :]
            if masked:
                s = jnp.where(causal, s, NEG_INF)
            m_prev = m_ref[h]
            m_new = jnp.maximum(m_prev, jnp.max(s, axis=1, keepdims=True))
            alpha = jnp.exp(m_prev - m_new)
            p = jnp.exp(s - jnp.concatenate([m_new] * (tk // LANES), axis=1))
            l_ref[h] = alpha * l_ref[h] + jnp.sum(p, axis=1, keepdims=True)
            m_ref[h] = m_new
            probs.append((p.astype(BF16), alpha))
        for h in range(H_FOX):
            cs = slice((h // 2) * LANES, (h // 2 + 1) * LANES)
            p, alpha = probs[h]
            acc_ref[h] = alpha * acc_ref[h] + _mm(p, v_ref[:, cs])

    @pl.when(ki < qi)
    def _():
        step(False)

    @pl.when(ki == qi)
    def _():
        step(True)
        for pr in range(H_FOX // 2):
            o0 = acc_ref[2 * pr] / l_ref[2 * pr]
            o1 = acc_ref[2 * pr + 1] / l_ref[2 * pr + 1]
            o_ref[:, pr * LANES:(pr + 1) * LANES] = jnp.where(lane < 64, o0, o1)


def _causal_pairs(nq):
    pairs = [(qi, ki) for qi in range(nq) for ki in range(qi + 1)]
    return (jnp.asarray([p[0] for p in pairs], I32), jnp.asarray([p[1] for p in pairs], I32))


def _fox_prompt(pb3, ct, *, tq):
    b, t, _ = pb3.shape
    qt, kt = _causal_pairs(t // tq)
    grid_spec = pltpu.PrefetchScalarGridSpec(
        num_scalar_prefetch=2, grid=(b, qt.shape[0]),
        in_specs=[pl.BlockSpec((None, tq, 256), lambda bi, s, qt, kt: (bi, qt[s], C_FQ // 256)),
                  pl.BlockSpec((None, tq, 256), lambda bi, s, qt, kt: (bi, kt[s], C_FK // 256)),
                  pl.BlockSpec((None, tq, 256), lambda bi, s, qt, kt: (bi, kt[s], C_FV // 256)),
                  pl.BlockSpec((None, 8, tq), lambda bi, s, qt, kt: (bi, 0, kt[s]))],
        out_specs=pl.BlockSpec((None, tq, 256), lambda bi, s, qt, kt: (bi, qt[s], 0)),
        scratch_shapes=[pltpu.VMEM((H_FOX, tq, LANES), F32), pltpu.VMEM((H_FOX, tq, LANES), F32),
                        pltpu.VMEM((H_FOX, tq, LANES), F32)])
    return pl.pallas_call(
        functools.partial(_fox_body, tq=tq, tk=tq),
        grid_spec=grid_spec,
        out_shape=jax.ShapeDtypeStruct((b, t, 256), F32),
        compiler_params=pltpu.CompilerParams(
            dimension_semantics=("parallel", "arbitrary"), vmem_limit_bytes=VMEM_LIMIT),
        name="fox_prompt",
    )(qt, kt, pb3, pb3, pb3, ct)


def _cmp_body(x_ref, pe_ref, w1_ref, w2_ref, o_ref, acc_ref, *, m):
    lo = _iota((m, LANES), 1) < HEAD_DIM
    acc_ref[...] = jnp.zeros(acc_ref.shape, F32)
    for pos in range(NSA_BLOCK):
        x = x_ref[pl.ds(pos, m, stride=NSA_BLOCK), :] + pe_ref[pos:pos + 1, :]
        a = jnp.concatenate([jnp.where(lo, x, 0.0), jnp.where(lo, 0.0, x)], axis=0)
        acc_ref[...] += _mm(a.astype(BF16), w1_ref[pos])
    h = acc_ref[...]
    o_ref[...] = _mm((h * _sigmoid(h)).astype(BF16), w2_ref[...])


def _compress_prompt(kvc, pe2, w1dup, w2):
    _, n, _ = kvc.shape
    m = n // NSA_BLOCK
    by_j = lambda blk: pl.BlockSpec((None,) + blk, lambda j: (j,) + (0,) * len(blk))
    return pl.pallas_call(
        functools.partial(_cmp_body, m=m), grid=(2,),
        in_specs=[by_j((n, LANES)), by_j((NSA_BLOCK, LANES)), by_j((NSA_BLOCK, LANES, 256)),
                  by_j((256, HEAD_DIM))],
        out_specs=by_j((G_NSA * m, HEAD_DIM)),
        out_shape=jax.ShapeDtypeStruct((2, G_NSA * m, HEAD_DIM), F32),
        scratch_shapes=[pltpu.VMEM((G_NSA * m, 256), F32)],
        compiler_params=pltpu.CompilerParams(vmem_limit_bytes=VMEM_LIMIT),
        name="cmp_prompt",
    )(kvc, pe2, w1dup, w2)


def _masked_softmax_cols(s, valid):
    s = jnp.where(valid, s, NEG_INF)
    e = jnp.where(valid, jnp.exp(s - jnp.max(s, axis=0, keepdims=True)), 0.0)
    return e / jnp.maximum(jnp.sum(e, axis=0, keepdims=True), 1e-30)


def _select_blocks(score, nb, topn):
    nbp, n = score.shape
    blk = _iota((nbp, n), 0)
    rank = jnp.zeros((nbp, n), F32)
    for m in range(nb):
        r = score[m:m + 1, :]
        rank = rank + jnp.where((r > score) | ((r == score) & (blk > m)), 1.0, 0.0)
    return jnp.where((rank < topn) & (score >= 0.0), 1.0, 0.0)


def _nsa_body(qt_ref, kt_ref, q_ref, ks_ref, vs_ref, kwc_ref, kwp_ref, vwc_ref, vwp_ref, kc_ref, vc_ref,
              aux_ref, o_ref, sel_ref, m_ref, l_ref, acc_ref, oc_ref, ow_ref, *, tq, nb, topn):
    qi = qt_ref[pl.program_id(1)]
    ki = kt_ref[pl.program_id(1)]
    q0 = qi * tq
    tk = tq
    lane = _iota((tq, LANES), 1)

    def qstack(g):
        q2 = q_ref[:, g * LANES:(g + 1) * LANES].astype(F32)
        q2r = pltpu.roll(q2, 64, 1)
        ing = (lane >= 64 * g) & (lane < 64 * g + 64)
        qa, qb = (q2, q2r) if g == 0 else (q2r, q2)
        return jnp.concatenate([jnp.where(ing, qa, 0.0), jnp.where(ing, qb, 0.0)],
                               axis=0).astype(BF16)

    def biased(raw, g, kd, valid):
        sa = jnp.where(valid, raw[:tq] + ALIBI_SLOPES[2 * g] * kd, NEG_INF)
        sb = jnp.where(valid, raw[tq:] + ALIBI_SLOPES[2 * g + 1] * kd, NEG_INF)
        return jnp.concatenate([sa, sb], axis=0)

    def lanes(x, n):
        return jnp.concatenate([x] * (n // LANES), axis=1)

    @pl.when(ki == 0)
    def _():
        m_ref[...] = jnp.full(m_ref.shape, NEG_INF, F32)
        l_ref[...] = jnp.zeros(l_ref.shape, F32)
        acc_ref[...] = jnp.zeros(acc_ref.shape, F32)
        dqk = _iota((tq, tk), 0) - _iota((tq, tk), 1)
        kd_cur = _iota((1, tk), 1).astype(F32)
        for g in range(G_NSA):
            qs = qstack(g)
            st = _nt(kc_ref[...], qs)
            blk = _iota((NBLK_PAD, 2 * tq), 0)
            col = _iota((NBLK_PAD, 2 * tq), 1)
            qpos = q0 + jnp.where(col >= tq, col - tq, col)
            slope = jnp.where(col < tq, ALIBI_SLOPES[2 * g], ALIBI_SLOPES[2 * g + 1])
            cend = blk * NSA_BLOCK + (NSA_BLOCK - 1)
            st = st + slope * (cend - q0).astype(F32)
            pt = _masked_softmax_cols(st, (cend <= qpos) & (blk < nb))
            oc_ref[g] = _mm(pt.T.astype(BF16), vc_ref[...])
            imp = pt[:, :tq] + pt[:, tq:]
            nbp = -(-nb // 8) * 8
            blk2 = _iota((nbp, tq), 0)
            cur = _div_pow2(q0 + _iota((nbp, tq), 1), NSA_BLOCK)
            forced = (blk2 == 0) | (blk2 == cur) | (blk2 == cur - 1)
            score = jnp.where(blk2 > cur, -1.0, jnp.where(forced, FORCE_SCORE, imp[:nbp]))
            selt = _select_blocks(score, nb, topn)
            if nbp < NBLK_PAD:
                selt = jnp.concatenate([selt, jnp.zeros((NBLK_PAD - nbp, tq), F32)], axis=0)
            sel_ref[g] = selt.T.astype(BF16)
            s1 = biased(_nt(qs, kwc_ref[...]), g, kd_cur, dqk >= 0)
            s2 = biased(_nt(qs, kwp_ref[...]), g, kd_cur - tq, (dqk + tq <= NSA_WINDOW) & (qi > 0))
            mw = jnp.maximum(jnp.max(s1, axis=1, keepdims=True), jnp.max(s2, axis=1, keepdims=True))
            mw = jnp.broadcast_to(mw, (2 * tq, LANES))
            e1 = jnp.exp(s1 - lanes(mw, tk))
            e2 = jnp.exp(s2 - lanes(mw, tk))
            lw = jnp.sum(e1, axis=1, keepdims=True) + jnp.sum(e2, axis=1, keepdims=True)
            ow_ref[g] = (_mm(e1.astype(BF16), vwc_ref[...]) + _mm(e2.astype(BF16), vwp_ref[...])) \
                / jnp.maximum(lw, 1e-30)

    def sel_step(diag):
        expand = (_iota((NBLK_PAD, tk), 0) ==
                  ki * (tk // NSA_BLOCK) + _div_pow2(_iota((NBLK_PAD, tk), 1), NSA_BLOCK)).astype(BF16)
        kd = (_iota((1, tk), 1) + (ki * tk - q0)).astype(F32)
        raws = [(_nt(qstack(g), ks_ref[...]), _mm(sel_ref[g], expand)) for g in range(G_NSA)]
        probs = []
        for g in range(G_NSA):
            raw, picked = raws[g]
            valid = picked > 0.5
            if diag:
                valid = valid & (_iota((tq, tk), 0) >= _iota((tq, tk), 1))
            s = biased(raw, g, kd, valid)
            m_prev = m_ref[g]
            m_new = jnp.maximum(m_prev, jnp.max(s, axis=1, keepdims=True))
            alpha = jnp.exp(m_prev - m_new)
            p = jnp.exp(s - lanes(m_new, tk))
            l_ref[g] = alpha * l_ref[g] + jnp.sum(p, axis=1, keepdims=True)
            m_ref[g] = m_new
            probs.append((p.astype(BF16), alpha))
        for g in range(G_NSA):
            p, alpha = probs[g]
            acc_ref[g] = alpha * acc_ref[g] + _mm(p, vs_ref[...])

    @pl.when(ki < qi)
    def _():
        sel_step(False)

    @pl.when(ki == qi)
    def _():
        sel_step(True)
        for g in range(G_NSA):
            o_s = acc_ref[g] / jnp.maximum(l_ref[g], 1e-30)
            o_c = oc_ref[g]
            o_w = ow_ref[g]
            outs = []
            for hh in range(2):
                h = 2 * g + hh
                rs = slice(hh * tq, (hh + 1) * tq)
                gl = LANE_GATE0 + 3 * h
                outs.append(aux_ref[:, gl:gl + 1] * o_c[rs] + aux_ref[:, gl + 1:gl + 2] * o_s[rs]
                            + aux_ref[:, gl + 2:gl + 3] * o_w[rs])
            oa, ob = outs
            if g == 0:
                ob = pltpu.roll(ob, 64, 1)
            else:
                oa = pltpu.roll(oa, 64, 1)
            o_ref[:, g * LANES:(g + 1) * LANES] = jnp.where(lane < 64, oa, ob)


def _nsa_prompt(pb3, kcmp, vcmp, aux3, *, tq, nb):
    b, t, _ = pb3.shape
    qt, kt = _causal_pairs(t // tq)
    ct = C_NKV // LANES
    cur = lambda c: (lambda bi, s, qt, kt: (bi, qt[s], c))
    prev = lambda c: (lambda bi, s, qt, kt: (bi, jnp.maximum(qt[s] - 1, 0), c))
    kvt = lambda c: (lambda bi, s, qt, kt: (bi, kt[s], c))
    cmp_spec = pl.BlockSpec((None, NBLK_PAD, LANES), lambda bi, s, qt, kt: (bi, 0, 0))
    tile = lambda f: pl.BlockSpec((None, tq, LANES), f)
    grid_spec = pltpu.PrefetchScalarGridSpec(
        num_scalar_prefetch=2, grid=(b, qt.shape[0]),
        in_specs=[pl.BlockSpec((None, tq, 256), lambda bi, s, qt, kt: (bi, qt[s], C_NQ // 256)),
                  tile(kvt(ct + 2)), tile(kvt(ct + 3)),
                  tile(cur(ct + 4)), tile(prev(ct + 4)), tile(cur(ct + 5)), tile(prev(ct + 5)),
                  cmp_spec, cmp_spec,
                  pl.BlockSpec((None, tq, LANES), lambda bi, s, qt, kt: (bi, qt[s], 0))],
        out_specs=pl.BlockSpec((None, tq, 256), lambda bi, s, qt, kt: (bi, qt[s], 0)),
        scratch_shapes=[pltpu.VMEM((G_NSA, tq, NBLK_PAD), BF16),
                        pltpu.VMEM((G_NSA, 2 * tq, LANES), F32), pltpu.VMEM((G_NSA, 2 * tq, LANES), F32),
                        pltpu.VMEM((G_NSA, 2 * tq, LANES), F32),
                        pltpu.VMEM((G_NSA, 2 * tq, LANES), F32),
                        pltpu.VMEM((G_NSA, 2 * tq, LANES), F32)])
    return pl.pallas_call(
        functools.partial(_nsa_body, tq=tq, nb=nb, topn=min(NSA_TOPN, nb)),
        grid_spec=grid_spec,
        out_shape=jax.ShapeDtypeStruct((b, t, 256), F32),
        compiler_params=pltpu.CompilerParams(
            dimension_semantics=("parallel", "arbitrary"), vmem_limit_bytes=VMEM_LIMIT),
        name="nsa_prompt",
    )(qt, kt, pb3, pb3, pb3, pb3, pb3, pb3, pb3, kcmp, vcmp, aux3)


def _mem_attend(mq_ref, mk_ref, mv_ref, rows):
    lane = _iota((rows, LANES), 1)
    n_mem = mk_ref.shape[0]
    scores = []
    for h in range(H_MEM):
        pr, hh = divmod(h, 2)
        cs = slice(pr * LANES, (pr + 1) * LANES)
        q2 = mq_ref[:, cs]
        qm = jnp.where((lane >= 64 * hh) & (lane < 64 * hh + 64), q2, jnp.zeros_like(q2))
        scores.append(_nt(qm, mk_ref[:, cs]))
    weights = []
    for h in range(H_MEM):
        s = scores[h]
        mx = jnp.broadcast_to(jnp.max(s, axis=1, keepdims=True), (rows, LANES))
        e = jnp.exp(s - jnp.concatenate([mx] * (n_mem // LANES), axis=1))
        weights.append((e.astype(BF16), jnp.sum(e, axis=1, keepdims=True)))
    outs = []
    for h in range(H_MEM):
        e, den = weights[h]
        cs = slice((h // 2) * LANES, (h // 2 + 1) * LANES)
        outs.append(_mm(e, mv_ref[:, cs]) / den)
    return [jnp.where(lane < 64, outs[2 * pr], outs[2 * pr + 1]) for pr in range(H_MEM // 2)]


def _mix_out(x, parts, z_of, wout_ref, fg_ref, final):
    y = x
    for c in range(0, len(parts), 2):
        gated = []
        for cc in (c, c + 1):
            z = z_of(cc)
            gated.append((parts[cc] * (z * _sigmoid(z))).astype(BF16))
        y = y + _mm(jnp.concatenate(gated, axis=1), wout_ref[c * LANES:(c + 2) * LANES, :])
    if final:
        y = y * lax.rsqrt(jnp.mean(y * y, axis=-1, keepdims=True) + RMS_EPS) * fg_ref[...]
    return y


def _pool_window_lane(shape):
    grp = _div_pow2(_iota(shape, 1), D_GROUP // len(POOL_WINDOWS))
    wl = jnp.where(grp == 0, POOL_WINDOWS[0], jnp.where(grp == 1, POOL_WINDOWS[1],
                   jnp.where(grp == 2, POOL_WINDOWS[2], POOL_WINDOWS[3])))
    return grp, wl


HALO = 32


def _post_body(oa_ref, ob_ref, pz_ref, halo_ref, mq_ref, mk_ref, mv_ref, x_ref, wout_ref, pw_ref,
               ps_ref, fg_ref, y_ref, s0, s1, s2, s3, *, tm, final):
    i = pl.program_id(1)
    u = pz_ref[:, 0:256]
    s0[0:HALO, :] = jnp.where(i > 0, halo_ref[...], 0.0)
    s0[HALO:HALO + tm, :] = u
    n = tm + HALO
    s1[8:n, :] = s0[8:n, :] + s0[7:n - 1, :]
    s2[16:n, :] = s1[16:n, :] + s1[14:n - 2, :]
    s3[24:n, :] = s2[24:n, :] + s2[20:n - 4, :]
    a16 = s3[HALO:n, :] + s3[HALO - 8:n - 8, :]
    grp, wl = _pool_window_lane((tm, 256))
    tsum = jnp.where(grp == 0, s1[HALO:n, :], jnp.where(grp == 1, s2[HALO:n, :],
                     jnp.where(grp == 2, s3[HALO:n, :], a16)))
    pos = i * tm + _iota((tm, 256), 0)
    cnt = jnp.minimum(pos + 1, wl).astype(F32)
    o_pool = _mm((tsum / cnt - u).astype(BF16), pw_ref[...]) * ps_ref[...]
    o_mem = _mem_attend(mq_ref, mk_ref, mv_ref, tm)
    parts = [oa_ref[:, 0:LANES], oa_ref[:, LANES:256], ob_ref[:, 0:LANES], ob_ref[:, LANES:256],
             o_pool[:, 0:LANES], o_pool[:, LANES:256], o_mem[0], o_mem[1]]
    z_of = lambda c: pz_ref[:, 256 + c * LANES:256 + (c + 1) * LANES]
    y_ref[...] = _mix_out(x_ref[...], parts, z_of, wout_ref, fg_ref, final)


def _post_prompt(oa, ob, pz3, pb3, memb, x3, wout, pw_bd, ps, fg, *, tm, final):
    b, t, d = x3.shape
    nt = t // tm
    row = lambda bi, i: (bi, i, 0)
    const = lambda bi, i: (0, 0)
    return pl.pallas_call(
        functools.partial(_post_body, tm=tm, final=final),
        grid=(b, nt),
        in_specs=[pl.BlockSpec((None, tm, 256), row), pl.BlockSpec((None, tm, 256), row),
                  pl.BlockSpec((None, tm, 1280), row),
                  pl.BlockSpec((None, HALO, 256),
                               lambda bi, i: (bi, jnp.maximum(i * (tm // HALO) - 1, 0), 0)),
                  pl.BlockSpec((None, tm, 256), lambda bi, i: (bi, i, C_MQ // 256)),
                  pl.BlockSpec((None, 256, 256), lambda bi, i: (bi, 0, 0)),
                  pl.BlockSpec((None, 256, 256), lambda bi, i: (bi, 0, 1)),
                  pl.BlockSpec((None, tm, d), row),
                  pl.BlockSpec((d, d), const), pl.BlockSpec((256, 256), const),
                  pl.BlockSpec((1, 256), const), pl.BlockSpec((1, d), const)],
        out_specs=pl.BlockSpec((None, tm, d), row),
        out_shape=jax.ShapeDtypeStruct((b, t, d), F32),
        scratch_shapes=[pltpu.VMEM((tm + HALO, 256), F32)] * 4,
        compiler_params=pltpu.CompilerParams(dimension_semantics=("parallel", "arbitrary"),
                                             vmem_limit_bytes=VMEM_LIMIT),
        name="post_prompt",
    )(oa, ob, pz3, pz3, pb3, memb, memb, x3, wout, pw_bd, ps, fg)


CMP_PITCH = 72
DQ = 4


def _cmp_sample_body(pt_ref, nsa_hbm, pe_ref, w1_ref, w2a_ref, w2b_ref, o_ref, xbuf, acc0, acc1, sem,
                     *, pages, layer):
    j = pl.program_id(0)
    i = pl.program_id(1)
    ni = pl.num_programs(1)
    s = j * ni + i
    m = pages * G_NSA

    def chunk_copy(jj, ii, slot, p, g):
        page = pt_ref[ii * pages + p]
        return pltpu.make_async_copy(
            nsa_hbm.at[layer, page, pl.ds(jj * LANES + g * HEAD_DIM, HEAD_DIM), :],
            xbuf.at[slot, pl.ds((p * G_NSA + g) * CMP_PITCH, HEAD_DIM), :], sem.at[slot])

    def for_chunks(jj, ii, slot, fn):
        def body(p, carry):
            for g in range(G_NSA):
                fn(chunk_copy(jj, ii, slot, p, g))
            return carry
        lax.fori_loop(0, pages, body, 0)

    @pl.when(s == 0)
    def _():
        for_chunks(0, 0, 0, lambda c: c.start())

    @pl.when(s + 1 < 2 * ni)
    def _():
        wrap = i + 1 == ni
        for_chunks(jnp.where(wrap, j + 1, j), jnp.where(wrap, 0, i + 1), (s + 1) % 2,
                   lambda c: c.start())

    slot = s % 2
    for_chunks(j, i, slot, lambda c: c.wait())

    lo = _iota((m, LANES), 1) < HEAD_DIM
    acc0[...] = jnp.zeros(acc0.shape, F32)
    acc1[...] = jnp.zeros(acc1.shape, F32)
    for dq in range(HEAD_DIM // DQ):
        xs = [xbuf[slot, pl.ds(DQ * dq + k, m, stride=CMP_PITCH), :] + pe_ref[DQ * dq + k:DQ * dq + k + 1, :]
              for k in range(DQ)]
        xr = [pltpu.roll(x, HEAD_DIM, 1) for x in xs]
        a0 = jnp.concatenate([jnp.where(lo, xs[0], xr[1]), jnp.where(lo, xs[2], xr[3])], axis=1)
        a1 = jnp.concatenate([jnp.where(lo, xr[0], xs[1]), jnp.where(lo, xr[2], xs[3])], axis=1)
        w = w1_ref[dq]
        acc0[...] += _mm(a0.astype(BF16), w)
        acc1[...] += _mm(a1.astype(BF16), w)
    h0 = acc0[...]
    h1 = acc1[...]
    h0 = (h0 * _sigmoid(h0)).astype(BF16)
    h1 = (h1 * _sigmoid(h1)).astype(BF16)
    o_ref[...] = _mm(h0, w2a_ref[...]) + _mm(h1, w2b_ref[...])


def _compress_sample(page_flat, nsa_t, pe_t, w1q, w2a, w2b, *, pages, layer):
    npages = page_flat.shape[0]
    nsteps = npages // pages
    m = pages * G_NSA
    by_j = lambda blk: pl.BlockSpec((None,) + blk, lambda j, i, pt: (j,) + (0,) * len(blk))
    grid_spec = pltpu.PrefetchScalarGridSpec(
        num_scalar_prefetch=1, grid=(2, nsteps),
        in_specs=[pl.BlockSpec(memory_space=pl.ANY), by_j((HEAD_DIM, LANES)),
                  by_j((HEAD_DIM // DQ, 256, 256)), by_j((256, LANES)), by_j((256, LANES))],
        out_specs=pl.BlockSpec((None, m, LANES), lambda j, i, pt: (j, i, 0)),
        scratch_shapes=[pltpu.VMEM((2, m * CMP_PITCH, LANES), F32), pltpu.VMEM((m, 256), F32),
                        pltpu.VMEM((m, 256), F32), pltpu.SemaphoreType.DMA((2,))])
    return pl.pallas_call(
        functools.partial(_cmp_sample_body, pages=pages, layer=layer),
        grid_spec=grid_spec,
        out_shape=jax.ShapeDtypeStruct((2, npages * G_NSA, LANES), F32),
        compiler_params=pltpu.CompilerParams(dimension_semantics=("arbitrary", "arbitrary"),
                                             vmem_limit_bytes=VMEM_LIMIT),
        name="cmp_sample",
    )(page_flat, nsa_t, pe_t, w1q, w2a, w2b)


HROWS = 16


def _rows_from_lanes(vec_row, pick):
    return jnp.sum(jnp.where(pick, vec_row, 0.0), axis=1, keepdims=True)


def _bf16r(x):
    return x.astype(BF16).astype(F32)


def _att_sample_body(pt_ref, fox_hbm, lf_hbm, nsa_hbm, pb_ref, sf_ref, aux_ref, win_ref, mem_ref,
                     kc_ref, vc_ref, exp_ref, *rest, npg, past, layer):
    o_ref, wout_ref, fbuf, lbuf, sbuf, sem = rest[-6:]
    b = pl.program_id(0)
    nb = pl.num_programs(0)

    def copies(seq, slot):
        out = []
        for i in range(npg):
            page = pt_ref[seq * npg + i]
            out.append(pltpu.make_async_copy(
                fox_hbm.at[layer, page], fbuf.at[slot, i], sem.at[0, slot]))
            out.append(pltpu.make_async_copy(
                lf_hbm.at[layer, page], lbuf.at[slot, pl.ds(i * 8, 8), :], sem.at[1, slot]))
            out.append(pltpu.make_async_copy(
                nsa_hbm.at[layer, page, pl.ds(256, 256), :], sbuf.at[slot, i], sem.at[2, slot]))
        return out

    def slab(buf, r0, r1):
        return jnp.concatenate([buf[slot, i, r0:r1, :].astype(BF16) for i in range(npg)], axis=1)

    @pl.when(b == 0)
    def _():
        for c in copies(0, 0):
            c.start()

    @pl.when(b + 1 < nb)
    def _():
        for c in copies(b + 1, (b + 1) % 2):
            c.start()

    slot = b % 2
    for c in copies(b, slot):
        c.wait()

    row8 = _iota((HROWS, LANES), 0)
    lane8 = _iota((HROWS, LANES), 1)
    aux = aux_ref[...]
    pbr = pb_ref[...].astype(F32)
    sfr = sf_ref[...]

    def head_rows(q256):
        r = _iota((HROWS, 256), 0)
        l = _iota((HROWS, 256), 1)
        return jnp.where(_div_pow2(l, HEAD_DIM) == r, q256, 0.0).astype(BF16)

    def pick_heads(o8):
        r = _iota((HROWS, 256), 0)
        l = _iota((HROWS, 256), 1)
        return jnp.sum(jnp.where(_div_pow2(l, HEAD_DIM) == r, o8, 0.0), axis=0, keepdims=True)

    kt = slab(fbuf, 0, 256)
    vt = slab(fbuf, 256, 512)
    qf = head_rows(pbr[:, C_FQ:C_FK])
    s = _mm(qf, kt)
    lf2 = lbuf[slot]
    r2 = _iota((npg * 8, npg * 8), 0)
    c2 = _iota((npg * 8, npg * 8), 1)
    upper = (_iota((LANES, LANES), 0) > _iota((LANES, LANES), 1)).astype(BF16)
    within = _exact_mm_rhs(lf2, upper)
    tot = jnp.sum(lf2, axis=1, keepdims=True)
    later_pages = ((_mod_pow2(c2, 8) == _mod_pow2(r2, 8))
                   & (_div_pow2(c2, 8) > _div_pow2(r2, 8))).astype(BF16)
    later = _exact_mm(later_pages, jnp.broadcast_to(tot, (npg * 8, LANES)))
    rr = _iota((npg * 8, LANES), 0)
    ll = _iota((npg * 8, LANES), 1)
    lf_new = _rows_from_lanes(aux, ll == _mod_pow2(rr, 8))
    bias = within + later + lf_new
    zpad = jnp.zeros((HROWS - 8, LANES), F32)
    s = jnp.concatenate(
        [s[:, i * PAGE_SIZE:(i + 1) * PAGE_SIZE] + jnp.concatenate([bias[i * 8:(i + 1) * 8, :], zpad], axis=0)
         for i in range(npg)], axis=1)
    knew = _bf16r(sfr[:, 0:256])
    vnew = _bf16r(sfr[:, 256:512])
    s_new = jnp.sum(qf.astype(F32) * knew, axis=1, keepdims=True)
    m = jnp.maximum(jnp.max(s, axis=1, keepdims=True), s_new)
    e = jnp.exp(s - m)
    e_new = jnp.exp(s_new - m)
    den = jnp.sum(e, axis=1, keepdims=True) + e_new
    o8 = (_nt(e.astype(BF16), vt) + _bf16r(e_new) * vnew) / den
    o_ref[:, 0:256] = pick_heads(o8)

    nq = pbr[:, C_NQ:C_NKV]
    t0 = jnp.broadcast_to(nq[:, 0:LANES], (HROWS, LANES))
    t1 = jnp.broadcast_to(nq[:, LANES:256], (HROWS, LANES))
    t0r, t1r = pltpu.roll(t0, 64, 1), pltpu.roll(t1, 64, 1)
    lo = lane8 < 64
    hrows = [jnp.where(lo, t0, 0.0), jnp.where(lo, t0r, 0.0),
             jnp.where(lo, 0.0, t1r), jnp.where(lo, 0.0, t1)]
    qn = jnp.zeros((HROWS, LANES), F32)
    for h in range(H_NSA):
        qn = jnp.where(row8 == h, hrows[h], qn)
    qn_b = qn.astype(BF16)
    slope8 = jnp.zeros((HROWS, 1), F32)
    r81 = _iota((HROWS, 1), 0)
    for h in range(H_NSA):
        slope8 = jnp.where(r81 == h, ALIBI_SLOPES[h], slope8)
    nkv_new = sfr[:, 512:1280]
    nblk = past // NSA_BLOCK

    def new_score(k128):
        return jnp.sum(qn_b.astype(F32) * _bf16r(k128), axis=1, keepdims=True)

    sc = _nt(qn_b, kc_ref[...].astype(BF16))
    cend = _iota((HROWS, nblk), 1) * NSA_BLOCK + (NSA_BLOCK - 1)
    sc = sc - slope8 * (past - cend).astype(F32)
    ec = jnp.exp(sc - jnp.max(sc, axis=1, keepdims=True))
    pc = ec / jnp.maximum(jnp.sum(ec, axis=1, keepdims=True), 1e-30)
    o_c = _mm(pc.astype(BF16), vc_ref[...].astype(BF16))
    imp = jnp.concatenate([pc[2 * g:2 * g + 1] + pc[2 * g + 1:2 * g + 2] for g in range(G_NSA)]
                          + [jnp.zeros((HROWS - G_NSA, nblk), F32)], axis=0)
    if nblk < LANES:
        imp = jnp.concatenate([imp, jnp.zeros((HROWS, LANES - nblk), F32)], axis=1)
    forced = (lane8 == 0) | (lane8 == nblk) | (lane8 == nblk - 1)
    score = jnp.where(lane8 > nblk, -1.0, jnp.where(forced, FORCE_SCORE, imp))
    rank = jnp.zeros((HROWS, LANES), F32)
    for mblk in range(nblk + 1):
        r = score[:, mblk:mblk + 1]
        rank = rank + jnp.where((r > score) | ((r == score) & (lane8 > mblk)), 1.0, 0.0)
    sel_g = jnp.where((rank < min(NSA_TOPN, nblk + 1)) & (score >= 0.0), 1.0, 0.0)
    sel_h = jnp.zeros((HROWS, LANES), F32)
    for h in range(H_NSA):
        sel_h = jnp.where(row8 == h, sel_g[h // 2:h // 2 + 1, :], sel_h)
    valid = _mm(sel_h.astype(BF16), exp_ref[...]) > 0.5
    kst = slab(sbuf, 0, LANES)
    vst = slab(sbuf, LANES, 256)
    ss = _mm(qn_b, kst) - slope8 * (past - _iota((HROWS, past), 1)).astype(F32)
    ss = jnp.where(valid, ss, NEG_INF)
    ss_new = new_score(nkv_new[:, 256:384])
    ms = jnp.maximum(jnp.max(ss, axis=1, keepdims=True), ss_new)
    es = jnp.where(valid, jnp.exp(ss - ms), 0.0)
    es_new = jnp.exp(ss_new - ms)
    o_s = (_nt(es.astype(BF16), vst) + _bf16r(es_new) * _bf16r(nkv_new[:, 384:512])) \
        / (jnp.sum(es, axis=1, keepdims=True) + es_new)
    wl_ = win_ref.shape[1]
    win = win_ref[...]
    sw = _mm(qn_b, win[0:LANES].astype(BF16)) - slope8 * (wl_ - _iota((HROWS, wl_), 1)).astype(F32)
    sw_new = new_score(nkv_new[:, 512:640])
    mw = jnp.maximum(jnp.max(sw, axis=1, keepdims=True), sw_new)
    ew = jnp.exp(sw - mw)
    ew_new = jnp.exp(sw_new - mw)
    o_w = (_nt(ew.astype(BF16), win[LANES:256].astype(BF16))
           + _bf16r(ew_new) * _bf16r(nkv_new[:, 640:768])) / (jnp.sum(ew, axis=1, keepdims=True) + ew_new)
    rw = _iota((256, 256), 0)
    lw_ = _iota((256, 256), 1)
    new_col = jnp.sum(jnp.where(rw == lw_, nkv_new[:, 512:768], 0.0), axis=1, keepdims=True)
    _store_layer(wout_ref, jnp.where(_iota((256, wl_), 1) == wl_ - 1, new_col, pltpu.roll(win, wl_ - 1, 1)),
                 layer == 0)
    gates = [_rows_from_lanes(aux, lane8 == LANE_GATE0 + 3 * row8 + c) for c in range(3)]
    o_n = gates[0] * o_c + gates[1] * o_s + gates[2] * o_w
    o_nr = pltpu.roll(o_n, 64, 1)
    lo1 = _iota((1, LANES), 1) < 64
    o_ref[:, 256:384] = jnp.where(lo1, o_n[0:1], o_nr[1:2])
    o_ref[:, 384:512] = jnp.where(lo1, o_nr[2:3], o_n[3:4])

    qm = head_rows(pbr[:, C_MQ:C_PU])
    sm = _mm(qm, mem_ref[0:256, :].astype(BF16))
    em = jnp.exp(sm - jnp.max(sm, axis=1, keepdims=True))
    pm = em / jnp.sum(em, axis=1, keepdims=True)
    o_ref[:, 512:768] = pick_heads(_nt(pm.astype(BF16), mem_ref[256:512, :].astype(BF16)))


def _exact_mm_rhs(x, m01):
    p1, p2, p3 = _split3(x)
    return _mm(p1, m01) + _mm(p2, m01) + _mm(p3, m01)


def _attend_sample(page_flat, fox_t, lf_t, nsa_t, pb3, sf3, aux3, win_t, mem_t, kcmp, vcmp, expand,
                   *, npg, layer, stacked=()):
    depth = win_t.shape[0]
    db = pb3.shape[0]
    past = npg * PAGE_SIZE
    assert len(stacked) == (1 if layer else 0)
    per_seq = lambda blk: pl.BlockSpec((None,) + blk, lambda b, pt: (b, 0, 0))
    per_seq_layer = lambda blk: pl.BlockSpec((None, None) + blk, lambda b, pt: (layer, b, 0, 0))
    grid_spec = pltpu.PrefetchScalarGridSpec(
        num_scalar_prefetch=1, grid=(db,),
        in_specs=[pl.BlockSpec(memory_space=pl.ANY), pl.BlockSpec(memory_space=pl.ANY),
                  pl.BlockSpec(memory_space=pl.ANY),
                  per_seq((1, N_PB)), per_seq((1, 1280)), per_seq((1, LANES)),
                  per_seq_layer(win_t.shape[2:]), per_seq_layer(mem_t.shape[2:]),
                  per_seq(kcmp.shape[1:]), per_seq(vcmp.shape[1:]),
                  pl.BlockSpec(expand.shape, lambda b, pt: (0, 0))]
        + [pl.BlockSpec(memory_space=pl.ANY)] * len(stacked),
        out_specs=[per_seq((1, 768)),
                   _stacked_spec(depth, layer, (None,) + win_t.shape[2:], lambda b, pt: (b, 0, 0))],
        scratch_shapes=[pltpu.VMEM((2, npg, 512, PAGE_SIZE), F32), pltpu.VMEM((2, npg * 8, LANES), F32),
                        pltpu.VMEM((2, npg, 256, PAGE_SIZE), F32), pltpu.SemaphoreType.DMA((3, 2))])
    return pl.pallas_call(
        functools.partial(_att_sample_body, npg=npg, past=past, layer=layer),
        grid_spec=grid_spec,
        out_shape=[jax.ShapeDtypeStruct((db, 1, 768), F32), jax.ShapeDtypeStruct(win_t.shape, F32)],
        input_output_aliases={12 + k: 1 + k for k in range(len(stacked))},
        compiler_params=pltpu.CompilerParams(dimension_semantics=("arbitrary",),
                                             vmem_limit_bytes=VMEM_LIMIT),
        name="att_sample",
    )(page_flat, fox_t, lf_t, nsa_t, pb3, sf3, aux3, win_t, mem_t, kcmp, vcmp, expand, *stacked)


def _post_sample_body(oatt_ref, pz_ref, sp_ref, x_ref, wout_ref, pw_ref, ps_ref, fg_ref, y_ref,
                      spo_ref, *, rows, past, final):
    u = pz_ref[:, 0:256]
    grp, wl = _pool_window_lane((rows, 256))
    tsum = u
    for j in range(1, POOL_BUF + 1):
        tsum = tsum + jnp.where(wl > j, sp_ref[POOL_BUF - j], 0.0)
    for j in range(POOL_BUF - 1):
        spo_ref[j] = sp_ref[j + 1]
    spo_ref[POOL_BUF - 1] = u
    cnt = jnp.minimum(past + 1, wl).astype(F32)
    o_pool = _mm((tsum / cnt - u).astype(BF16), pw_ref[...]) * ps_ref[...]
    parts = [oatt_ref[:, c * LANES:(c + 1) * LANES] for c in range(4)]
    parts += [o_pool[:, 0:LANES], o_pool[:, LANES:256]]
    parts += [oatt_ref[:, 512:640], oatt_ref[:, 640:768]]
    z_of = lambda c: pz_ref[:, 256 + c * LANES:256 + (c + 1) * LANES]
    y_ref[...] = _mix_out(x_ref[...], parts, z_of, wout_ref, fg_ref, final)


def _post_sample(oatt, pz, pool_t, x2d, wout, pw_bd, ps, fg, *, past, final, layer):
    rows, d = x2d.shape
    full = lambda a: pl.BlockSpec(a.shape, lambda i: (0,) * a.ndim)
    return pl.pallas_call(
        functools.partial(_post_sample_body, rows=rows, past=past, final=final),
        grid=(1,),
        in_specs=[full(oatt), full(pz),
                  pl.BlockSpec((None, POOL_BUF, rows, 256), lambda i: (layer, 0, 0, 0)),
                  full(x2d), full(wout), full(pw_bd), full(ps), full(fg)],
        out_specs=[pl.BlockSpec((rows, d), lambda i: (0, 0)),
                   pl.BlockSpec((POOL_BUF, rows, 256), lambda i: (0, 0, 0))],
        out_shape=[jax.ShapeDtypeStruct((rows, d), F32),
                   jax.ShapeDtypeStruct((POOL_BUF, rows, 256), F32)],
        compiler_params=pltpu.CompilerParams(vmem_limit_bytes=VMEM_LIMIT),
        name="post_sample",
    )(oatt, pz, pool_t, x2d, wout, pw_bd, ps, fg)


def _prep_layer_weights(norm_g, w_in, b_fgt, nsa_pe, nsa_w1, nsa_w2, pool_w, pool_scale,
                        w_mem_kv, w_out):
    d = w_in.shape[0]
    o = 0
    segs = {}
    for name, width in (("fqkv", 768), ("ff", H_FOX), ("nq", 256), ("nkv", 768),
                        ("ng", 3 * H_NSA), ("pu", 256), ("mq", 256), ("z", d)):
        segs[name] = w_in[:, o:o + width]
        o += width
    pad = jnp.zeros((d, LANES - H_FOX - 3 * H_NSA), w_in.dtype)
    wp = jnp.concatenate([segs["fqkv"], segs["nq"], segs["nkv"], segs["mq"], segs["pu"],
                          segs["z"], segs["ff"], segs["ng"], pad], axis=1).astype(BF16)
    bvec = jnp.zeros((1, LANES), F32).at[0, 0:H_FOX].set(b_fgt.astype(F32))
    gw = D_GROUP // len(POOL_WINDOWS)
    pw_bd = jnp.zeros((D_GROUP, D_GROUP), F32)
    for g in range(len(POOL_WINDOWS)):
        pw_bd = pw_bd.at[g * gw:(g + 1) * gw, g * gw:(g + 1) * gw].set(pool_w[g])
    return dict(g=norm_g.reshape(1, d), wp=wp, bvec=bvec,
                pe2=jnp.concatenate([nsa_pe, nsa_pe], axis=2).astype(F32),
                w1dup=jnp.concatenate([nsa_w1.reshape(2, NSA_BLOCK, HEAD_DIM, -1)] * 2, axis=2).astype(BF16),
                w2=nsa_w2.astype(BF16),
                w1q=jnp.transpose(nsa_w1.reshape(2, NSA_BLOCK, HEAD_DIM, -1), (0, 2, 1, 3)).reshape(
                    2, HEAD_DIM // DQ, DQ * NSA_BLOCK, -1).astype(BF16),
                pe_t=jnp.concatenate([jnp.transpose(nsa_pe, (0, 2, 1))] * 2, axis=2).astype(F32),
                w2a=jnp.pad(nsa_w2, ((0, 0), (0, 0), (0, HEAD_DIM))).astype(BF16),
                w2b=jnp.pad(nsa_w2, ((0, 0), (0, 0), (HEAD_DIM, 0))).astype(BF16),
                pw_bd=pw_bd.astype(BF16), ps=pool_scale.reshape(1, D_GROUP).astype(F32),
                wmkv=w_mem_kv.astype(BF16), wout=w_out.astype(BF16))


def _layer_prompt(x, mem, lw, fg, tile, layer, depth, stacked):
    b, t, d = x.shape
    n = b * t
    nb = t // NSA_BLOCK
    final = layer == depth - 1
    pb, pz, aux, kvc, ct, lft, fkvt, nkvt, nwt = _project(
        x.reshape(n, d), lw["g"], lw["wp"], lw["bvec"], tm=tile, tiles_per_seq=t // tile, prompt=True,
        layer=layer, depth=depth, stacked=stacked)
    pb3 = pb.reshape(b, t, N_PB)
    o_a = _fox_prompt(pb3, ct, tq=tile)
    cmp = _compress_prompt(kvc, lw["pe2"], lw["w1dup"], lw["w2"]).reshape(2, G_NSA, b, nb, HEAD_DIM)
    cmp = jnp.transpose(cmp, (0, 2, 3, 1, 4)).reshape(2, b, nb, LANES)
    cmp = jnp.pad(cmp, ((0, 0), (0, 0), (0, NBLK_PAD - nb), (0, 0))).astype(BF16)
    o_b = _nsa_prompt(pb3, cmp[0], cmp[1], aux.reshape(b, t, LANES), tq=tile, nb=nb)
    n_mem = mem.shape[1]
    mem_t, memb = _mem_project(mem, lw["wmkv"])
    pz3 = pz.reshape(b, t, 1280)
    y = _post_prompt(o_a, o_b, pz3, pb3, memb, x, lw["wout"], lw["pw_bd"], lw["ps"], fg,
                     tm=tile, final=final)
    w_keep = min(NSA_WINDOW, t)
    state = (jnp.transpose(lft[:, 0:H_FOX, :], (0, 2, 1)),
             jnp.transpose(nwt[:, :, t - w_keep:].reshape(b, 2, G_NSA, HEAD_DIM, w_keep), POS_FIRST),
             pz3[:, t - POOL_BUF:, 0:256],
             jnp.transpose(mem_t.reshape(b, 2, H_MEM, HEAD_DIM, n_mem), POS_FIRST))
    return y, state, (fkvt, nkvt)


POS_FIRST = (0, 4, 1, 2, 3)


def _native_views(cache_fox_kv, cache_fox_lf, cache_nsa_kv, state_nsa_win, state_pool, cache_mem_kv):
    dp, n_phys = cache_fox_kv.shape[:2]
    db = state_nsa_win.shape[1]
    to_t = (0, 1, 3, 4, 5, 2)
    fox_t = jnp.transpose(cache_fox_kv, to_t).reshape(dp, n_phys, 512, PAGE_SIZE)
    lf_t = jnp.pad(jnp.transpose(cache_fox_lf.astype(F32), (0, 1, 3, 2)),
                   ((0, 0), (0, 0), (0, 8 - H_FOX), (0, 0)))
    nsa_t = jnp.transpose(cache_nsa_kv, to_t).reshape(dp, n_phys, 512, PAGE_SIZE)
    win_t = jnp.transpose(state_nsa_win, to_t).reshape(dp, db, 256, state_nsa_win.shape[2])
    mem_t = jnp.transpose(cache_mem_kv, to_t).reshape(dp, db, 512, cache_mem_kv.shape[2])
    pool_t = jnp.transpose(state_pool, (0, 2, 1, 3))
    return dict(fox_t=fox_t, lf_t=lf_t, nsa_t=nsa_t, win_t=win_t, mem_t=mem_t, pool_t=pool_t)


def _layer_sample(x, nv, page_table, lw, fg, layer, depth, stacked):
    db, ds, d = x.shape
    npg = page_table.shape[1]
    past = npg * PAGE_SIZE
    nblk = past // NSA_BLOCK
    final = layer == depth - 1
    pb, pz, aux, sf, sft, auxt = _project(x.reshape(db, d), lw["g"], lw["wp"], lw["bvec"],
                                          tm=db, tiles_per_seq=1, prompt=False)
    page_flat = page_table.reshape(db * npg).astype(I32)
    cmp = _compress_sample(page_flat, nv["nsa_t"], lw["pe_t"], lw["w1q"], lw["w2a"], lw["w2b"],
                           pages=min(db * npg, 128), layer=layer)
    cmp = jnp.transpose(cmp.reshape(2, db, npg, G_NSA, PAGE_SIZE // NSA_BLOCK, HEAD_DIM),
                        (0, 1, 2, 4, 3, 5)).reshape(2, db, nblk, LANES)
    kcmp, vcmp = cmp[0], cmp[1]
    expand = (np.arange(LANES)[:, None] == (np.arange(past)[None, :] // NSA_BLOCK)).astype(np.float32)
    oatt, win_new = _attend_sample(
        page_flat, nv["fox_t"], nv["lf_t"], nv["nsa_t"], pb.reshape(db, 1, N_PB),
        sf.reshape(db, 1, 1280), aux.reshape(db, 1, LANES), nv["win_t"], nv["mem_t"], kcmp, vcmp,
        jnp.asarray(expand, BF16), npg=npg, layer=layer, stacked=stacked)
    y, pool_new = _post_sample(oatt.reshape(db, 768), pz, nv["pool_t"], x.reshape(db, d), lw["wout"],
                               lw["pw_bd"], lw["ps"], fg, past=past, final=final, layer=layer)
    seq_first = (3, 0, 1, 2)
    state = (jnp.transpose(sft[0:512].reshape(2, H_FOX, HEAD_DIM, db), seq_first)[:, None],
             jnp.transpose(auxt[0:H_FOX])[:, None, :],
             jnp.transpose(sft[512:1024].reshape(4, G_NSA, HEAD_DIM, db), seq_first)[:, None],
             jnp.transpose(pool_new, (1, 0, 2)))
    return y.reshape(db, ds, d), state, (win_new,)


def _from_t(x_t, a, b):
    dp, bt, _, pos = x_t.shape
    return jnp.transpose(x_t.reshape(dp, bt, a, b, HEAD_DIM, pos), (0, 1, 5, 2, 3, 4))


def kernel(x_prompt, x_sample, cache_fox_kv, cache_fox_lf, cache_nsa_kv, state_nsa_win, state_pool,
           cache_mem_kv, page_table, mem_prompt, norm_g, w_in, b_fgt, nsa_pe, nsa_w1, nsa_w2, pool_w,
           pool_scale, w_mem_kv, w_out, final_g):
    depth = norm_g.shape[0]
    t = x_prompt.shape[1]
    assert x_sample.shape[1] == 1 and t % NSA_WINDOW == 0 and t // NSA_BLOCK <= NBLK_PAD
    hp, hs = x_prompt, x_sample
    fg = final_g.reshape(1, -1).astype(F32)
    nv = _native_views(cache_fox_kv, cache_fox_lf, cache_nsa_kv, state_nsa_win, state_pool,
                       cache_mem_kv)
    sp = [[] for _ in range(4)]
    ss = [[] for _ in range(4)]
    stk_p, stk_s = (), ()
    for l in range(depth):
        lw = _prep_layer_weights(norm_g[l], w_in[l], b_fgt[l], nsa_pe[l], nsa_w1[l], nsa_w2[l],
                                 pool_w[l], pool_scale[l], w_mem_kv[l], w_out[l])
        hp, st_p, stk_p = _layer_prompt(hp, mem_prompt, lw, fg, NSA_WINDOW, l, depth, stk_p)
        hs, st_s, stk_s = _layer_sample(hs, nv, page_table, lw, fg, l, depth, stk_s)
        for acc, s in zip(sp, st_p):
            acc.append(s)
        for acc, s in zip(ss, st_s):
            acc.append(s)
    fox_lf_p, nsa_win_p, pool_p, mem_kv_p = (jnp.stack(a) for a in sp)
    fox_kv_s, fox_lf_s, nsa_kv_s, pool_s = (jnp.stack(a) for a in ss)
    return (hp, hs, _from_t(stk_p[0], 2, H_FOX), fox_lf_p, _from_t(stk_p[1], 4, G_NSA), nsa_win_p, pool_p,
            mem_kv_p, fox_kv_s, fox_lf_s, nsa_kv_s, _from_t(stk_s[0], 2, G_NSA), pool_s)
```

```python
import functools

import numpy as np
import jax
import jax.numpy as jnp
from jax import lax
from jax.experimental import pallas as pl
from jax.experimental.pallas import tpu as pltpu

F32 = jnp.float32
BF16 = jnp.bfloat16
I32 = jnp.int32

HEAD_DIM = 64
H_FOX = 4
H_NSA = 4
G_NSA = 2
H_MEM = 4
D_GROUP = 256
POOL_WINDOWS = (2, 4, 8, 16)
POOL_BUF = 15
NSA_BLOCK = 64
NSA_TOPN = 16
NSA_WINDOW = 512
PAGE_SIZE = 128
RMS_EPS = 1e-6
NEG_INF = -1e30
FORCE_SCORE = 1e4
SCALE = HEAD_DIM ** -0.5
ALIBI_SLOPES = tuple(2.0 ** (-8.0 * (h + 1) / H_NSA) for h in range(H_NSA))

LANES = 128
NBLK_PAD = 128
VMEM_LIMIT = 56 * 1024 * 1024

C_FQ, C_FK, C_FV, C_NQ, C_NKV, C_MQ, C_PU, C_Z, C_SM, C_END = (
    0, 256, 512, 768, 1024, 1792, 2048, 2304, 3328, 3456)
N_PB = C_PU
LANE_GATE0 = 4


def _nt(a, b):
    return lax.dot_general(a, b, (((1,), (1,)), ((), ())), preferred_element_type=F32)


def _mm(a, b):
    return jnp.dot(a, b, preferred_element_type=F32)


def _split3(x):
    p1 = x.astype(BF16)
    r1 = x - p1.astype(F32)
    p2 = r1.astype(BF16)
    p3 = (r1 - p2.astype(F32)).astype(BF16)
    return p1, p2, p3


def _exact_mm(m01, x):
    p1, p2, p3 = _split3(x)
    return _mm(m01, p1) + _mm(m01, p2) + _mm(m01, p3)


def _sigmoid(x):
    return 1.0 / (1.0 + jnp.exp(-x))


def _log_sigmoid(x):
    return jnp.minimum(x, 0.0) - jnp.log1p(jnp.exp(-jnp.abs(x)))


def _iota(shape, dim):
    return lax.broadcasted_iota(I32, shape, dim)


def _div_pow2(x, n):
    assert n & (n - 1) == 0
    return jnp.right_shift(x, n.bit_length() - 1)


def _mod_pow2(x, n):
    assert n & (n - 1) == 0
    return jnp.bitwise_and(x, n - 1)


def _store_layer(ref, value, first_layer):
    if first_layer:
        ref[0] = value
        for k in range(1, ref.shape[0]):
            ref[k] = jnp.zeros(value.shape, value.dtype)
    else:
        ref[...] = value


def _proj_body(x_ref, g_ref, w_ref, b_ref, *rest, tm, tiles_per_seq, prompt, n_alias):
    pb_ref, pz_ref, aux_ref, *rest = rest[n_alias:]
    x = x_ref[...]
    ms = jnp.mean(x * x, axis=-1, keepdims=True)
    xn = (x * lax.rsqrt(ms + RMS_EPS) * g_ref[...]).astype(BF16)

    def seg(a, b):
        return _mm(xn, w_ref[:, a:b])

    pb_ref[:, C_FQ:C_FK] = (seg(C_FQ, C_FK) * SCALE).astype(BF16)
    fkv = seg(C_FK, C_NQ)
    pb_ref[:, C_FK:C_NQ] = fkv.astype(BF16)
    pb_ref[:, C_NQ:C_NKV] = (seg(C_NQ, C_NKV) * SCALE).astype(BF16)
    nkv = seg(C_NKV, C_MQ)
    pb_ref[:, C_NKV:C_MQ] = nkv.astype(BF16)
    pb_ref[:, C_MQ:C_PU] = (seg(C_MQ, C_PU) * SCALE).astype(BF16)
    pz_ref[...] = seg(C_PU, C_SM)
    small = seg(C_SM, C_END)

    lane = _iota((tm, LANES), 1)
    lf = _log_sigmoid(small + b_ref[...])
    aux = jnp.where(lane < H_FOX, lf, jnp.where(lane < LANE_GATE0 + 3 * H_NSA, _sigmoid(small), 0.0))
    aux_ref[...] = aux
    if prompt:
        kvc_ref, ct_ref, lft_ref, fkvt_ref, nkvt_ref, nwt_ref, carry_ref = rest
        kvc_ref[0] = nkv[:, 0:LANES]
        kvc_ref[1] = nkv[:, LANES:256]
        _store_layer(fkvt_ref, fkv.T, n_alias == 0)
        _store_layer(nkvt_ref, nkv[:, 0:512].T, n_alias == 0)
        nwt_ref[...] = nkv[:, 512:768].T
        lfm = jnp.where(lane < H_FOX, lf, 0.0)
        lft_ref[...] = lfm.T[0:8, :]
        tri = (_iota((LANES, LANES), 0) >= _iota((LANES, LANES), 1)).astype(BF16)
        first = (pl.program_id(0) % tiles_per_seq) == 0
        carry = jnp.where(first, 0.0, carry_ref[0:1, :])
        for r in range(tm // LANES):
            blk = lfm[r * LANES:(r + 1) * LANES]
            cblk = _exact_mm(tri, blk) + carry
            carry = cblk[LANES - 1:LANES, :]
            ct_ref[:, r * LANES:(r + 1) * LANES] = cblk.T[0:8, :]
        carry_ref[0:1, :] = carry
    else:
        sf_ref, sft_ref, auxt_ref = rest
        sf_ref[:, 0:512] = fkv
        sf_ref[:, 512:1280] = nkv
        sft_ref[0:512, :] = fkv.T
        sft_ref[512:1280, :] = nkv.T
        auxt_ref[...] = aux.T


def _stacked_spec(depth, layer, blk, index):
    if layer == 0:
        return pl.BlockSpec((depth,) + blk, lambda *a: (0,) + index(*a))
    return pl.BlockSpec((None,) + blk, lambda *a: (layer,) + index(*a))


def _project(x2d, g, wp, bvec, *, tm, tiles_per_seq, prompt, layer=0, depth=1, stacked=()):
    n, d = x2d.shape
    grid = (n // tm,)
    row = lambda i: (i, 0)
    const = lambda i: (0, 0)
    out_shape = [jax.ShapeDtypeStruct((n, N_PB), BF16), jax.ShapeDtypeStruct((n, 1280), F32),
                 jax.ShapeDtypeStruct((n, LANES), F32)]
    out_specs = [pl.BlockSpec((tm, N_PB), row), pl.BlockSpec((tm, 1280), row),
                 pl.BlockSpec((tm, LANES), row)]
    scratch = []
    aliases = {}
    if prompt:
        assert len(stacked) == (2 if layer else 0)
        nseq = n // (tm * tiles_per_seq)
        t = tm * tiles_per_seq
        seq_idx = lambda i: (i // tiles_per_seq, 0, i % tiles_per_seq)
        seq_t = lambda rows: pl.BlockSpec((None, rows, tm), seq_idx)
        lay_t = lambda rows: _stacked_spec(depth, layer, (None, rows, tm), seq_idx)
        out_shape += [jax.ShapeDtypeStruct((2, n, LANES), F32),
                      jax.ShapeDtypeStruct((nseq, 8, t), F32), jax.ShapeDtypeStruct((nseq, 8, t), F32),
                      jax.ShapeDtypeStruct((depth, nseq, 512, t), F32),
                      jax.ShapeDtypeStruct((depth, nseq, 512, t), F32),
                      jax.ShapeDtypeStruct((nseq, 256, t), F32)]
        out_specs += [pl.BlockSpec((2, tm, LANES), lambda i: (0, i, 0)),
                      seq_t(8), seq_t(8), lay_t(512), lay_t(512), seq_t(256)]
        scratch = [pltpu.VMEM((8, LANES), F32)]
        aliases = {4 + k: 6 + k for k in range(len(stacked))}
    else:
        assert grid == (1,) and not stacked
        out_shape += [jax.ShapeDtypeStruct((n, 1280), F32), jax.ShapeDtypeStruct((1280, n), F32),
                      jax.ShapeDtypeStruct((LANES, n), F32)]
        out_specs += [pl.BlockSpec((tm, 1280), row), pl.BlockSpec((1280, tm), const),
                      pl.BlockSpec((LANES, tm), const)]
    return pl.pallas_call(
        functools.partial(_proj_body, tm=tm, tiles_per_seq=tiles_per_seq, prompt=prompt,
                          n_alias=len(stacked)),
        grid=grid,
        in_specs=[pl.BlockSpec((tm, d), row), pl.BlockSpec((1, d), const),
                  pl.BlockSpec((d, C_END), const), pl.BlockSpec((1, LANES), const)]
        + [pl.BlockSpec(memory_space=pl.ANY)] * len(stacked),
        out_specs=out_specs, out_shape=out_shape, scratch_shapes=scratch,
        input_output_aliases=aliases,
        compiler_params=pltpu.CompilerParams(dimension_semantics=("arbitrary",),
                                             vmem_limit_bytes=VMEM_LIMIT),
        name="proj",
    )(x2d, g, wp, bvec, *stacked)


def _memkv_body(m_ref, w_ref, ot_ref, ob_ref):
    r = _mm(m_ref[...].astype(BF16), w_ref[...])
    ot_ref[...] = r.T
    ob_ref[...] = r.astype(BF16)


def _mem_project(mem3, w):
    b, n_mem, d = mem3.shape
    e = w.shape[1]
    return pl.pallas_call(
        _memkv_body, grid=(b,),
        in_specs=[pl.BlockSpec((None, n_mem, d), lambda i: (i, 0, 0)),
                  pl.BlockSpec((d, e), lambda i: (0, 0))],
        out_specs=[pl.BlockSpec((None, e, n_mem), lambda i: (i, 0, 0)),
                   pl.BlockSpec((None, n_mem, e), lambda i: (i, 0, 0))],
        out_shape=[jax.ShapeDtypeStruct((b, e, n_mem), F32), jax.ShapeDtypeStruct((b, n_mem, e), BF16)],
        name="memkv",
    )(mem3, w)


def _fox_body(qt_ref, kt_ref, q_ref, k_ref, v_ref, ct_ref, o_ref, m_ref, l_ref, acc_ref, *, tq, tk):
    qi = qt_ref[pl.program_id(1)]
    ki = kt_ref[pl.program_id(1)]

    @pl.when(ki == 0)
    def _():
        m_ref[...] = jnp.full(m_ref.shape, NEG_INF, F32)
        l_ref[...] = jnp.zeros(l_ref.shape, F32)
        acc_ref[...] = jnp.zeros(acc_ref.shape, F32)

    lane = _iota((tq, LANES), 1)

    def step(masked):
        if masked:
            causal = _iota((tq, tk), 0) >= _iota((tq, tk), 1)
        scores = []
        for h in range(H_FOX):
            pr, hh = divmod(h, 2)
            cs = slice(pr * LANES, (pr + 1) * LANES)
            q2 = q_ref[:, cs]
            qm = jnp.where((lane >= 64 * hh) & (lane < 64 * hh + 64), q2, jnp.zeros_like(q2))
            scores.append(_nt(qm, k_ref[:, cs]))
        probs = []
        for h in range(H_FOX):
            s = scores[h] - ct_ref[h:h + 1, :]
            if masked:
                s = jnp.where(causal, s, NEG_INF)
            m_prev = m_ref[h]
            m_new = jnp.maximum(m_prev, jnp.max(s, axis=1, keepdims=True))
            alpha = jnp.exp(m_prev - m_new)
            p = jnp.exp(s - jnp.concatenate([m_new] * (tk // LANES), axis=1))
            l_ref[h] = alpha * l_ref[h] + jnp.sum(p, axis=1, keepdims=True)
            m_ref[h] = m_new
            probs.append((p.astype(BF16), alpha))
        for h in range(H_FOX):
            cs = slice((h // 2) * LANES, (h // 2 + 1) * LANES)
            p, alpha = probs[h]
            acc_ref[h] = alpha * acc_ref[h] + _mm(p, v_ref[:, cs])

    @pl.when(ki < qi)
    def _():
        step(False)

    @pl.when(ki == qi)
    def _():
        step(True)
        for pr in range(H_FOX // 2):
            o0 = acc_ref[2 * pr] / l_ref[2 * pr]
            o1 = acc_ref[2 * pr + 1] / l_ref[2 * pr + 1]
            o_ref[:, pr * LANES:(pr + 1) * LANES] = jnp.where(lane < 64, o0, o1)


def _causal_pairs(nq):
    pairs = [(qi, ki) for qi in range(nq) for ki in range(qi + 1)]
    return (jnp.asarray([p[0] for p in pairs], I32), jnp.asarray([p[1] for p in pairs], I32))


def _fox_prompt(pb3, ct, *, tq):
    b, t, _ = pb3.shape
    qt, kt = _causal_pairs(t // tq)
    grid_spec = pltpu.PrefetchScalarGridSpec(
        num_scalar_prefetch=2, grid=(b, qt.shape[0]),
        in_specs=[pl.BlockSpec((None, tq, 256), lambda bi, s, qt, kt: (bi, qt[s], C_FQ // 256)),
                  pl.BlockSpec((None, tq, 256), lambda bi, s, qt, kt: (bi, kt[s], C_FK // 256)),
                  pl.BlockSpec((None, tq, 256), lambda bi, s, qt, kt: (bi, kt[s], C_FV // 256)),
                  pl.BlockSpec((None, 8, tq), lambda bi, s, qt, kt: (bi, 0, kt[s]))],
        out_specs=pl.BlockSpec((None, tq, 256), lambda bi, s, qt, kt: (bi, qt[s], 0)),
        scratch_shapes=[pltpu.VMEM((H_FOX, tq, LANES), F32), pltpu.VMEM((H_FOX, tq, LANES), F32),
                        pltpu.VMEM((H_FOX, tq, LANES), F32)])
    return pl.pallas_call(
        functools.partial(_fox_body, tq=tq, tk=tq),
        grid_spec=grid_spec,
        out_shape=jax.ShapeDtypeStruct((b, t, 256), F32),
        compiler_params=pltpu.CompilerParams(
            dimension_semantics=("parallel", "arbitrary"), vmem_limit_bytes=VMEM_LIMIT),
        name="fox_prompt",
    )(qt, kt, pb3, pb3, pb3, ct)


def _cmp_body(x_ref, pe_ref, w1_ref, w2_ref, o_ref, acc_ref, *, m):
    lo = _iota((m, LANES), 1) < HEAD_DIM
    acc_ref[...] = jnp.zeros(acc_ref.shape, F32)
    for pos in range(NSA_BLOCK):
        x = x_ref[pl.ds(pos, m, stride=NSA_BLOCK), :] + pe_ref[pos:pos + 1, :]
        a = jnp.concatenate([jnp.where(lo, x, 0.0), jnp.where(lo, 0.0, x)], axis=0)
        acc_ref[...] += _mm(a.astype(BF16), w1_ref[pos])
    h = acc_ref[...]
    o_ref[...] = _mm((h * _sigmoid(h)).astype(BF16), w2_ref[...])


def _compress_prompt(kvc, pe2, w1dup, w2):
    _, n, _ = kvc.shape
    m = n // NSA_BLOCK
    by_j = lambda blk: pl.BlockSpec((None,) + blk, lambda j: (j,) + (0,) * len(blk))
    return pl.pallas_call(
        functools.partial(_cmp_body, m=m), grid=(2,),
        in_specs=[by_j((n, LANES)), by_j((NSA_BLOCK, LANES)), by_j((NSA_BLOCK, LANES, 256)),
                  by_j((256, HEAD_DIM))],
        out_specs=by_j((G_NSA * m, HEAD_DIM)),
        out_shape=jax.ShapeDtypeStruct((2, G_NSA * m, HEAD_DIM), F32),
        scratch_shapes=[pltpu.VMEM((G_NSA * m, 256), F32)],
        compiler_params=pltpu.CompilerParams(vmem_limit_bytes=VMEM_LIMIT),
        name="cmp_prompt",
    )(kvc, pe2, w1dup, w2)


def _masked_softmax_cols(s, valid):
    s = jnp.where(valid, s, NEG_INF)
    e = jnp.where(valid, jnp.exp(s - jnp.max(s, axis=0, keepdims=True)), 0.0)
    return e / jnp.maximum(jnp.sum(e, axis=0, keepdims=True), 1e-30)


def _select_blocks(score, nb, topn):
    nbp, n = score.shape
    blk = _iota((nbp, n), 0)
    rank = jnp.zeros((nbp, n), F32)
    for m in range(nb):
        r = score[m:m + 1, :]
        rank = rank + jnp.where((r > score) | ((r == score) & (blk > m)), 1.0, 0.0)
    return jnp.where((rank < topn) & (score >= 0.0), 1.0, 0.0)


def _nsa_body(qt_ref, kt_ref, q_ref, ks_ref, vs_ref, kwc_ref, kwp_ref, vwc_ref, vwp_ref, kc_ref, vc_ref,
              aux_ref, o_ref, sel_ref, m_ref, l_ref, acc_ref, oc_ref, ow_ref, *, tq, nb, topn):
    qi = qt_ref[pl.program_id(1)]
    ki = kt_ref[pl.program_id(1)]
    q0 = qi * tq
    tk = tq
    lane = _iota((tq, LANES), 1)

    def qstack(g):
        q2 = q_ref[:, g * LANES:(g + 1) * LANES].astype(F32)
        q2r = pltpu.roll(q2, 64, 1)
        ing = (lane >= 64 * g) & (lane < 64 * g + 64)
        qa, qb = (q2, q2r) if g == 0 else (q2r, q2)
        return jnp.concatenate([jnp.where(ing, qa, 0.0), jnp.where(ing, qb, 0.0)],
                               axis=0).astype(BF16)

    def biased(raw, g, kd, valid):
        sa = jnp.where(valid, raw[:tq] + ALIBI_SLOPES[2 * g] * kd, NEG_INF)
        sb = jnp.where(valid, raw[tq:] + ALIBI_SLOPES[2 * g + 1] * kd, NEG_INF)
        return jnp.concatenate([sa, sb], axis=0)

    def lanes(x, n):
        return jnp.concatenate([x] * (n // LANES), axis=1)

    @pl.when(ki == 0)
    def _():
        m_ref[...] = jnp.full(m_ref.shape, NEG_INF, F32)
        l_ref[...] = jnp.zeros(l_ref.shape, F32)
        acc_ref[...] = jnp.zeros(acc_ref.shape, F32)
        dqk = _iota((tq, tk), 0) - _iota((tq, tk), 1)
        kd_cur = _iota((1, tk), 1).astype(F32)
        for g in range(G_NSA):
            qs = qstack(g)
            st = _nt(kc_ref[...], qs)
            blk = _iota((NBLK_PAD, 2 * tq), 0)
            col = _iota((NBLK_PAD, 2 * tq), 1)
            qpos = q0 + jnp.where(col >= tq, col - tq, col)
            slope = jnp.where(col < tq, ALIBI_SLOPES[2 * g], ALIBI_SLOPES[2 * g + 1])
            cend = blk * NSA_BLOCK + (NSA_BLOCK - 1)
            st = st + slope * (cend - q0).astype(F32)
            pt = _masked_softmax_cols(st, (cend <= qpos) & (blk < nb))
            oc_ref[g] = _mm(pt.T.astype(BF16), vc_ref[...])
            imp = pt[:, :tq] + pt[:, tq:]
            nbp = -(-nb // 8) * 8
            blk2 = _iota((nbp, tq), 0)
            cur = _div_pow2(q0 + _iota((nbp, tq), 1), NSA_BLOCK)
            forced = (blk2 == 0) | (blk2 == cur) | (blk2 == cur - 1)
            score = jnp.where(blk2 > cur, -1.0, jnp.where(forced, FORCE_SCORE, imp[:nbp]))
            selt = _select_blocks(score, nb, topn)
            if nbp < NBLK_PAD:
                selt = jnp.concatenate([selt, jnp.zeros((NBLK_PAD - nbp, tq), F32)], axis=0)
            sel_ref[g] = selt.T.astype(BF16)
            s1 = biased(_nt(qs, kwc_ref[...]), g, kd_cur, dqk >= 0)
            s2 = biased(_nt(qs, kwp_ref[...]), g, kd_cur - tq, (dqk + tq <= NSA_WINDOW) & (qi > 0))
            mw = jnp.maximum(jnp.max(s1, axis=1, keepdims=True), jnp.max(s2, axis=1, keepdims=True))
            mw = jnp.broadcast_to(mw, (2 * tq, LANES))
            e1 = jnp.exp(s1 - lanes(mw, tk))
            e2 = jnp.exp(s2 - lanes(mw, tk))
            lw = jnp.sum(e1, axis=1, keepdims=True) + jnp.sum(e2, axis=1, keepdims=True)
            ow_ref[g] = (_mm(e1.astype(BF16), vwc_ref[...]) + _mm(e2.astype(BF16), vwp_ref[...])) \
                / jnp.maximum(lw, 1e-30)

    def sel_step(diag):
        expand = (_iota((NBLK_PAD, tk), 0) ==
                  ki * (tk // NSA_BLOCK) + _div_pow2(_iota((NBLK_PAD, tk), 1), NSA_BLOCK)).astype(BF16)
        kd = (_iota((1, tk), 1) + (ki * tk - q0)).astype(F32)
        raws = [(_nt(qstack(g), ks_ref[...]), _mm(sel_ref[g], expand)) for g in range(G_NSA)]
        probs = []
        for g in range(G_NSA):
            raw, picked = raws[g]
            valid = picked > 0.5
            if diag:
                valid = valid & (_iota((tq, tk), 0) >= _iota((tq, tk), 1))
            s = biased(raw, g, kd, valid)
            m_prev = m_ref[g]
            m_new = jnp.maximum(m_prev, jnp.max(s, axis=1, keepdims=True))
            alpha = jnp.exp(m_prev - m_new)
            p = jnp.exp(s - lanes(m_new, tk))
            l_ref[g] = alpha * l_ref[g] + jnp.sum(p, axis=1, keepdims=True)
            m_ref[g] = m_new
            probs.append((p.astype(BF16), alpha))
        for g in range(G_NSA):
            p, alpha = probs[g]
            acc_ref[g] = alpha * acc_ref[g] + _mm(p, vs_ref[...])

    @pl.when(ki < qi)
    def _():
        sel_step(False)

    @pl.when(ki == qi)
    def _():
        sel_step(True)
        for g in range(G_NSA):
            o_s = acc_ref[g] / jnp.maximum(l_ref[g], 1e-30)
            o_c = oc_ref[g]
            o_w = ow_ref[g]
            outs = []
            for hh in range(2):
                h = 2 * g + hh
                rs = slice(hh * tq, (hh + 1) * tq)
                gl = LANE_GATE0 + 3 * h
                outs.append(aux_ref[:, gl:gl + 1] * o_c[rs] + aux_ref[:, gl + 1:gl + 2] * o_s[rs]
                            + aux_ref[:, gl + 2:gl + 3] * o_w[rs])
            oa, ob = outs
            if g == 0:
                ob = pltpu.roll(ob, 64, 1)
            else:
                oa = pltpu.roll(oa, 64, 1)
            o_ref[:, g * LANES:(g + 1) * LANES] = jnp.where(lane < 64, oa, ob)


def _nsa_prompt(pb3, kcmp, vcmp, aux3, *, tq, nb):
    b, t, _ = pb3.shape
    qt, kt = _causal_pairs(t // tq)
    ct = C_NKV // LANES
    cur = lambda c: (lambda bi, s, qt, kt: (bi, qt[s], c))
    prev = lambda c: (lambda bi, s, qt, kt: (bi, jnp.maximum(qt[s] - 1, 0), c))
    kvt = lambda c: (lambda bi, s, qt, kt: (bi, kt[s], c))
    cmp_spec = pl.BlockSpec((None, NBLK_PAD, LANES), lambda bi, s, qt, kt: (bi, 0, 0))
    tile = lambda f: pl.BlockSpec((None, tq, LANES), f)
    grid_spec = pltpu.PrefetchScalarGridSpec(
        num_scalar_prefetch=2, grid=(b, qt.shape[0]),
        in_specs=[pl.BlockSpec((None, tq, 256), lambda bi, s, qt, kt: (bi, qt[s], C_NQ // 256)),
                  tile(kvt(ct + 2)), tile(kvt(ct + 3)),
                  tile(cur(ct + 4)), tile(prev(ct + 4)), tile(cur(ct + 5)), tile(prev(ct + 5)),
                  cmp_spec, cmp_spec,
                  pl.BlockSpec((None, tq, LANES), lambda bi, s, qt, kt: (bi, qt[s], 0))],
        out_specs=pl.BlockSpec((None, tq, 256), lambda bi, s, qt, kt: (bi, qt[s], 0)),
        scratch_shapes=[pltpu.VMEM((G_NSA, tq, NBLK_PAD), BF16),
                        pltpu.VMEM((G_NSA, 2 * tq, LANES), F32), pltpu.VMEM((G_NSA, 2 * tq, LANES), F32),
                        pltpu.VMEM((G_NSA, 2 * tq, LANES), F32),
                        pltpu.VMEM((G_NSA, 2 * tq, LANES), F32),
                        pltpu.VMEM((G_NSA, 2 * tq, LANES), F32)])
    return pl.pallas_call(
        functools.partial(_nsa_body, tq=tq, nb=nb, topn=min(NSA_TOPN, nb)),
        grid_spec=grid_spec,
        out_shape=jax.ShapeDtypeStruct((b, t, 256), F32),
        compiler_params=pltpu.CompilerParams(
            dimension_semantics=("parallel", "arbitrary"), vmem_limit_bytes=VMEM_LIMIT),
        name="nsa_prompt",
    )(qt, kt, pb3, pb3, pb3, pb3, pb3, pb3, pb3, kcmp, vcmp, aux3)


def _mem_attend(mq_ref, mk_ref, mv_ref, rows):
    lane = _iota((rows, LANES), 1)
    n_mem = mk_ref.shape[0]
    scores = []
    for h in range(H_MEM):
        pr, hh = divmod(h, 2)
        cs = slice(pr * LANES, (pr + 1) * LANES)
        q2 = mq_ref[:, cs]
        qm = jnp.where((lane >= 64 * hh) & (lane < 64 * hh + 64), q2, jnp.zeros_like(q2))
        scores.append(_nt(qm, mk_ref[:, cs]))
    weights = []
    for h in range(H_MEM):
        s = scores[h]
        mx = jnp.broadcast_to(jnp.max(s, axis=1, keepdims=True), (rows, LANES))
        e = jnp.exp(s - jnp.concatenate([mx] * (n_mem // LANES), axis=1))
        weights.append((e.astype(BF16), jnp.sum(e, axis=1, keepdims=True)))
    outs = []
    for h in range(H_MEM):
        e, den = weights[h]
        cs = slice((h // 2) * LANES, (h // 2 + 1) * LANES)
        outs.append(_mm(e, mv_ref[:, cs]) / den)
    return [jnp.where(lane < 64, outs[2 * pr], outs[2 * pr + 1]) for pr in range(H_MEM // 2)]


def _mix_out(x, parts, z_of, wout_ref, fg_ref, final):
    y = x
    for c in range(0, len(parts), 2):
        gated = []
        for cc in (c, c + 1):
            z = z_of(cc)
            gated.append((parts[cc] * (z * _sigmoid(z))).astype(BF16))
        y = y + _mm(jnp.concatenate(gated, axis=1), wout_ref[c * LANES:(c + 2) * LANES, :])
    if final:
        y = y * lax.rsqrt(jnp.mean(y * y, axis=-1, keepdims=True) + RMS_EPS) * fg_ref[...]
    return y


def _pool_window_lane(shape):
    grp = _div_pow2(_iota(shape, 1), D_GROUP // len(POOL_WINDOWS))
    wl = jnp.where(grp == 0, POOL_WINDOWS[0], jnp.where(grp == 1, POOL_WINDOWS[1],
                   jnp.where(grp == 2, POOL_WINDOWS[2], POOL_WINDOWS[3])))
    return grp, wl


HALO = 32


def _post_body(oa_ref, ob_ref, pz_ref, halo_ref, mq_ref, mk_ref, mv_ref, x_ref, wout_ref, pw_ref,
               ps_ref, fg_ref, y_ref, s0, s1, s2, s3, *, tm, final):
    i = pl.program_id(1)
    u = pz_ref[:, 0:256]
    s0[0:HALO, :] = jnp.where(i > 0, halo_ref[...], 0.0)
    s0[HALO:HALO + tm, :] = u
    n = tm + HALO
    s1[8:n, :] = s0[8:n, :] + s0[7:n - 1, :]
    s2[16:n, :] = s1[16:n, :] + s1[14:n - 2, :]
    s3[24:n, :] = s2[24:n, :] + s2[20:n - 4, :]
    a16 = s3[HALO:n, :] + s3[HALO - 8:n - 8, :]
    grp, wl = _pool_window_lane((tm, 256))
    tsum = jnp.where(grp == 0, s1[HALO:n, :], jnp.where(grp == 1, s2[HALO:n, :],
                     jnp.where(grp == 2, s3[HALO:n, :], a16)))
    pos = i * tm + _iota((tm, 256), 0)
    cnt = jnp.minimum(pos + 1, wl).astype(F32)
    o_pool = _mm((tsum / cnt - u).astype(BF16), pw_ref[...]) * ps_ref[...]
    o_mem = _mem_attend(mq_ref, mk_ref, mv_ref, tm)
    parts = [oa_ref[:, 0:LANES], oa_ref[:, LANES:256], ob_ref[:, 0:LANES], ob_ref[:, LANES:256],
             o_pool[:, 0:LANES], o_pool[:, LANES:256], o_mem[0], o_mem[1]]
    z_of = lambda c: pz_ref[:, 256 + c * LANES:256 + (c + 1) * LANES]
    y_ref[...] = _mix_out(x_ref[...], parts, z_of, wout_ref, fg_ref, final)


def _post_prompt(oa, ob, pz3, pb3, memb, x3, wout, pw_bd, ps, fg, *, tm, final):
    b, t, d = x3.shape
    nt = t // tm
    row = lambda bi, i: (bi, i, 0)
    const = lambda bi, i: (0, 0)
    return pl.pallas_call(
        functools.partial(_post_body, tm=tm, final=final),
        grid=(b, nt),
        in_specs=[pl.BlockSpec((None, tm, 256), row), pl.BlockSpec((None, tm, 256), row),
                  pl.BlockSpec((None, tm, 1280), row),
                  pl.BlockSpec((None, HALO, 256),
                               lambda bi, i: (bi, jnp.maximum(i * (tm // HALO) - 1, 0), 0)),
                  pl.BlockSpec((None, tm, 256), lambda bi, i: (bi, i, C_MQ // 256)),
                  pl.BlockSpec((None, 256, 256), lambda bi, i: (bi, 0, 0)),
                  pl.BlockSpec((None, 256, 256), lambda bi, i: (bi, 0, 1)),
                  pl.BlockSpec((None, tm, d), row),
                  pl.BlockSpec((d, d), const), pl.BlockSpec((256, 256), const),
                  pl.BlockSpec((1, 256), const), pl.BlockSpec((1, d), const)],
        out_specs=pl.BlockSpec((None, tm, d), row),
        out_shape=jax.ShapeDtypeStruct((b, t, d), F32),
        scratch_shapes=[pltpu.VMEM((tm + HALO, 256), F32)] * 4,
        compiler_params=pltpu.CompilerParams(dimension_semantics=("parallel", "arbitrary"),
                                             vmem_limit_bytes=VMEM_LIMIT),
        name="post_prompt",
    )(oa, ob, pz3, pz3, pb3, memb, memb, x3, wout, pw_bd, ps, fg)


CMP_PITCH = 72
DQ = 4


def _cmp_sample_body(pt_ref, nsa_hbm, pe_ref, w1_ref, w2a_ref, w2b_ref, o_ref, xbuf, acc0, acc1, sem,
                     *, pages, layer):
    j = pl.program_id(0)
    i = pl.program_id(1)
    ni = pl.num_programs(1)
    s = j * ni + i
    m = pages * G_NSA

    def chunk_copy(jj, ii, slot, p, g):
        page = pt_ref[ii * pages + p]
        return pltpu.make_async_copy(
            nsa_hbm.at[layer, page, pl.ds(jj * LANES + g * HEAD_DIM, HEAD_DIM), :],
            xbuf.at[slot, pl.ds((g * pages + p) * CMP_PITCH, HEAD_DIM), :], sem.at[slot])

    def for_chunks(jj, ii, slot, fn):
        def body(p, carry):
            for g in range(G_NSA):
                fn(chunk_copy(jj, ii, slot, p, g))
            return carry
        lax.fori_loop(0, pages, body, 0)

    @pl.when(s == 0)
    def _():
        for_chunks(0, 0, 0, lambda c: c.start())

    @pl.when(s + 1 < 2 * ni)
    def _():
        wrap = i + 1 == ni
        for_chunks(jnp.where(wrap, j + 1, j), jnp.where(wrap, 0, i + 1), (s + 1) % 2,
                   lambda c: c.start())

    slot = s % 2
    for_chunks(j, i, slot, lambda c: c.wait())

    lo = _iota((m, LANES), 1) < HEAD_DIM
    acc0[...] = jnp.zeros(acc0.shape, F32)
    acc1[...] = jnp.zeros(acc1.shape, F32)
    for dq in range(HEAD_DIM // DQ):
        xs = [xbuf[slot, pl.ds(DQ * dq + k, m, stride=CMP_PITCH), :] + pe_ref[DQ * dq + k:DQ * dq + k + 1, :]
              for k in range(DQ)]
        xr = [pltpu.roll(x, HEAD_DIM, 1) for x in xs]
        a0 = jnp.concatenate([jnp.where(lo, xs[0], xr[1]), jnp.where(lo, xs[2], xr[3])], axis=1)
        a1 = jnp.concatenate([jnp.where(lo, xr[0], xs[1]), jnp.where(lo, xr[2], xs[3])], axis=1)
        w = w1_ref[dq]
        acc0[...] += _mm(a0.astype(BF16), w)
        acc1[...] += _mm(a1.astype(BF16), w)
    for blk, acc in enumerate((acc0, acc1)):
        h = acc[...]
        h = (h * _sigmoid(h)).astype(BF16)
        o_ref[pl.ds(blk, pages, stride=PAGE_SIZE // NSA_BLOCK), :] = (
            _mm(h[0:pages], w2a_ref[...]) + _mm(h[pages:m], w2b_ref[...]))


def _compress_sample(page_flat, nsa_t, pe_t, w1q, w2a, w2b, *, pages, layer):
    npages = page_flat.shape[0]
    nsteps = npages // pages
    m = pages * G_NSA
    by_j = lambda blk: pl.BlockSpec((None,) + blk, lambda j, i, pt: (j,) + (0,) * len(blk))
    grid_spec = pltpu.PrefetchScalarGridSpec(
        num_scalar_prefetch=1, grid=(2, nsteps),
        in_specs=[pl.BlockSpec(memory_space=pl.ANY), by_j((HEAD_DIM, LANES)),
                  by_j((HEAD_DIM // DQ, 256, 256)), by_j((256, LANES)), by_j((256, LANES))],
        out_specs=pl.BlockSpec((None, m, LANES), lambda j, i, pt: (j, i, 0)),
        scratch_shapes=[pltpu.VMEM((2, m * CMP_PITCH, LANES), F32), pltpu.VMEM((m, 256), F32),
                        pltpu.VMEM((m, 256), F32), pltpu.SemaphoreType.DMA((2,))])
    return pl.pallas_call(
        functools.partial(_cmp_sample_body, pages=pages, layer=layer),
        grid_spec=grid_spec,
        out_shape=jax.ShapeDtypeStruct((2, npages * G_NSA, LANES), F32),
        compiler_params=pltpu.CompilerParams(dimension_semantics=("arbitrary", "arbitrary"),
                                             vmem_limit_bytes=VMEM_LIMIT),
        name="cmp_sample",
    )(page_flat, nsa_t, pe_t, w1q, w2a, w2b)


HROWS = 16


def _rows_from_lanes(vec_row, pick):
    return jnp.sum(jnp.where(pick, vec_row, 0.0), axis=1, keepdims=True)


def _bf16r(x):
    return x.astype(BF16).astype(F32)


def _att_sample_body(pt_ref, fox_hbm, lf_hbm, nsa_hbm, pb_ref, sf_ref, aux_ref, win_ref, mem_ref,
                     kc_ref, vc_ref, exp_ref, *rest, npg, past, layer):
    o_ref, wout_ref, fbuf, lbuf, sbuf, sem = rest[-6:]
    b = pl.program_id(0)
    nb = pl.num_programs(0)

    def copies(seq, slot):
        out = []
        for i in range(npg):
            page = pt_ref[seq * npg + i]
            out.append(pltpu.make_async_copy(
                fox_hbm.at[layer, page], fbuf.at[slot, i], sem.at[0, slot]))
            out.append(pltpu.make_async_copy(
                lf_hbm.at[layer, page], lbuf.at[slot, pl.ds(i * 8, 8), :], sem.at[1, slot]))
            out.append(pltpu.make_async_copy(
                nsa_hbm.at[layer, page, pl.ds(256, 256), :], sbuf.at[slot, i], sem.at[2, slot]))
        return out

    def slab(buf, r0, r1):
        return jnp.concatenate([buf[slot, i, r0:r1, :].astype(BF16) for i in range(npg)], axis=1)

    @pl.when(b == 0)
    def _():
        for c in copies(0, 0):
            c.start()

    @pl.when(b + 1 < nb)
    def _():
        for c in copies(b + 1, (b + 1) % 2):
            c.start()

    slot = b % 2
    for c in copies(b, slot):
        c.wait()

    row8 = _iota((HROWS, LANES), 0)
    lane8 = _iota((HROWS, LANES), 1)
    aux = aux_ref[...]
    pbr = pb_ref[...].astype(F32)
    sfr = sf_ref[...]

    def head_rows(q256):
        r = _iota((HROWS, 256), 0)
        l = _iota((HROWS, 256), 1)
        return jnp.where(_div_pow2(l, HEAD_DIM) == r, q256, 0.0).astype(BF16)

    def pick_heads(o8):
        r = _iota((HROWS, 256), 0)
        l = _iota((HROWS, 256), 1)
        return jnp.sum(jnp.where(_div_pow2(l, HEAD_DIM) == r, o8, 0.0), axis=0, keepdims=True)

    kt = slab(fbuf, 0, 256)
    vt = slab(fbuf, 256, 512)
    qf = head_rows(pbr[:, C_FQ:C_FK])
    s = _mm(qf, kt)
    lf2 = lbuf[slot]
    r2 = _iota((npg * 8, npg * 8), 0)
    c2 = _iota((npg * 8, npg * 8), 1)
    upper = (_iota((LANES, LANES), 0) > _iota((LANES, LANES), 1)).astype(BF16)
    within = _exact_mm_rhs(lf2, upper)
    tot = jnp.sum(lf2, axis=1, keepdims=True)
    later_pages = ((_mod_pow2(c2, 8) == _mod_pow2(r2, 8))
                   & (_div_pow2(c2, 8) > _div_pow2(r2, 8))).astype(BF16)
    later = _exact_mm(later_pages, jnp.broadcast_to(tot, (npg * 8, LANES)))
    rr = _iota((npg * 8, LANES), 0)
    ll = _iota((npg * 8, LANES), 1)
    lf_new = _rows_from_lanes(aux, ll == _mod_pow2(rr, 8))
    bias = within + later + lf_new
    zpad = jnp.zeros((HROWS - 8, LANES), F32)
    s = jnp.concatenate(
        [s[:, i * PAGE_SIZE:(i + 1) * PAGE_SIZE] + jnp.concatenate([bias[i * 8:(i + 1) * 8, :], zpad], axis=0)
         for i in range(npg)], axis=1)
    knew = _bf16r(sfr[:, 0:256])
    vnew = _bf16r(sfr[:, 256:512])
    s_new = jnp.sum(qf.astype(F32) * knew, axis=1, keepdims=True)
    m = jnp.maximum(jnp.max(s, axis=1, keepdims=True), s_new)
    e = jnp.exp(s - m)
    e_new = jnp.exp(s_new - m)
    den = jnp.sum(e, axis=1, keepdims=True) + e_new
    o8 = (_nt(e.astype(BF16), vt) + _bf16r(e_new) * vnew) / den
    o_ref[:, 0:256] = pick_heads(o8)

    nq = pbr[:, C_NQ:C_NKV]
    t0 = jnp.broadcast_to(nq[:, 0:LANES], (HROWS, LANES))
    t1 = jnp.broadcast_to(nq[:, LANES:256], (HROWS, LANES))
    t0r, t1r = pltpu.roll(t0, 64, 1), pltpu.roll(t1, 64, 1)
    lo = lane8 < 64
    hrows = [jnp.where(lo, t0, 0.0), jnp.where(lo, t0r, 0.0),
             jnp.where(lo, 0.0, t1r), jnp.where(lo, 0.0, t1)]
    qn = jnp.zeros((HROWS, LANES), F32)
    for h in range(H_NSA):
        qn = jnp.where(row8 == h, hrows[h], qn)
    qn_b = qn.astype(BF16)
    slope8 = jnp.zeros((HROWS, 1), F32)
    r81 = _iota((HROWS, 1), 0)
    for h in range(H_NSA):
        slope8 = jnp.where(r81 == h, ALIBI_SLOPES[h], slope8)
    nkv_new = sfr[:, 512:1280]
    nblk = past // NSA_BLOCK

    def new_score(k128):
        return jnp.sum(qn_b.astype(F32) * _bf16r(k128), axis=1, keepdims=True)

    sc = _nt(qn_b, kc_ref[...].astype(BF16))
    cend = _iota((HROWS, nblk), 1) * NSA_BLOCK + (NSA_BLOCK - 1)
    sc = sc - slope8 * (past - cend).astype(F32)
    ec = jnp.exp(sc - jnp.max(sc, axis=1, keepdims=True))
    pc = ec / jnp.maximum(jnp.sum(ec, axis=1, keepdims=True), 1e-30)
    o_c = _mm(pc.astype(BF16), vc_ref[...].astype(BF16))
    imp = jnp.concatenate([pc[2 * g:2 * g + 1] + pc[2 * g + 1:2 * g + 2] for g in range(G_NSA)]
                          + [jnp.zeros((HROWS - G_NSA, nblk), F32)], axis=0)
    if nblk < LANES:
        imp = jnp.concatenate([imp, jnp.zeros((HROWS, LANES - nblk), F32)], axis=1)
    forced = (lane8 == 0) | (lane8 == nblk) | (lane8 == nblk - 1)
    score = jnp.where(lane8 > nblk, -1.0, jnp.where(forced, FORCE_SCORE, imp))
    rank = jnp.zeros((HROWS, LANES), F32)
    for mblk in range(nblk + 1):
        r = score[:, mblk:mblk + 1]
        rank = rank + jnp.where((r > score) | ((r == score) & (lane8 > mblk)), 1.0, 0.0)
    sel_g = jnp.where((rank < min(NSA_TOPN, nblk + 1)) & (score >= 0.0), 1.0, 0.0)
    sel_h = jnp.zeros((HROWS, LANES), F32)
    for h in range(H_NSA):
        sel_h = jnp.where(row8 == h, sel_g[h // 2:h // 2 + 1, :], sel_h)
    valid = _mm(sel_h.astype(BF16), exp_ref[...]) > 0.5
    kst = slab(sbuf, 0, LANES)
    vst = slab(sbuf, LANES, 256)
    ss = _mm(qn_b, kst) - slope8 * (past - _iota((HROWS, past), 1)).astype(F32)
    ss = jnp.where(valid, ss, NEG_INF)
    ss_new = new_score(nkv_new[:, 256:384])
    ms = jnp.maximum(jnp.max(ss, axis=1, keepdims=True), ss_new)
    es = jnp.where(valid, jnp.exp(ss - ms), 0.0)
    es_new = jnp.exp(ss_new - ms)
    o_s = (_nt(es.astype(BF16), vst) + _bf16r(es_new) * _bf16r(nkv_new[:, 384:512])) \
        / (jnp.sum(es, axis=1, keepdims=True) + es_new)
    wl_ = win_ref.shape[1]
    win = win_ref[...]
    sw = _mm(qn_b, win[0:LANES].astype(BF16)) - slope8 * (wl_ - _iota((HROWS, wl_), 1)).astype(F32)
    sw_new = new_score(nkv_new[:, 512:640])
    mw = jnp.maximum(jnp.max(sw, axis=1, keepdims=True), sw_new)
    ew = jnp.exp(sw - mw)
    ew_new = jnp.exp(sw_new - mw)
    o_w = (_nt(ew.astype(BF16), win[LANES:256].astype(BF16))
           + _bf16r(ew_new) * _bf16r(nkv_new[:, 640:768])) / (jnp.sum(ew, axis=1, keepdims=True) + ew_new)
    rw = _iota((256, 256), 0)
    lw_ = _iota((256, 256), 1)
    new_col = jnp.sum(jnp.where(rw == lw_, nkv_new[:, 512:768], 0.0), axis=1, keepdims=True)
    _store_layer(wout_ref, jnp.where(_iota((256, wl_), 1) == wl_ - 1, new_col, pltpu.roll(win, wl_ - 1, 1)),
                 layer == 0)
    gates = [_rows_from_lanes(aux, lane8 == LANE_GATE0 + 3 * row8 + c) for c in range(3)]
    o_n = gates[0] * o_c + gates[1] * o_s + gates[2] * o_w
    o_nr = pltpu.roll(o_n, 64, 1)
    lo1 = _iota((1, LANES), 1) < 64
    o_ref[:, 256:384] = jnp.where(lo1, o_n[0:1], o_nr[1:2])
    o_ref[:, 384:512] = jnp.where(lo1, o_nr[2:3], o_n[3:4])

    qm = head_rows(pbr[:, C_MQ:C_PU])
    sm = _mm(qm, mem_ref[0:256, :].astype(BF16))
    em = jnp.exp(sm - jnp.max(sm, axis=1, keepdims=True))
    pm = em / jnp.sum(em, axis=1, keepdims=True)
    o_ref[:, 512:768] = pick_heads(_nt(pm.astype(BF16), mem_ref[256:512, :].astype(BF16)))


def _exact_mm_rhs(x, m01):
    p1, p2, p3 = _split3(x)
    return _mm(p1, m01) + _mm(p2, m01) + _mm(p3, m01)


def _attend_sample(page_flat, fox_t, lf_t, nsa_t, pb3, sf3, aux3, win_t, mem_t, kcmp, vcmp, expand,
                   *, npg, layer, stacked=()):
    depth = win_t.shape[0]
    db = pb3.shape[0]
    past = npg * PAGE_SIZE
    assert len(stacked) == (1 if layer else 0)
    per_seq = lambda blk: pl.BlockSpec((None,) + blk, lambda b, pt: (b, 0, 0))
    per_seq_layer = lambda blk: pl.BlockSpec((None, None) + blk, lambda b, pt: (layer, b, 0, 0))
    grid_spec = pltpu.PrefetchScalarGridSpec(
        num_scalar_prefetch=1, grid=(db,),
        in_specs=[pl.BlockSpec(memory_space=pl.ANY), pl.BlockSpec(memory_space=pl.ANY),
                  pl.BlockSpec(memory_space=pl.ANY),
                  per_seq((1, N_PB)), per_seq((1, 1280)), per_seq((1, LANES)),
                  per_seq_layer(win_t.shape[2:]), per_seq_layer(mem_t.shape[2:]),
                  per_seq(kcmp.shape[1:]), per_seq(vcmp.shape[1:]),
                  pl.BlockSpec(expand.shape, lambda b, pt: (0, 0))]
        + [pl.BlockSpec(memory_space=pl.ANY)] * len(stacked),
        out_specs=[per_seq((1, 768)),
                   _stacked_spec(depth, layer, (None,) + win_t.shape[2:], lambda b, pt: (b, 0, 0))],
        scratch_shapes=[pltpu.VMEM((2, npg, 512, PAGE_SIZE), F32), pltpu.VMEM((2, npg * 8, LANES), F32),
                        pltpu.VMEM((2, npg, 256, PAGE_SIZE), F32), pltpu.SemaphoreType.DMA((3, 2))])
    return pl.pallas_call(
        functools.partial(_att_sample_body, npg=npg, past=past, layer=layer),
        grid_spec=grid_spec,
        out_shape=[jax.ShapeDtypeStruct((db, 1, 768), F32), jax.ShapeDtypeStruct(win_t.shape, F32)],
        input_output_aliases={12 + k: 1 + k for k in range(len(stacked))},
        compiler_params=pltpu.CompilerParams(dimension_semantics=("arbitrary",),
                                             vmem_limit_bytes=VMEM_LIMIT),
        name="att_sample",
    )(page_flat, fox_t, lf_t, nsa_t, pb3, sf3, aux3, win_t, mem_t, kcmp, vcmp, expand, *stacked)


def _post_sample_body(oatt_ref, pz_ref, sp_ref, x_ref, wout_ref, pw_ref, ps_ref, fg_ref, y_ref,
                      spo_ref, *, rows, past, final):
    u = pz_ref[:, 0:256]
    grp, wl = _pool_window_lane((rows, 256))
    tsum = u
    for j in range(1, POOL_BUF + 1):
        tsum = tsum + jnp.where(wl > j, sp_ref[POOL_BUF - j], 0.0)
    for j in range(POOL_BUF - 1):
        spo_ref[j] = sp_ref[j + 1]
    spo_ref[POOL_BUF - 1] = u
    cnt = jnp.minimum(past + 1, wl).astype(F32)
    o_pool = _mm((tsum / cnt - u).astype(BF16), pw_ref[...]) * ps_ref[...]
    parts = [oatt_ref[:, c * LANES:(c + 1) * LANES] for c in range(4)]
    parts += [o_pool[:, 0:LANES], o_pool[:, LANES:256]]
    parts += [oatt_ref[:, 512:640], oatt_ref[:, 640:768]]
    z_of = lambda c: pz_ref[:, 256 + c * LANES:256 + (c + 1) * LANES]
    y_ref[...] = _mix_out(x_ref[...], parts, z_of, wout_ref, fg_ref, final)


def _post_sample(oatt, pz, pool_t, x2d, wout, pw_bd, ps, fg, *, past, final, layer):
    rows, d = x2d.shape
    full = lambda a: pl.BlockSpec(a.shape, lambda i: (0,) * a.ndim)
    return pl.pallas_call(
        functools.partial(_post_sample_body, rows=rows, past=past, final=final),
        grid=(1,),
        in_specs=[full(oatt), full(pz),
                  pl.BlockSpec((None, POOL_BUF, rows, 256), lambda i: (layer, 0, 0, 0)),
                  full(x2d), full(wout), full(pw_bd), full(ps), full(fg)],
        out_specs=[pl.BlockSpec((rows, d), lambda i: (0, 0)),
                   pl.BlockSpec((POOL_BUF, rows, 256), lambda i: (0, 0, 0))],
        out_shape=[jax.ShapeDtypeStruct((rows, d), F32),
                   jax.ShapeDtypeStruct((POOL_BUF, rows, 256), F32)],
        compiler_params=pltpu.CompilerParams(vmem_limit_bytes=VMEM_LIMIT),
        name="post_sample",
    )(oatt, pz, pool_t, x2d, wout, pw_bd, ps, fg)


def _prep_layer_weights(norm_g, w_in, b_fgt, nsa_pe, nsa_w1, nsa_w2, pool_w, pool_scale,
                        w_mem_kv, w_out):
    d = w_in.shape[0]
    o = 0
    segs = {}
    for name, width in (("fqkv", 768), ("ff", H_FOX), ("nq", 256), ("nkv", 768),
                        ("ng", 3 * H_NSA), ("pu", 256), ("mq", 256), ("z", d)):
        segs[name] = w_in[:, o:o + width]
        o += width
    pad = jnp.zeros((d, LANES - H_FOX - 3 * H_NSA), w_in.dtype)
    wp = jnp.concatenate([segs["fqkv"], segs["nq"], segs["nkv"], segs["mq"], segs["pu"],
                          segs["z"], segs["ff"], segs["ng"], pad], axis=1).astype(BF16)
    bvec = jnp.zeros((1, LANES), F32).at[0, 0:H_FOX].set(b_fgt.astype(F32))
    gw = D_GROUP // len(POOL_WINDOWS)
    pw_bd = jnp.zeros((D_GROUP, D_GROUP), F32)
    for g in range(len(POOL_WINDOWS)):
        pw_bd = pw_bd.at[g * gw:(g + 1) * gw, g * gw:(g + 1) * gw].set(pool_w[g])
    return dict(g=norm_g.reshape(1, d), wp=wp, bvec=bvec,
                pe2=jnp.concatenate([nsa_pe, nsa_pe], axis=2).astype(F32),
                w1dup=jnp.concatenate([nsa_w1.reshape(2, NSA_BLOCK, HEAD_DIM, -1)] * 2, axis=2).astype(BF16),
                w2=nsa_w2.astype(BF16),
                w1q=jnp.transpose(nsa_w1.reshape(2, NSA_BLOCK, HEAD_DIM, -1), (0, 2, 1, 3)).reshape(
                    2, HEAD_DIM // DQ, DQ * NSA_BLOCK, -1).astype(BF16),
                pe_t=jnp.concatenate([jnp.transpose(nsa_pe, (0, 2, 1))] * 2, axis=2).astype(F32),
                w2a=jnp.pad(nsa_w2, ((0, 0), (0, 0), (0, HEAD_DIM))).astype(BF16),
                w2b=jnp.pad(nsa_w2, ((0, 0), (0, 0), (HEAD_DIM, 0))).astype(BF16),
                pw_bd=pw_bd.astype(BF16), ps=pool_scale.reshape(1, D_GROUP).astype(F32),
                wmkv=w_mem_kv.astype(BF16), wout=w_out.astype(BF16))


def _layer_prompt(x, mem, lw, fg, tile, layer, depth, stacked):
    b, t, d = x.shape
    n = b * t
    nb = t // NSA_BLOCK
    final = layer == depth - 1
    pb, pz, aux, kvc, ct, lft, fkvt, nkvt, nwt = _project(
        x.reshape(n, d), lw["g"], lw["wp"], lw["bvec"], tm=tile, tiles_per_seq=t // tile, prompt=True,
        layer=layer, depth=depth, stacked=stacked)
    pb3 = pb.reshape(b, t, N_PB)
    o_a = _fox_prompt(pb3, ct, tq=tile)
    cmp = _compress_prompt(kvc, lw["pe2"], lw["w1dup"], lw["w2"]).reshape(2, G_NSA, b, nb, HEAD_DIM)
    cmp = jnp.transpose(cmp, (0, 2, 3, 1, 4)).reshape(2, b, nb, LANES)
    cmp = jnp.pad(cmp, ((0, 0), (0, 0), (0, NBLK_PAD - nb), (0, 0))).astype(BF16)
    o_b = _nsa_prompt(pb3, cmp[0], cmp[1], aux.reshape(b, t, LANES), tq=tile, nb=nb)
    n_mem = mem.shape[1]
    mem_t, memb = _mem_project(mem, lw["wmkv"])
    pz3 = pz.reshape(b, t, 1280)
    y = _post_prompt(o_a, o_b, pz3, pb3, memb, x, lw["wout"], lw["pw_bd"], lw["ps"], fg,
                     tm=tile, final=final)
    w_keep = min(NSA_WINDOW, t)
    state = (jnp.transpose(lft[:, 0:H_FOX, :], (0, 2, 1)),
             jnp.transpose(nwt[:, :, t - w_keep:].reshape(b, 2, G_NSA, HEAD_DIM, w_keep), POS_FIRST),
             pz3[:, t - POOL_BUF:, 0:256],
             jnp.transpose(mem_t.reshape(b, 2, H_MEM, HEAD_DIM, n_mem), POS_FIRST))
    return y, state, (fkvt, nkvt)


POS_FIRST = (0, 4, 1, 2, 3)


def _native_views(cache_fox_kv, cache_fox_lf, cache_nsa_kv, state_nsa_win, state_pool, cache_mem_kv):
    dp, n_phys = cache_fox_kv.shape[:2]
    db = state_nsa_win.shape[1]
    to_t = (0, 1, 3, 4, 5, 2)
    fox_t = jnp.transpose(cache_fox_kv, to_t).reshape(dp, n_phys, 512, PAGE_SIZE)
    lf_t = jnp.pad(jnp.transpose(cache_fox_lf.astype(F32), (0, 1, 3, 2)),
                   ((0, 0), (0, 0), (0, 8 - H_FOX), (0, 0)))
    nsa_t = jnp.transpose(cache_nsa_kv, to_t).reshape(dp, n_phys, 512, PAGE_SIZE)
    win_t = jnp.transpose(state_nsa_win, to_t).reshape(dp, db, 256, state_nsa_win.shape[2])
    mem_t = jnp.transpose(cache_mem_kv, to_t).reshape(dp, db, 512, cache_mem_kv.shape[2])
    pool_t = jnp.transpose(state_pool, (0, 2, 1, 3))
    return dict(fox_t=fox_t, lf_t=lf_t, nsa_t=nsa_t, win_t=win_t, mem_t=mem_t, pool_t=pool_t)


def _layer_sample(x, nv, page_table, lw, fg, layer, depth, stacked):
    db, ds, d = x.shape
    npg = page_table.shape[1]
    past = npg * PAGE_SIZE
    nblk = past // NSA_BLOCK
    final = layer == depth - 1
    pb, pz, aux, sf, sft, auxt = _project(x.reshape(db, d), lw["g"], lw["wp"], lw["bvec"],
                                          tm=db, tiles_per_seq=1, prompt=False)
    page_flat = page_table.reshape(db * npg).astype(I32)
    cmp = _compress_sample(page_flat, nv["nsa_t"], lw["pe_t"], lw["w1q"], lw["w2a"], lw["w2b"],
                           pages=min(db * npg, 128), layer=layer)
    cmp = cmp.reshape(2, db, nblk, LANES)
    kcmp, vcmp = cmp[0], cmp[1]
    expand = (np.arange(LANES)[:, None] == (np.arange(past)[None, :] // NSA_BLOCK)).astype(np.float32)
    oatt, win_new = _attend_sample(
        page_flat, nv["fox_t"], nv["lf_t"], nv["nsa_t"], pb.reshape(db, 1, N_PB),
        sf.reshape(db, 1, 1280), aux.reshape(db, 1, LANES), nv["win_t"], nv["mem_t"], kcmp, vcmp,
        jnp.asarray(expand, BF16), npg=npg, layer=layer, stacked=stacked)
    y, pool_new = _post_sample(oatt.reshape(db, 768), pz, nv["pool_t"], x.reshape(db, d), lw["wout"],
                               lw["pw_bd"], lw["ps"], fg, past=past, final=final, layer=layer)
    seq_first = (3, 0, 1, 2)
    state = (jnp.transpose(sft[0:512].reshape(2, H_FOX, HEAD_DIM, db), seq_first)[:, None],
             jnp.transpose(auxt[0:H_FOX])[:, None, :],
             jnp.transpose(sft[512:1024].reshape(4, G_NSA, HEAD_DIM, db), seq_first)[:, None],
             jnp.transpose(pool_new, (1, 0, 2)))
    return y.reshape(db, ds, d), state, (win_new,)


def _from_t(x_t, a, b):
    dp, bt, _, pos = x_t.shape
    return jnp.transpose(x_t.reshape(dp, bt, a, b, HEAD_DIM, pos), (0, 1, 5, 2, 3, 4))


def kernel(x_prompt, x_sample, cache_fox_kv, cache_fox_lf, cache_nsa_kv, state_nsa_win, state_pool,
           cache_mem_kv, page_table, mem_prompt, norm_g, w_in, b_fgt, nsa_pe, nsa_w1, nsa_w2, pool_w,
           pool_scale, w_mem_kv, w_out, final_g):
    depth = norm_g.shape[0]
    t = x_prompt.shape[1]
    assert x_sample.shape[1] == 1 and t % NSA_WINDOW == 0 and t // NSA_BLOCK <= NBLK_PAD
    hp, hs = x_prompt, x_sample
    fg = final_g.reshape(1, -1).astype(F32)
    nv = _native_views(cache_fox_kv, cache_fox_lf, cache_nsa_kv, state_nsa_win, state_pool,
                       cache_mem_kv)
    sp = [[] for _ in range(4)]
    ss = [[] for _ in range(4)]
    stk_p, stk_s = (), ()
    for l in range(depth):
        lw = _prep_layer_weights(norm_g[l], w_in[l], b_fgt[l], nsa_pe[l], nsa_w1[l], nsa_w2[l],
                                 pool_w[l], pool_scale[l], w_mem_kv[l], w_out[l])
        hp, st_p, stk_p = _layer_prompt(hp, mem_prompt, lw, fg, NSA_WINDOW, l, depth, stk_p)
        hs, st_s, stk_s = _layer_sample(hs, nv, page_table, lw, fg, l, depth, stk_s)
        for acc, s in zip(sp, st_p):
            acc.append(s)
        for acc, s in zip(ss, st_s):
            acc.append(s)
    fox_lf_p, nsa_win_p, pool_p, mem_kv_p = (jnp.stack(a) for a in sp)
    fox_kv_s, fox_lf_s, nsa_kv_s, pool_s = (jnp.stack(a) for a in ss)
    return (hp, hs, _from_t(stk_p[0], 2, H_FOX), fox_lf_p, _from_t(stk_p[1], 4, G_NSA), nsa_win_p, pool_p,
            mem_kv_p, fox_kv_s, fox_lf_s, nsa_kv_s, _from_t(stk_s[0], 2, G_NSA), pool_s)
```

```python
import functools

import numpy as np
import jax
import jax.numpy as jnp
from jax import lax
from jax.experimental import pallas as pl
from jax.experimental.pallas import tpu as pltpu

F32 = jnp.float32
BF16 = jnp.bfloat16
I32 = jnp.int32

HEAD_DIM = 64
H_FOX = 4
H_NSA = 4
G_NSA = 2
H_MEM = 4
D_GROUP = 256
POOL_WINDOWS = (2, 4, 8, 16)
POOL_BUF = 15
NSA_BLOCK = 64
NSA_TOPN = 16
NSA_WINDOW = 512
PAGE_SIZE = 128
RMS_EPS = 1e-6
NEG_INF = -1e30
FORCE_SCORE = 1e4
SCALE = HEAD_DIM ** -0.5
ALIBI_SLOPES = tuple(2.0 ** (-8.0 * (h + 1) / H_NSA) for h in range(H_NSA))

LANES = 128
NBLK_PAD = 128
VMEM_LIMIT = 56 * 1024 * 1024

C_FQ, C_FK, C_FV, C_NQ, C_NKV, C_MQ, C_PU, C_Z, C_SM, C_END = (
    0, 256, 512, 768, 1024, 1792, 2048, 2304, 3328, 3456)
N_PB = C_PU
LANE_GATE0 = 4


def _nt(a, b):
    return lax.dot_general(a, b, (((1,), (1,)), ((), ())), preferred_element_type=F32)


def _mm(a, b):
    return jnp.dot(a, b, preferred_element_type=F32)


def _split3(x):
    p1 = x.astype(BF16)
    r1 = x - p1.astype(F32)
    p2 = r1.astype(BF16)
    p3 = (r1 - p2.astype(F32)).astype(BF16)
    return p1, p2, p3


def _exact_mm(m01, x):
    p1, p2, p3 = _split3(x)
    return _mm(m01, p1) + _mm(m01, p2) + _mm(m01, p3)


def _sigmoid(x):
    return 1.0 / (1.0 + jnp.exp(-x))


def _log_sigmoid(x):
    return jnp.minimum(x, 0.0) - jnp.log1p(jnp.exp(-jnp.abs(x)))


def _iota(shape, dim):
    return lax.broadcasted_iota(I32, shape, dim)


def _div_pow2(x, n):
    assert n & (n - 1) == 0
    return jnp.right_shift(x, n.bit_length() - 1)


def _mod_pow2(x, n):
    assert n & (n - 1) == 0
    return jnp.bitwise_and(x, n - 1)


def _store_layer(ref, value, first_layer):
    if first_layer:
        ref[0] = value
        for k in range(1, ref.shape[0]):
            ref[k] = jnp.zeros(value.shape, value.dtype)
    else:
        ref[...] = value


def _proj_body(x_ref, g_ref, w_ref, b_ref, *rest, tm, tiles_per_seq, prompt, n_alias):
    pb_ref, pz_ref, aux_ref, *rest = rest[n_alias:]
    x = x_ref[...]
    ms = jnp.mean(x * x, axis=-1, keepdims=True)
    xn = (x * lax.rsqrt(ms + RMS_EPS) * g_ref[...]).astype(BF16)

    def seg(a, b):
        return _mm(xn, w_ref[:, a:b])

    pb_ref[:, C_FQ:C_FK] = (seg(C_FQ, C_FK) * SCALE).astype(BF16)
    fkv = seg(C_FK, C_NQ)
    pb_ref[:, C_FK:C_NQ] = fkv.astype(BF16)
    pb_ref[:, C_NQ:C_NKV] = (seg(C_NQ, C_NKV) * SCALE).astype(BF16)
    nkv = seg(C_NKV, C_MQ)
    pb_ref[:, C_NKV:C_MQ] = nkv.astype(BF16)
    pb_ref[:, C_MQ:C_PU] = (seg(C_MQ, C_PU) * SCALE).astype(BF16)
    pz_ref[...] = seg(C_PU, C_SM)
    small = seg(C_SM, C_END)

    lane = _iota((tm, LANES), 1)
    lf = _log_sigmoid(small + b_ref[...])
    aux = jnp.where(lane < H_FOX, lf, jnp.where(lane < LANE_GATE0 + 3 * H_NSA, _sigmoid(small), 0.0))
    aux_ref[...] = aux
    if prompt:
        kvc_ref, ct_ref, lft_ref, fkvt_ref, nkvt_ref, nwt_ref, carry_ref = rest
        kvc_ref[0] = nkv[:, 0:LANES]
        kvc_ref[1] = nkv[:, LANES:256]
        _store_layer(fkvt_ref, fkv.T, n_alias == 0)
        _store_layer(nkvt_ref, nkv[:, 0:512].T, n_alias == 0)
        nwt_ref[...] = nkv[:, 512:768].T
        lfm = jnp.where(lane < H_FOX, lf, 0.0)
        lft_ref[...] = lfm.T[0:8, :]
        tri = (_iota((LANES, LANES), 0) >= _iota((LANES, LANES), 1)).astype(BF16)
        first = (pl.program_id(0) % tiles_per_seq) == 0
        carry = jnp.where(first, 0.0, carry_ref[0:1, :])
        for r in range(tm // LANES):
            blk = lfm[r * LANES:(r + 1) * LANES]
            cblk = _exact_mm(tri, blk) + carry
            carry = cblk[LANES - 1:LANES, :]
            ct_ref[:, r * LANES:(r + 1) * LANES] = cblk.T[0:8, :]
        carry_ref[0:1, :] = carry
    else:
        sf_ref, sft_ref, auxt_ref = rest
        sf_ref[:, 0:512] = fkv
        sf_ref[:, 512:1280] = nkv
        sft_ref[0:512, :] = fkv.T
        sft_ref[512:1280, :] = nkv.T
        auxt_ref[...] = aux.T


def _stacked_spec(depth, layer, blk, index):
    if layer == 0:
        return pl.BlockSpec((depth,) + blk, lambda *a: (0,) + index(*a))
    return pl.BlockSpec((None,) + blk, lambda *a: (layer,) + index(*a))


def _project(x2d, g, wp, bvec, *, tm, tiles_per_seq, prompt, layer=0, depth=1, stacked=()):
    n, d = x2d.shape
    grid = (n // tm,)
    row = lambda i: (i, 0)
    const = lambda i: (0, 0)
    out_shape = [jax.ShapeDtypeStruct((n, N_PB), BF16), jax.ShapeDtypeStruct((n, 1280), F32),
                 jax.ShapeDtypeStruct((n, LANES), F32)]
    out_specs = [pl.BlockSpec((tm, N_PB), row), pl.BlockSpec((tm, 1280), row),
                 pl.BlockSpec((tm, LANES), row)]
    scratch = []
    aliases = {}
    if prompt:
        assert len(stacked) == (2 if layer else 0)
        nseq = n // (tm * tiles_per_seq)
        t = tm * tiles_per_seq
        seq_idx = lambda i: (i // tiles_per_seq, 0, i % tiles_per_seq)
        seq_t = lambda rows: pl.BlockSpec((None, rows, tm), seq_idx)
        lay_t = lambda rows: _stacked_spec(depth, layer, (None, rows, tm), seq_idx)
        out_shape += [jax.ShapeDtypeStruct((2, n, LANES), F32),
                      jax.ShapeDtypeStruct((nseq, 8, t), F32), jax.ShapeDtypeStruct((nseq, 8, t), F32),
                      jax.ShapeDtypeStruct((depth, nseq, 512, t), F32),
                      jax.ShapeDtypeStruct((depth, nseq, 512, t), F32),
                      jax.ShapeDtypeStruct((nseq, 256, t), F32)]
        out_specs += [pl.BlockSpec((2, tm, LANES), lambda i: (0, i, 0)),
                      seq_t(8), seq_t(8), lay_t(512), lay_t(512), seq_t(256)]
        scratch = [pltpu.VMEM((8, LANES), F32)]
        aliases = {4 + k: 6 + k for k in range(len(stacked))}
    else:
        assert grid == (1,) and not stacked
        out_shape += [jax.ShapeDtypeStruct((n, 1280), F32), jax.ShapeDtypeStruct((1280, n), F32),
                      jax.ShapeDtypeStruct((LANES, n), F32)]
        out_specs += [pl.BlockSpec((tm, 1280), row), pl.BlockSpec((1280, tm), const),
                      pl.BlockSpec((LANES, tm), const)]
    return pl.pallas_call(
        functools.partial(_proj_body, tm=tm, tiles_per_seq=tiles_per_seq, prompt=prompt,
                          n_alias=len(stacked)),
        grid=grid,
        in_specs=[pl.BlockSpec((tm, d), row), pl.BlockSpec((1, d), const),
                  pl.BlockSpec((d, C_END), const), pl.BlockSpec((1, LANES), const)]
        + [pl.BlockSpec(memory_space=pl.ANY)] * len(stacked),
        out_specs=out_specs, out_shape=out_shape, scratch_shapes=scratch,
        input_output_aliases=aliases,
        compiler_params=pltpu.CompilerParams(dimension_semantics=("arbitrary",),
                                             vmem_limit_bytes=VMEM_LIMIT),
        name="proj",
    )(x2d, g, wp, bvec, *stacked)


def _memkv_body(m_ref, w_ref, ot_ref, ob_ref):
    r = _mm(m_ref[...].astype(BF16), w_ref[...])
    ot_ref[...] = r.T
    ob_ref[...] = r.astype(BF16)


def _mem_project(mem3, w):
    b, n_mem, d = mem3.shape
    e = w.shape[1]
    return pl.pallas_call(
        _memkv_body, grid=(b,),
        in_specs=[pl.BlockSpec((None, n_mem, d), lambda i: (i, 0, 0)),
                  pl.BlockSpec((d, e), lambda i: (0, 0))],
        out_specs=[pl.BlockSpec((None, e, n_mem), lambda i: (i, 0, 0)),
                   pl.BlockSpec((None, n_mem, e), lambda i: (i, 0, 0))],
        out_shape=[jax.ShapeDtypeStruct((b, e, n_mem), F32), jax.ShapeDtypeStruct((b, n_mem, e), BF16)],
        name="memkv",
    )(mem3, w)


def _fox_body(qt_ref, kt_ref, q_ref, k_ref, v_ref, ct_ref, o_ref, m_ref, l_ref, acc_ref, *, tq, tk):
    qi = qt_ref[pl.program_id(1)]
    ki = kt_ref[pl.program_id(1)]

    @pl.when(ki == 0)
    def _():
        m_ref[...] = jnp.full(m_ref.shape, NEG_INF, F32)
        l_ref[...] = jnp.zeros(l_ref.shape, F32)
        acc_ref[...] = jnp.zeros(acc_ref.shape, F32)

    lane = _iota((tq, LANES), 1)

    def step(masked):
        if masked:
            causal = _iota((tq, tk), 0) >= _iota((tq, tk), 1)
        scores = []
        for h in range(H_FOX):
            pr, hh = divmod(h, 2)
            cs = slice(pr * LANES, (pr + 1) * LANES)
            q2 = q_ref[:, cs]
            qm = jnp.where((lane >= 64 * hh) & (lane < 64 * hh + 64), q2, jnp.zeros_like(q2))
            scores.append(_nt(qm, k_ref[:, cs]))
        probs = []
        for h in range(H_FOX):
            s = scores[h] - ct_ref[h:h + 1, :]
            if masked:
                s = jnp.where(causal, s, NEG_INF)
            m_prev = m_ref[h]
            m_new = jnp.maximum(m_prev, jnp.max(s, axis=1, keepdims=True))
            alpha = jnp.exp(m_prev - m_new)
            p = jnp.exp(s - jnp.concatenate([m_new] * (tk // LANES), axis=1))
            l_ref[h] = alpha * l_ref[h] + jnp.sum(p, axis=1, keepdims=True)
            m_ref[h] = m_new
            probs.append((p.astype(BF16), alpha))
        for h in range(H_FOX):
            cs = slice((h // 2) * LANES, (h // 2 + 1) * LANES)
            p, alpha = probs[h]
            acc_ref[h] = alpha * acc_ref[h] + _mm(p, v_ref[:, cs])

    @pl.when(ki < qi)
    def _():
        step(False)

    @pl.when(ki == qi)
    def _():
        step(True)
        for pr in range(H_FOX // 2):
            o0 = acc_ref[2 * pr] / l_ref[2 * pr]
            o1 = acc_ref[2 * pr + 1] / l_ref[2 * pr + 1]
            o_ref[:, pr * LANES:(pr + 1) * LANES] = jnp.where(lane < 64, o0, o1)


def _causal_pairs(nq):
    pairs = [(qi, ki) for qi in range(nq) for ki in range(qi + 1)]
    return (jnp.asarray([p[0] for p in pairs], I32), jnp.asarray([p[1] for p in pairs], I32))


def _fox_prompt(pb3, ct, *, tq):
    b, t, _ = pb3.shape
    qt, kt = _causal_pairs(t // tq)
    grid_spec = pltpu.PrefetchScalarGridSpec(
        num_scalar_prefetch=2, grid=(b, qt.shape[0]),
        in_specs=[pl.BlockSpec((None, tq, 256), lambda bi, s, qt, kt: (bi, qt[s], C_FQ // 256)),
                  pl.BlockSpec((None, tq, 256), lambda bi, s, qt, kt: (bi, kt[s], C_FK // 256)),
                  pl.BlockSpec((None, tq, 256), lambda bi, s, qt, kt: (bi, kt[s], C_FV // 256)),
                  pl.BlockSpec((None, 8, tq), lambda bi, s, qt, kt: (bi, 0, kt[s]))],
        out_specs=pl.BlockSpec((None, tq, 256), lambda bi, s, qt, kt: (bi, qt[s], 0)),
        scratch_shapes=[pltpu.VMEM((H_FOX, tq, LANES), F32), pltpu.VMEM((H_FOX, tq, LANES), F32),
                        pltpu.VMEM((H_FOX, tq, LANES), F32)])
    return pl.pallas_call(
        functools.partial(_fox_body, tq=tq, tk=tq),
        grid_spec=grid_spec,
        out_shape=jax.ShapeDtypeStruct((b, t, 256), F32),
        compiler_params=pltpu.CompilerParams(
            dimension_semantics=("parallel", "arbitrary"), vmem_limit_bytes=VMEM_LIMIT),
        name="fox_prompt",
    )(qt, kt, pb3, pb3, pb3, ct)


def _cmp_body(x_ref, pe_ref, w1_ref, w2_ref, o_ref, acc_ref, *, m):
    lo = _iota((m, LANES), 1) < HEAD_DIM
    acc_ref[...] = jnp.zeros(acc_ref.shape, F32)
    for pos in range(NSA_BLOCK):
        x = x_ref[pl.ds(pos, m, stride=NSA_BLOCK), :] + pe_ref[pos:pos + 1, :]
        a = jnp.concatenate([jnp.where(lo, x, 0.0), jnp.where(lo, 0.0, x)], axis=0)
        acc_ref[...] += _mm(a.astype(BF16), w1_ref[pos])
    h = acc_ref[...]
    o_ref[...] = _mm((h * _sigmoid(h)).astype(BF16), w2_ref[...])


def _compress_prompt(kvc, pe2, w1dup, w2):
    _, n, _ = kvc.shape
    m = n // NSA_BLOCK
    by_j = lambda blk: pl.BlockSpec((None,) + blk, lambda j: (j,) + (0,) * len(blk))
    return pl.pallas_call(
        functools.partial(_cmp_body, m=m), grid=(2,),
        in_specs=[by_j((n, LANES)), by_j((NSA_BLOCK, LANES)), by_j((NSA_BLOCK, LANES, 256)),
                  by_j((256, HEAD_DIM))],
        out_specs=by_j((G_NSA * m, HEAD_DIM)),
        out_shape=jax.ShapeDtypeStruct((2, G_NSA * m, HEAD_DIM), F32),
        scratch_shapes=[pltpu.VMEM((G_NSA * m, 256), F32)],
        compiler_params=pltpu.CompilerParams(vmem_limit_bytes=VMEM_LIMIT),
        name="cmp_prompt",
    )(kvc, pe2, w1dup, w2)


def _masked_softmax_cols(s, valid):
    s = jnp.where(valid, s, NEG_INF)
    e = jnp.where(valid, jnp.exp(s - jnp.max(s, axis=0, keepdims=True)), 0.0)
    return e / jnp.maximum(jnp.sum(e, axis=0, keepdims=True), 1e-30)


def _select_blocks(score, nb, topn):
    nbp, n = score.shape
    blk = _iota((nbp, n), 0)
    rank = jnp.zeros((nbp, n), F32)
    for m in range(nb):
        r = score[m:m + 1, :]
        rank = rank + jnp.where((r > score) | ((r == score) & (blk > m)), 1.0, 0.0)
    return jnp.where((rank < topn) & (score >= 0.0), 1.0, 0.0)


def _nsa_body(qt_ref, kt_ref, q_ref, ks_ref, vs_ref, kwc_ref, kwp_ref, vwc_ref, vwp_ref, kc_ref, vc_ref,
              aux_ref, o_ref, sel_ref, m_ref, l_ref, acc_ref, oc_ref, ow_ref, *, tq, nb, topn):
    qi = qt_ref[pl.program_id(1)]
    ki = kt_ref[pl.program_id(1)]
    q0 = qi * tq
    tk = tq
    lane = _iota((tq, LANES), 1)

    def qstack(g):
        q2 = q_ref[:, g * LANES:(g + 1) * LANES].astype(F32)
        q2r = pltpu.roll(q2, 64, 1)
        ing = (lane >= 64 * g) & (lane < 64 * g + 64)
        qa, qb = (q2, q2r) if g == 0 else (q2r, q2)
        return jnp.concatenate([jnp.where(ing, qa, 0.0), jnp.where(ing, qb, 0.0)],
                               axis=0).astype(BF16)

    def biased(raw, g, kd, valid):
        sa = jnp.where(valid, raw[:tq] + ALIBI_SLOPES[2 * g] * kd, NEG_INF)
        sb = jnp.where(valid, raw[tq:] + ALIBI_SLOPES[2 * g + 1] * kd, NEG_INF)
        return jnp.concatenate([sa, sb], axis=0)

    def lanes(x, n):
        return jnp.concatenate([x] * (n // LANES), axis=1)

    @pl.when(ki == 0)
    def _():
        m_ref[...] = jnp.full(m_ref.shape, NEG_INF, F32)
        l_ref[...] = jnp.zeros(l_ref.shape, F32)
        acc_ref[...] = jnp.zeros(acc_ref.shape, F32)
        dqk = _iota((tq, tk), 0) - _iota((tq, tk), 1)
        kd_cur = _iota((1, tk), 1).astype(F32)
        for g in range(G_NSA):
            qs = qstack(g)
            st = _nt(kc_ref[...], qs)
            blk = _iota((NBLK_PAD, 2 * tq), 0)
            col = _iota((NBLK_PAD, 2 * tq), 1)
            qpos = q0 + jnp.where(col >= tq, col - tq, col)
            slope = jnp.where(col < tq, ALIBI_SLOPES[2 * g], ALIBI_SLOPES[2 * g + 1])
            cend = blk * NSA_BLOCK + (NSA_BLOCK - 1)
            st = st + slope * (cend - q0).astype(F32)
            pt = _masked_softmax_cols(st, (cend <= qpos) & (blk < nb))
            oc_ref[g] = _mm(pt.T.astype(BF16), vc_ref[...])
            imp = pt[:, :tq] + pt[:, tq:]
            nbp = -(-nb // 8) * 8
            blk2 = _iota((nbp, tq), 0)
            cur = _div_pow2(q0 + _iota((nbp, tq), 1), NSA_BLOCK)
            forced = (blk2 == 0) | (blk2 == cur) | (blk2 == cur - 1)
            score = jnp.where(blk2 > cur, -1.0, jnp.where(forced, FORCE_SCORE, imp[:nbp]))
            selt = _select_blocks(score, nb, topn)
            if nbp < NBLK_PAD:
                selt = jnp.concatenate([selt, jnp.zeros((NBLK_PAD - nbp, tq), F32)], axis=0)
            sel_ref[g] = selt.T.astype(BF16)
            s1 = biased(_nt(qs, kwc_ref[...]), g, kd_cur, dqk >= 0)
            s2 = biased(_nt(qs, kwp_ref[...]), g, kd_cur - tq, (dqk + tq <= NSA_WINDOW) & (qi > 0))
            mw = jnp.maximum(jnp.max(s1, axis=1, keepdims=True), jnp.max(s2, axis=1, keepdims=True))
            mw = jnp.broadcast_to(mw, (2 * tq, LANES))
            e1 = jnp.exp(s1 - lanes(mw, tk))
            e2 = jnp.exp(s2 - lanes(mw, tk))
            lw = jnp.sum(e1, axis=1, keepdims=True) + jnp.sum(e2, axis=1, keepdims=True)
            ow_ref[g] = (_mm(e1.astype(BF16), vwc_ref[...]) + _mm(e2.astype(BF16), vwp_ref[...])) \
                / jnp.maximum(lw, 1e-30)

    def sel_step(diag):
        expand = (_iota((NBLK_PAD, tk), 0) ==
                  ki * (tk // NSA_BLOCK) + _div_pow2(_iota((NBLK_PAD, tk), 1), NSA_BLOCK)).astype(BF16)
        kd = (_iota((1, tk), 1) + (ki * tk - q0)).astype(F32)
        raws = [(_nt(qstack(g), ks_ref[...]), _mm(sel_ref[g], expand)) for g in range(G_NSA)]
        probs = []
        for g in range(G_NSA):
            raw, picked = raws[g]
            valid = picked > 0.5
            if diag:
                valid = valid & (_iota((tq, tk), 0) >= _iota((tq, tk), 1))
            s = biased(raw, g, kd, valid)
            m_prev = m_ref[g]
            m_new = jnp.maximum(m_prev, jnp.max(s, axis=1, keepdims=True))
            alpha = jnp.exp(m_prev - m_new)
            p = jnp.exp(s - lanes(m_new, tk))
            l_ref[g] = alpha * l_ref[g] + jnp.sum(p, axis=1, keepdims=True)
            m_ref[g] = m_new
            probs.append((p.astype(BF16), alpha))
        for g in range(G_NSA):
            p, alpha = probs[g]
            acc_ref[g] = alpha * acc_ref[g] + _mm(p, vs_ref[...])

    @pl.when(ki < qi)
    def _():
        sel_step(False)

    @pl.when(ki == qi)
    def _():
        sel_step(True)
        for g in range(G_NSA):
            o_s = acc_ref[g] / jnp.maximum(l_ref[g], 1e-30)
            o_c = oc_ref[g]
            o_w = ow_ref[g]
            outs = []
            for hh in range(2):
                h = 2 * g + hh
                rs = slice(hh * tq, (hh + 1) * tq)
                gl = LANE_GATE0 + 3 * h
                outs.append(aux_ref[:, gl:gl + 1] * o_c[rs] + aux_ref[:, gl + 1:gl + 2] * o_s[rs]
                            + aux_ref[:, gl + 2:gl + 3] * o_w[rs])
            oa, ob = outs
            if g == 0:
                ob = pltpu.roll(ob, 64, 1)
            else:
                oa = pltpu.roll(oa, 64, 1)
            o_ref[:, g * LANES:(g + 1) * LANES] = jnp.where(lane < 64, oa, ob)


def _nsa_prompt(pb3, kcmp, vcmp, aux3, *, tq, nb):
    b, t, _ = pb3.shape
    qt, kt = _causal_pairs(t // tq)
    ct = C_NKV // LANES
    cur = lambda c: (lambda bi, s, qt, kt: (bi, qt[s], c))
    prev = lambda c: (lambda bi, s, qt, kt: (bi, jnp.maximum(qt[s] - 1, 0), c))
    kvt = lambda c: (lambda bi, s, qt, kt: (bi, kt[s], c))
    cmp_spec = pl.BlockSpec((None, NBLK_PAD, LANES), lambda bi, s, qt, kt: (bi, 0, 0))
    tile = lambda f: pl.BlockSpec((None, tq, LANES), f)
    grid_spec = pltpu.PrefetchScalarGridSpec(
        num_scalar_prefetch=2, grid=(b, qt.shape[0]),
        in_specs=[pl.BlockSpec((None, tq, 256), lambda bi, s, qt, kt: (bi, qt[s], C_NQ // 256)),
                  tile(kvt(ct + 2)), tile(kvt(ct + 3)),
                  tile(cur(ct + 4)), tile(prev(ct + 4)), tile(cur(ct + 5)), tile(prev(ct + 5)),
                  cmp_spec, cmp_spec,
                  pl.BlockSpec((None, tq, LANES), lambda bi, s, qt, kt: (bi, qt[s], 0))],
        out_specs=pl.BlockSpec((None, tq, 256), lambda bi, s, qt, kt: (bi, qt[s], 0)),
        scratch_shapes=[pltpu.VMEM((G_NSA, tq, NBLK_PAD), BF16),
                        pltpu.VMEM((G_NSA, 2 * tq, LANES), F32), pltpu.VMEM((G_NSA, 2 * tq, LANES), F32),
                        pltpu.VMEM((G_NSA, 2 * tq, LANES), F32),
                        pltpu.VMEM((G_NSA, 2 * tq, LANES), F32),
                        pltpu.VMEM((G_NSA, 2 * tq, LANES), F32)])
    return pl.pallas_call(
        functools.partial(_nsa_body, tq=tq, nb=nb, topn=min(NSA_TOPN, nb)),
        grid_spec=grid_spec,
        out_shape=jax.ShapeDtypeStruct((b, t, 256), F32),
        compiler_params=pltpu.CompilerParams(
            dimension_semantics=("parallel", "arbitrary"), vmem_limit_bytes=VMEM_LIMIT),
        name="nsa_prompt",
    )(qt, kt, pb3, pb3, pb3, pb3, pb3, pb3, pb3, kcmp, vcmp, aux3)


def _mem_attend(mq_ref, mk_ref, mv_ref, rows):
    lane = _iota((rows, LANES), 1)
    n_mem = mk_ref.shape[0]
    scores = []
    for h in range(H_MEM):
        pr, hh = divmod(h, 2)
        cs = slice(pr * LANES, (pr + 1) * LANES)
        q2 = mq_ref[:, cs]
        qm = jnp.where((lane >= 64 * hh) & (lane < 64 * hh + 64), q2, jnp.zeros_like(q2))
        scores.append(_nt(qm, mk_ref[:, cs]))
    weights = []
    for h in range(H_MEM):
        s = scores[h]
        mx = jnp.broadcast_to(jnp.max(s, axis=1, keepdims=True), (rows, LANES))
        e = jnp.exp(s - jnp.concatenate([mx] * (n_mem // LANES), axis=1))
        weights.append((e.astype(BF16), jnp.sum(e, axis=1, keepdims=True)))
    outs = []
    for h in range(H_MEM):
        e, den = weights[h]
        cs = slice((h // 2) * LANES, (h // 2 + 1) * LANES)
        outs.append(_mm(e, mv_ref[:, cs]) / den)
    return [jnp.where(lane < 64, outs[2 * pr], outs[2 * pr + 1]) for pr in range(H_MEM // 2)]


def _mix_out(x, parts, z_of, wout_ref, fg_ref, final):
    y = x
    for c in range(0, len(parts), 2):
        gated = []
        for cc in (c, c + 1):
            z = z_of(cc)
            gated.append((parts[cc] * (z * _sigmoid(z))).astype(BF16))
        y = y + _mm(jnp.concatenate(gated, axis=1), wout_ref[c * LANES:(c + 2) * LANES, :])
    if final:
        y = y * lax.rsqrt(jnp.mean(y * y, axis=-1, keepdims=True) + RMS_EPS) * fg_ref[...]
    return y


def _pool_window_lane(shape):
    grp = _div_pow2(_iota(shape, 1), D_GROUP // len(POOL_WINDOWS))
    wl = jnp.where(grp == 0, POOL_WINDOWS[0], jnp.where(grp == 1, POOL_WINDOWS[1],
                   jnp.where(grp == 2, POOL_WINDOWS[2], POOL_WINDOWS[3])))
    return grp, wl


HALO = 32


def _post_body(oa_ref, ob_ref, pz_ref, halo_ref, mq_ref, mk_ref, mv_ref, x_ref, wout_ref, pw_ref,
               ps_ref, fg_ref, y_ref, s0, s1, s2, s3, *, tm, final):
    i = pl.program_id(1)
    u = pz_ref[:, 0:256]
    s0[0:HALO, :] = jnp.where(i > 0, halo_ref[...], 0.0)
    s0[HALO:HALO + tm, :] = u
    n = tm + HALO
    s1[8:n, :] = s0[8:n, :] + s0[7:n - 1, :]
    s2[16:n, :] = s1[16:n, :] + s1[14:n - 2, :]
    s3[24:n, :] = s2[24:n, :] + s2[20:n - 4, :]
    a16 = s3[HALO:n, :] + s3[HALO - 8:n - 8, :]
    grp, wl = _pool_window_lane((tm, 256))
    tsum = jnp.where(grp == 0, s1[HALO:n, :], jnp.where(grp == 1, s2[HALO:n, :],
                     jnp.where(grp == 2, s3[HALO:n, :], a16)))
    pos = i * tm + _iota((tm, 256), 0)
    cnt = jnp.minimum(pos + 1, wl).astype(F32)
    o_pool = _mm((tsum / cnt - u).astype(BF16), pw_ref[...]) * ps_ref[...]
    o_mem = _mem_attend(mq_ref, mk_ref, mv_ref, tm)
    parts = [oa_ref[:, 0:LANES], oa_ref[:, LANES:256], ob_ref[:, 0:LANES], ob_ref[:, LANES:256],
             o_pool[:, 0:LANES], o_pool[:, LANES:256], o_mem[0], o_mem[1]]
    z_of = lambda c: pz_ref[:, 256 + c * LANES:256 + (c + 1) * LANES]
    y_ref[...] = _mix_out(x_ref[...], parts, z_of, wout_ref, fg_ref, final)


def _post_prompt(oa, ob, pz3, pb3, memb, x3, wout, pw_bd, ps, fg, *, tm, final):
    b, t, d = x3.shape
    nt = t // tm
    row = lambda bi, i: (bi, i, 0)
    const = lambda bi, i: (0, 0)
    return pl.pallas_call(
        functools.partial(_post_body, tm=tm, final=final),
        grid=(b, nt),
        in_specs=[pl.BlockSpec((None, tm, 256), row), pl.BlockSpec((None, tm, 256), row),
                  pl.BlockSpec((None, tm, 1280), row),
                  pl.BlockSpec((None, HALO, 256),
                               lambda bi, i: (bi, jnp.maximum(i * (tm // HALO) - 1, 0), 0)),
                  pl.BlockSpec((None, tm, 256), lambda bi, i: (bi, i, C_MQ // 256)),
                  pl.BlockSpec((None, 256, 256), lambda bi, i: (bi, 0, 0)),
                  pl.BlockSpec((None, 256, 256), lambda bi, i: (bi, 0, 1)),
                  pl.BlockSpec((None, tm, d), row),
                  pl.BlockSpec((d, d), const), pl.BlockSpec((256, 256), const),
                  pl.BlockSpec((1, 256), const), pl.BlockSpec((1, d), const)],
        out_specs=pl.BlockSpec((None, tm, d), row),
        out_shape=jax.ShapeDtypeStruct((b, t, d), F32),
        scratch_shapes=[pltpu.VMEM((tm + HALO, 256), F32)] * 4,
        compiler_params=pltpu.CompilerParams(dimension_semantics=("parallel", "arbitrary"),
                                             vmem_limit_bytes=VMEM_LIMIT),
        name="post_prompt",
    )(oa, ob, pz3, pz3, pb3, memb, memb, x3, wout, pw_bd, ps, fg)


CMP_PITCH = 72
DQ = 4


def _cmp_sample_body(pt_ref, nsa_hbm, pe_ref, w1_ref, w2a_ref, w2b_ref, o_ref, xbuf, acc0, acc1, sem,
                     *, pages, layer):
    j = pl.program_id(0)
    i = pl.program_id(1)
    ni = pl.num_programs(1)
    s = j * ni + i
    m = pages * G_NSA

    def chunk_copy(jj, ii, slot, p, g):
        page = pt_ref[ii * pages + p]
        return pltpu.make_async_copy(
            nsa_hbm.at[layer, page, pl.ds(jj * LANES + g * HEAD_DIM, HEAD_DIM), :],
            xbuf.at[slot, pl.ds((g * pages + p) * CMP_PITCH, HEAD_DIM), :], sem.at[slot])

    def for_chunks(jj, ii, slot, fn):
        def body(p, carry):
            for g in range(G_NSA):
                fn(chunk_copy(jj, ii, slot, p, g), g)
            return carry
        lax.fori_loop(0, pages, body, 0)

    @pl.when(s == 0)
    def _():
        for_chunks(0, 0, 0, lambda c, g: c.start(priority=g))

    @pl.when(s + 1 < 2 * ni)
    def _():
        wrap = i + 1 == ni
        for_chunks(jnp.where(wrap, j + 1, j), jnp.where(wrap, 0, i + 1), (s + 1) % 2,
                   lambda c, g: c.start(priority=g))

    slot = s % 2
    for_chunks(j, i, slot, lambda c, g: c.wait())

    lo = _iota((m, LANES), 1) < HEAD_DIM
    acc0[...] = jnp.zeros(acc0.shape, F32)
    acc1[...] = jnp.zeros(acc1.shape, F32)
    for dq in range(HEAD_DIM // DQ):
        xs = [xbuf[slot, pl.ds(DQ * dq + k, m, stride=CMP_PITCH), :] + pe_ref[DQ * dq + k:DQ * dq + k + 1, :]
              for k in range(DQ)]
        xr = [pltpu.roll(x, HEAD_DIM, 1) for x in xs]
        a0 = jnp.concatenate([jnp.where(lo, xs[0], xr[1]), jnp.where(lo, xs[2], xr[3])], axis=1)
        a1 = jnp.concatenate([jnp.where(lo, xr[0], xs[1]), jnp.where(lo, xr[2], xs[3])], axis=1)
        w = w1_ref[dq]
        acc0[...] += _mm(a0.astype(BF16), w)
        acc1[...] += _mm(a1.astype(BF16), w)
    for blk, acc in enumerate((acc0, acc1)):
        h = acc[...]
        h = (h * _sigmoid(h)).astype(BF16)
        o_ref[pl.ds(blk, pages, stride=PAGE_SIZE // NSA_BLOCK), :] = (
            _mm(h[0:pages], w2a_ref[...]) + _mm(h[pages:m], w2b_ref[...]))


def _compress_sample(page_flat, nsa_t, pe_t, w1q, w2a, w2b, *, pages, layer):
    npages = page_flat.shape[0]
    nsteps = npages // pages
    m = pages * G_NSA
    by_j = lambda blk: pl.BlockSpec((None,) + blk, lambda j, i, pt: (j,) + (0,) * len(blk))
    grid_spec = pltpu.PrefetchScalarGridSpec(
        num_scalar_prefetch=1, grid=(2, nsteps),
        in_specs=[pl.BlockSpec(memory_space=pl.ANY), by_j((HEAD_DIM, LANES)),
                  by_j((HEAD_DIM // DQ, 256, 256)), by_j((256, LANES)), by_j((256, LANES))],
        out_specs=pl.BlockSpec((None, m, LANES), lambda j, i, pt: (j, i, 0)),
        scratch_shapes=[pltpu.VMEM((2, m * CMP_PITCH, LANES), F32), pltpu.VMEM((m, 256), F32),
                        pltpu.VMEM((m, 256), F32), pltpu.SemaphoreType.DMA((2,))])
    return pl.pallas_call(
        functools.partial(_cmp_sample_body, pages=pages, layer=layer),
        grid_spec=grid_spec,
        out_shape=jax.ShapeDtypeStruct((2, npages * G_NSA, LANES), F32),
        compiler_params=pltpu.CompilerParams(dimension_semantics=("arbitrary", "arbitrary"),
                                             vmem_limit_bytes=VMEM_LIMIT),
        name="cmp_sample",
    )(page_flat, nsa_t, pe_t, w1q, w2a, w2b)


HROWS = 16


def _rows_from_lanes(vec_row, pick):
    return jnp.sum(jnp.where(pick, vec_row, 0.0), axis=1, keepdims=True)


def _bf16r(x):
    return x.astype(BF16).astype(F32)


def _att_sample_body(pt_ref, fox_hbm, lf_hbm, nsa_hbm, pb_ref, sf_ref, aux_ref, win_ref, mem_ref,
                     kc_ref, vc_ref, exp_ref, *rest, npg, past, layer):
    o_ref, wout_ref, fbuf, lbuf, sbuf, sem = rest[-6:]
    b = pl.program_id(0)
    nb = pl.num_programs(0)

    def copies(seq, slot):
        out = []
        for i in range(npg):
            page = pt_ref[seq * npg + i]
            out.append(pltpu.make_async_copy(
                fox_hbm.at[layer, page], fbuf.at[slot, i], sem.at[0, slot]))
            out.append(pltpu.make_async_copy(
                lf_hbm.at[layer, page], lbuf.at[slot, pl.ds(i * 8, 8), :], sem.at[1, slot]))
            out.append(pltpu.make_async_copy(
                nsa_hbm.at[layer, page, pl.ds(256, 256), :], sbuf.at[slot, i], sem.at[2, slot]))
        return out

    def slab(buf, r0, r1):
        return jnp.concatenate([buf[slot, i, r0:r1, :].astype(BF16) for i in range(npg)], axis=1)

    @pl.when(b == 0)
    def _():
        for c in copies(0, 0):
            c.start()

    @pl.when(b + 1 < nb)
    def _():
        for c in copies(b + 1, (b + 1) % 2):
            c.start()

    slot = b % 2
    for c in copies(b, slot):
        c.wait()

    row8 = _iota((HROWS, LANES), 0)
    lane8 = _iota((HROWS, LANES), 1)
    aux = aux_ref[...]
    pbr = pb_ref[...].astype(F32)
    sfr = sf_ref[...]

    def head_rows(q256):
        r = _iota((HROWS, 256), 0)
        l = _iota((HROWS, 256), 1)
        return jnp.where(_div_pow2(l, HEAD_DIM) == r, q256, 0.0).astype(BF16)

    def pick_heads(o8):
        r = _iota((HROWS, 256), 0)
        l = _iota((HROWS, 256), 1)
        return jnp.sum(jnp.where(_div_pow2(l, HEAD_DIM) == r, o8, 0.0), axis=0, keepdims=True)

    kt = slab(fbuf, 0, 256)
    vt = slab(fbuf, 256, 512)
    qf = head_rows(pbr[:, C_FQ:C_FK])
    s = _mm(qf, kt)
    lf2 = lbuf[slot]
    r2 = _iota((npg * 8, npg * 8), 0)
    c2 = _iota((npg * 8, npg * 8), 1)
    upper = (_iota((LANES, LANES), 0) > _iota((LANES, LANES), 1)).astype(BF16)
    within = _exact_mm_rhs(lf2, upper)
    tot = jnp.sum(lf2, axis=1, keepdims=True)
    later_pages = ((_mod_pow2(c2, 8) == _mod_pow2(r2, 8))
                   & (_div_pow2(c2, 8) > _div_pow2(r2, 8))).astype(BF16)
    later = _exact_mm(later_pages, jnp.broadcast_to(tot, (npg * 8, LANES)))
    rr = _iota((npg * 8, LANES), 0)
    ll = _iota((npg * 8, LANES), 1)
    lf_new = _rows_from_lanes(aux, ll == _mod_pow2(rr, 8))
    bias = within + later + lf_new
    zpad = jnp.zeros((HROWS - 8, LANES), F32)
    s = jnp.concatenate(
        [s[:, i * PAGE_SIZE:(i + 1) * PAGE_SIZE] + jnp.concatenate([bias[i * 8:(i + 1) * 8, :], zpad], axis=0)
         for i in range(npg)], axis=1)
    knew = _bf16r(sfr[:, 0:256])
    vnew = _bf16r(sfr[:, 256:512])
    s_new = jnp.sum(qf.astype(F32) * knew, axis=1, keepdims=True)
    m = jnp.maximum(jnp.max(s, axis=1, keepdims=True), s_new)
    e = jnp.exp(s - m)
    e_new = jnp.exp(s_new - m)
    den = jnp.sum(e, axis=1, keepdims=True) + e_new
    o8 = (_nt(e.astype(BF16), vt) + _bf16r(e_new) * vnew) / den
    o_ref[:, 0:256] = pick_heads(o8)

    nq = pbr[:, C_NQ:C_NKV]
    t0 = jnp.broadcast_to(nq[:, 0:LANES], (HROWS, LANES))
    t1 = jnp.broadcast_to(nq[:, LANES:256], (HROWS, LANES))
    t0r, t1r = pltpu.roll(t0, 64, 1), pltpu.roll(t1, 64, 1)
    lo = lane8 < 64
    hrows = [jnp.where(lo, t0, 0.0), jnp.where(lo, t0r, 0.0),
             jnp.where(lo, 0.0, t1r), jnp.where(lo, 0.0, t1)]
    qn = jnp.zeros((HROWS, LANES), F32)
    for h in range(H_NSA):
        qn = jnp.where(row8 == h, hrows[h], qn)
    qn_b = qn.astype(BF16)
    slope8 = jnp.zeros((HROWS, 1), F32)
    r81 = _iota((HROWS, 1), 0)
    for h in range(H_NSA):
        slope8 = jnp.where(r81 == h, ALIBI_SLOPES[h], slope8)
    nkv_new = sfr[:, 512:1280]
    nblk = past // NSA_BLOCK

    def new_score(k128):
        return jnp.sum(qn_b.astype(F32) * _bf16r(k128), axis=1, keepdims=True)

    sc = _nt(qn_b, kc_ref[...].astype(BF16))
    cend = _iota((HROWS, nblk), 1) * NSA_BLOCK + (NSA_BLOCK - 1)
    sc = sc - slope8 * (past - cend).astype(F32)
    ec = jnp.exp(sc - jnp.max(sc, axis=1, keepdims=True))
    pc = ec / jnp.maximum(jnp.sum(ec, axis=1, keepdims=True), 1e-30)
    o_c = _mm(pc.astype(BF16), vc_ref[...].astype(BF16))
    imp = jnp.concatenate([pc[2 * g:2 * g + 1] + pc[2 * g + 1:2 * g + 2] for g in range(G_NSA)]
                          + [jnp.zeros((HROWS - G_NSA, nblk), F32)], axis=0)
    if nblk < LANES:
        imp = jnp.concatenate([imp, jnp.zeros((HROWS, LANES - nblk), F32)], axis=1)
    forced = (lane8 == 0) | (lane8 == nblk) | (lane8 == nblk - 1)
    score = jnp.where(lane8 > nblk, -1.0, jnp.where(forced, FORCE_SCORE, imp))
    rank = jnp.zeros((HROWS, LANES), F32)
    for mblk in range(nblk + 1):
        r = score[:, mblk:mblk + 1]
        rank = rank + jnp.where((r > score) | ((r == score) & (lane8 > mblk)), 1.0, 0.0)
    sel_g = jnp.where((rank < min(NSA_TOPN, nblk + 1)) & (score >= 0.0), 1.0, 0.0)
    sel_h = jnp.zeros((HROWS, LANES), F32)
    for h in range(H_NSA):
        sel_h = jnp.where(row8 == h, sel_g[h // 2:h // 2 + 1, :], sel_h)
    valid = _mm(sel_h.astype(BF16), exp_ref[...]) > 0.5
    kst = slab(sbuf, 0, LANES)
    vst = slab(sbuf, LANES, 256)
    ss = _mm(qn_b, kst) - slope8 * (past - _iota((HROWS, past), 1)).astype(F32)
    ss = jnp.where(valid, ss, NEG_INF)
    ss_new = new_score(nkv_new[:, 256:384])
    ms = jnp.maximum(jnp.max(ss, axis=1, keepdims=True), ss_new)
    es = jnp.where(valid, jnp.exp(ss - ms), 0.0)
    es_new = jnp.exp(ss_new - ms)
    o_s = (_nt(es.astype(BF16), vst) + _bf16r(es_new) * _bf16r(nkv_new[:, 384:512])) \
        / (jnp.sum(es, axis=1, keepdims=True) + es_new)
    wl_ = win_ref.shape[1]
    win = win_ref[...]
    sw = _mm(qn_b, win[0:LANES].astype(BF16)) - slope8 * (wl_ - _iota((HROWS, wl_), 1)).astype(F32)
    sw_new = new_score(nkv_new[:, 512:640])
    mw = jnp.maximum(jnp.max(sw, axis=1, keepdims=True), sw_new)
    ew = jnp.exp(sw - mw)
    ew_new = jnp.exp(sw_new - mw)
    o_w = (_nt(ew.astype(BF16), win[LANES:256].astype(BF16))
           + _bf16r(ew_new) * _bf16r(nkv_new[:, 640:768])) / (jnp.sum(ew, axis=1, keepdims=True) + ew_new)
    rw = _iota((256, 256), 0)
    lw_ = _iota((256, 256), 1)
    new_col = jnp.sum(jnp.where(rw == lw_, nkv_new[:, 512:768], 0.0), axis=1, keepdims=True)
    _store_layer(wout_ref, jnp.where(_iota((256, wl_), 1) == wl_ - 1, new_col, pltpu.roll(win, wl_ - 1, 1)),
                 layer == 0)
    gates = [_rows_from_lanes(aux, lane8 == LANE_GATE0 + 3 * row8 + c) for c in range(3)]
    o_n = gates[0] * o_c + gates[1] * o_s + gates[2] * o_w
    o_nr = pltpu.roll(o_n, 64, 1)
    lo1 = _iota((1, LANES), 1) < 64
    o_ref[:, 256:384] = jnp.where(lo1, o_n[0:1], o_nr[1:2])
    o_ref[:, 384:512] = jnp.where(lo1, o_nr[2:3], o_n[3:4])

    qm = head_rows(pbr[:, C_MQ:C_PU])
    sm = _mm(qm, mem_ref[0:256, :].astype(BF16))
    em = jnp.exp(sm - jnp.max(sm, axis=1, keepdims=True))
    pm = em / jnp.sum(em, axis=1, keepdims=True)
    o_ref[:, 512:768] = pick_heads(_nt(pm.astype(BF16), mem_ref[256:512, :].astype(BF16)))


def _exact_mm_rhs(x, m01):
    p1, p2, p3 = _split3(x)
    return _mm(p1, m01) + _mm(p2, m01) + _mm(p3, m01)


def _attend_sample(page_flat, fox_t, lf_t, nsa_t, pb3, sf3, aux3, win_t, mem_t, kcmp, vcmp, expand,
                   *, npg, layer, stacked=()):
    depth = win_t.shape[0]
    db = pb3.shape[0]
    past = npg * PAGE_SIZE
    assert len(stacked) == (1 if layer else 0)
    per_seq = lambda blk: pl.BlockSpec((None,) + blk, lambda b, pt: (b, 0, 0))
    per_seq_layer = lambda blk: pl.BlockSpec((None, None) + blk, lambda b, pt: (layer, b, 0, 0))
    grid_spec = pltpu.PrefetchScalarGridSpec(
        num_scalar_prefetch=1, grid=(db,),
        in_specs=[pl.BlockSpec(memory_space=pl.ANY), pl.BlockSpec(memory_space=pl.ANY),
                  pl.BlockSpec(memory_space=pl.ANY),
                  per_seq((1, N_PB)), per_seq((1, 1280)), per_seq((1, LANES)),
                  per_seq_layer(win_t.shape[2:]), per_seq_layer(mem_t.shape[2:]),
                  per_seq(kcmp.shape[1:]), per_seq(vcmp.shape[1:]),
                  pl.BlockSpec(expand.shape, lambda b, pt: (0, 0))]
        + [pl.BlockSpec(memory_space=pl.ANY)] * len(stacked),
        out_specs=[per_seq((1, 768)),
                   _stacked_spec(depth, layer, (None,) + win_t.shape[2:], lambda b, pt: (b, 0, 0))],
        scratch_shapes=[pltpu.VMEM((2, npg, 512, PAGE_SIZE), F32), pltpu.VMEM((2, npg * 8, LANES), F32),
                        pltpu.VMEM((2, npg, 256, PAGE_SIZE), F32), pltpu.SemaphoreType.DMA((3, 2))])
    return pl.pallas_call(
        functools.partial(_att_sample_body, npg=npg, past=past, layer=layer),
        grid_spec=grid_spec,
        out_shape=[jax.ShapeDtypeStruct((db, 1, 768), F32), jax.ShapeDtypeStruct(win_t.shape, F32)],
        input_output_aliases={12 + k: 1 + k for k in range(len(stacked))},
        compiler_params=pltpu.CompilerParams(dimension_semantics=("arbitrary",),
                                             vmem_limit_bytes=VMEM_LIMIT),
        name="att_sample",
    )(page_flat, fox_t, lf_t, nsa_t, pb3, sf3, aux3, win_t, mem_t, kcmp, vcmp, expand, *stacked)


def _post_sample_body(oatt_ref, pz_ref, sp_ref, x_ref, wout_ref, pw_ref, ps_ref, fg_ref, y_ref,
                      spo_ref, *, rows, past, final):
    u = pz_ref[:, 0:256]
    grp, wl = _pool_window_lane((rows, 256))
    tsum = u
    for j in range(1, POOL_BUF + 1):
        tsum = tsum + jnp.where(wl > j, sp_ref[POOL_BUF - j], 0.0)
    for j in range(POOL_BUF - 1):
        spo_ref[j] = sp_ref[j + 1]
    spo_ref[POOL_BUF - 1] = u
    cnt = jnp.minimum(past + 1, wl).astype(F32)
    o_pool = _mm((tsum / cnt - u).astype(BF16), pw_ref[...]) * ps_ref[...]
    parts = [oatt_ref[:, c * LANES:(c + 1) * LANES] for c in range(4)]
    parts += [o_pool[:, 0:LANES], o_pool[:, LANES:256]]
    parts += [oatt_ref[:, 512:640], oatt_ref[:, 640:768]]
    z_of = lambda c: pz_ref[:, 256 + c * LANES:256 + (c + 1) * LANES]
    y_ref[...] = _mix_out(x_ref[...], parts, z_of, wout_ref, fg_ref, final)


def _post_sample(oatt, pz, pool_t, x2d, wout, pw_bd, ps, fg, *, past, final, layer):
    rows, d = x2d.shape
    full = lambda a: pl.BlockSpec(a.shape, lambda i: (0,) * a.ndim)
    return pl.pallas_call(
        functools.partial(_post_sample_body, rows=rows, past=past, final=final),
        grid=(1,),
        in_specs=[full(oatt), full(pz),
                  pl.BlockSpec((None, POOL_BUF, rows, 256), lambda i: (layer, 0, 0, 0)),
                  full(x2d), full(wout), full(pw_bd), full(ps), full(fg)],
        out_specs=[pl.BlockSpec((rows, d), lambda i: (0, 0)),
                   pl.BlockSpec((POOL_BUF, rows, 256), lambda i: (0, 0, 0))],
        out_shape=[jax.ShapeDtypeStruct((rows, d), F32),
                   jax.ShapeDtypeStruct((POOL_BUF, rows, 256), F32)],
        compiler_params=pltpu.CompilerParams(vmem_limit_bytes=VMEM_LIMIT),
        name="post_sample",
    )(oatt, pz, pool_t, x2d, wout, pw_bd, ps, fg)


def _prep_layer_weights(norm_g, w_in, b_fgt, nsa_pe, nsa_w1, nsa_w2, pool_w, pool_scale,
                        w_mem_kv, w_out):
    d = w_in.shape[0]
    o = 0
    segs = {}
    for name, width in (("fqkv", 768), ("ff", H_FOX), ("nq", 256), ("nkv", 768),
                        ("ng", 3 * H_NSA), ("pu", 256), ("mq", 256), ("z", d)):
        segs[name] = w_in[:, o:o + width]
        o += width
    pad = jnp.zeros((d, LANES - H_FOX - 3 * H_NSA), w_in.dtype)
    wp = jnp.concatenate([segs["fqkv"], segs["nq"], segs["nkv"], segs["mq"], segs["pu"],
                          segs["z"], segs["ff"], segs["ng"], pad], axis=1).astype(BF16)
    bvec = jnp.zeros((1, LANES), F32).at[0, 0:H_FOX].set(b_fgt.astype(F32))
    gw = D_GROUP // len(POOL_WINDOWS)
    pw_bd = jnp.zeros((D_GROUP, D_GROUP), F32)
    for g in range(len(POOL_WINDOWS)):
        pw_bd = pw_bd.at[g * gw:(g + 1) * gw, g * gw:(g + 1) * gw].set(pool_w[g])
    return dict(g=norm_g.reshape(1, d), wp=wp, bvec=bvec,
                pe2=jnp.concatenate([nsa_pe, nsa_pe], axis=2).astype(F32),
                w1dup=jnp.concatenate([nsa_w1.reshape(2, NSA_BLOCK, HEAD_DIM, -1)] * 2, axis=2).astype(BF16),
                w2=nsa_w2.astype(BF16),
                w1q=jnp.transpose(nsa_w1.reshape(2, NSA_BLOCK, HEAD_DIM, -1), (0, 2, 1, 3)).reshape(
                    2, HEAD_DIM // DQ, DQ * NSA_BLOCK, -1).astype(BF16),
                pe_t=jnp.concatenate([jnp.transpose(nsa_pe, (0, 2, 1))] * 2, axis=2).astype(F32),
                w2a=jnp.pad(nsa_w2, ((0, 0), (0, 0), (0, HEAD_DIM))).astype(BF16),
                w2b=jnp.pad(nsa_w2, ((0, 0), (0, 0), (HEAD_DIM, 0))).astype(BF16),
                pw_bd=pw_bd.astype(BF16), ps=pool_scale.reshape(1, D_GROUP).astype(F32),
                wmkv=w_mem_kv.astype(BF16), wout=w_out.astype(BF16))


def _layer_prompt(x, mem, lw, fg, tile, layer, depth, stacked):
    b, t, d = x.shape
    n = b * t
    nb = t // NSA_BLOCK
    final = layer == depth - 1
    pb, pz, aux, kvc, ct, lft, fkvt, nkvt, nwt = _project(
        x.reshape(n, d), lw["g"], lw["wp"], lw["bvec"], tm=tile, tiles_per_seq=t // tile, prompt=True,
        layer=layer, depth=depth, stacked=stacked)
    pb3 = pb.reshape(b, t, N_PB)
    o_a = _fox_prompt(pb3, ct, tq=tile)
    cmp = _compress_prompt(kvc, lw["pe2"], lw["w1dup"], lw["w2"]).reshape(2, G_NSA, b, nb, HEAD_DIM)
    cmp = jnp.transpose(cmp, (0, 2, 3, 1, 4)).reshape(2, b, nb, LANES)
    cmp = jnp.pad(cmp, ((0, 0), (0, 0), (0, NBLK_PAD - nb), (0, 0))).astype(BF16)
    o_b = _nsa_prompt(pb3, cmp[0], cmp[1], aux.reshape(b, t, LANES), tq=tile, nb=nb)
    n_mem = mem.shape[1]
    mem_t, memb = _mem_project(mem, lw["wmkv"])
    pz3 = pz.reshape(b, t, 1280)
    y = _post_prompt(o_a, o_b, pz3, pb3, memb, x, lw["wout"], lw["pw_bd"], lw["ps"], fg,
                     tm=tile, final=final)
    w_keep = min(NSA_WINDOW, t)
    state = (jnp.transpose(lft[:, 0:H_FOX, :], (0, 2, 1)),
             jnp.transpose(nwt[:, :, t - w_keep:].reshape(b, 2, G_NSA, HEAD_DIM, w_keep), POS_FIRST),
             pz3[:, t - POOL_BUF:, 0:256],
             jnp.transpose(mem_t.reshape(b, 2, H_MEM, HEAD_DIM, n_mem), POS_FIRST))
    return y, state, (fkvt, nkvt)


POS_FIRST = (0, 4, 1, 2, 3)


def _native_views(cache_fox_kv, cache_fox_lf, cache_nsa_kv, state_nsa_win, state_pool, cache_mem_kv):
    dp, n_phys = cache_fox_kv.shape[:2]
    db = state_nsa_win.shape[1]
    to_t = (0, 1, 3, 4, 5, 2)
    fox_t = jnp.transpose(cache_fox_kv, to_t).reshape(dp, n_phys, 512, PAGE_SIZE)
    lf_t = jnp.pad(jnp.transpose(cache_fox_lf.astype(F32), (0, 1, 3, 2)),
                   ((0, 0), (0, 0), (0, 8 - H_FOX), (0, 0)))
    nsa_t = jnp.transpose(cache_nsa_kv, to_t).reshape(dp, n_phys, 512, PAGE_SIZE)
    win_t = jnp.transpose(state_nsa_win, to_t).reshape(dp, db, 256, state_nsa_win.shape[2])
    mem_t = jnp.transpose(cache_mem_kv, to_t).reshape(dp, db, 512, cache_mem_kv.shape[2])
    pool_t = jnp.transpose(state_pool, (0, 2, 1, 3))
    return dict(fox_t=fox_t, lf_t=lf_t, nsa_t=nsa_t, win_t=win_t, mem_t=mem_t, pool_t=pool_t)


def _layer_sample(x, nv, page_table, lw, fg, layer, depth, stacked):
    db, ds, d = x.shape
    npg = page_table.shape[1]
    past = npg * PAGE_SIZE
    nblk = past // NSA_BLOCK
    final = layer == depth - 1
    pb, pz, aux, sf, sft, auxt = _project(x.reshape(db, d), lw["g"], lw["wp"], lw["bvec"],
                                          tm=db, tiles_per_seq=1, prompt=False)
    page_flat = page_table.reshape(db * npg).astype(I32)
    cmp = _compress_sample(page_flat, nv["nsa_t"], lw["pe_t"], lw["w1q"], lw["w2a"], lw["w2b"],
                           pages=min(db * npg, 128), layer=layer)
    cmp = cmp.reshape(2, db, nblk, LANES)
    kcmp, vcmp = cmp[0], cmp[1]
    expand = (np.arange(LANES)[:, None] == (np.arange(past)[None, :] // NSA_BLOCK)).astype(np.float32)
    oatt, win_new = _attend_sample(
        page_flat, nv["fox_t"], nv["lf_t"], nv["nsa_t"], pb.reshape(db, 1, N_PB),
        sf.reshape(db, 1, 1280), aux.reshape(db, 1, LANES), nv["win_t"], nv["mem_t"], kcmp, vcmp,
        jnp.asarray(expand, BF16), npg=npg, layer=layer, stacked=stacked)
    y, pool_new = _post_sample(oatt.reshape(db, 768), pz, nv["pool_t"], x.reshape(db, d), lw["wout"],
                               lw["pw_bd"], lw["ps"], fg, past=past, final=final, layer=layer)
    seq_first = (3, 0, 1, 2)
    state = (jnp.transpose(sft[0:512].reshape(2, H_FOX, HEAD_DIM, db), seq_first)[:, None],
             jnp.transpose(auxt[0:H_FOX])[:, None, :],
             jnp.transpose(sft[512:1024].reshape(4, G_NSA, HEAD_DIM, db), seq_first)[:, None],
             jnp.transpose(pool_new, (1, 0, 2)))
    return y.reshape(db, ds, d), state, (win_new,)


def _from_t(x_t, a, b):
    dp, bt, _, pos = x_t.shape
    return jnp.transpose(x_t.reshape(dp, bt, a, b, HEAD_DIM, pos), (0, 1, 5, 2, 3, 4))


def kernel(x_prompt, x_sample, cache_fox_kv, cache_fox_lf, cache_nsa_kv, state_nsa_win, state_pool,
           cache_mem_kv, page_table, mem_prompt, norm_g, w_in, b_fgt, nsa_pe, nsa_w1, nsa_w2, pool_w,
           pool_scale, w_mem_kv, w_out, final_g):
    depth = norm_g.shape[0]
    t = x_prompt.shape[1]
    assert x_sample.shape[1] == 1 and t % NSA_WINDOW == 0 and t // NSA_BLOCK <= NBLK_PAD
    hp, hs = x_prompt, x_sample
    fg = final_g.reshape(1, -1).astype(F32)
    nv = _native_views(cache_fox_kv, cache_fox_lf, cache_nsa_kv, state_nsa_win, state_pool,
                       cache_mem_kv)
    sp = [[] for _ in range(4)]
    ss = [[] for _ in range(4)]
    stk_p, stk_s = (), ()
    for l in range(depth):
        lw = _prep_layer_weights(norm_g[l], w_in[l], b_fgt[l], nsa_pe[l], nsa_w1[l], nsa_w2[l],
                                 pool_w[l], pool_scale[l], w_mem_kv[l], w_out[l])
        hp, st_p, stk_p = _layer_prompt(hp, mem_prompt, lw, fg, NSA_WINDOW, l, depth, stk_p)
        hs, st_s, stk_s = _layer_sample(hs, nv, page_table, lw, fg, l, depth, stk_s)
        for acc, s in zip(sp, st_p):
            acc.append(s)
        for acc, s in zip(ss, st_s):
            acc.append(s)
    fox_lf_p, nsa_win_p, pool_p, mem_kv_p = (jnp.stack(a) for a in sp)
    fox_kv_s, fox_lf_s, nsa_kv_s, pool_s = (jnp.stack(a) for a in ss)
    return (hp, hs, _from_t(stk_p[0], 2, H_FOX), fox_lf_p, _from_t(stk_p[1], 4, G_NSA), nsa_win_p, pool_p,
            mem_kv_p, fox_kv_s, fox_lf_s, nsa_kv_s, _from_t(stk_s[0], 2, G_NSA), pool_s)
```
